```python
import jax, jax.numpy as jnp
from jax import lax
import numpy as np

D_MODEL = 1024
BATCH = 8
SEQ = 16384
DEPTH = 4

D_FF = 2816
LRU_WIDTH = 1024
LRU_HEADS = 4
LRU_HEAD_DIM = LRU_WIDTH // LRU_HEADS
LRU_CONV = 4
LRU_PAD = (2, 1)
LRU_C = 8.0
SC_WIDTH = 512
SC_CONV = 3
SC_PAD = (1, 1)
SGU_WIDTH = 512
SGU_HEADS = 4
SGU_HEAD_DIM = SGU_WIDTH // SGU_HEADS
CHUNK = 128
N_BRANCH = 3
EPS = 1e-6

_PART = (LRU_WIDTH, LRU_WIDTH, SC_WIDTH, SC_WIDTH, SC_WIDTH,
         SGU_WIDTH, SGU_WIDTH, N_BRANCH * D_MODEL)
D_IN = sum(_PART)
SPLIT_POINTS = tuple(int(p) for p in np.cumsum(_PART)[:-1])

kernel_name = "hybrid_rglru_shortconv_sgu_encoder"


def rmsnorm(x, g):
    xf = x.astype(jnp.float32)
    y = xf * lax.rsqrt(jnp.mean(xf * xf, axis=-1, keepdims=True) + EPS)
    return y.astype(x.dtype) * g


def layernorm(x, g, b):
    xf = x.astype(jnp.float32)
    mu = jnp.mean(xf, axis=-1, keepdims=True)
    var = jnp.mean(jnp.square(xf - mu), axis=-1, keepdims=True)
    return ((xf - mu) * lax.rsqrt(var + EPS)).astype(x.dtype) * g + b


def swiglu(h, w_gate, w_up, w_down):
    return (jax.nn.silu(h @ w_gate) * (h @ w_up)) @ w_down


def depthwise_conv(x, w, pad):
    c = x.shape[-1]
    return lax.conv_general_dilated(
        x, w[:, None, :], window_strides=(1,), padding=[pad],
        dimension_numbers=("NWC", "WIO", "NWC"), feature_group_count=c)


def _lin_combine(left, right):
    a_l, b_l = left
    a_r, b_r = right
    return a_l * a_r, a_r * b_l + b_r


def rg_lru(x, w_a, b_a, w_x, b_x, lam, reverse):
    bsz, s, wdt = x.shape
    xh = x.reshape(bsz, s, LRU_HEADS, LRU_HEAD_DIM)
    r = jax.nn.sigmoid(jnp.einsum("bshd,hde->bshe", xh, w_a).reshape(bsz, s, wdt) + b_a)
    i = jax.nn.sigmoid(jnp.einsum("bshd,hde->bshe", xh, w_x).reshape(bsz, s, wdt) + b_x)
    log_a = (-LRU_C * jax.nn.softplus(-lam.astype(jnp.float32))) * r.astype(jnp.float32)
    a = jnp.exp(log_a)
    u = (i * x).astype(jnp.float32) * jnp.sqrt(-jnp.expm1(2.0 * log_a))
    _, h = lax.associative_scan(_lin_combine, (a, u), reverse=reverse, axis=1)
    return h.astype(x.dtype)


def spatial_gating(u, v, ln_g, ln_b, w_s, b_s):
    u = jax.nn.gelu(u)
    v = layernorm(jax.nn.gelu(v), ln_g, ln_b)
    bsz, s, _ = v.shape
    vc = v.reshape(bsz, s // CHUNK, CHUNK, SGU_HEADS, SGU_HEAD_DIM)
    mixed = jnp.einsum("gpq,bnqgc->bnpgc", w_s, vc) + b_s.T[:, :, None]
    return u * mixed.reshape(bsz, s, SGU_WIDTH)


def mixer_block(h, w_in, lru_conv_w, lru_conv_b, lru_wa, lru_ba, lru_wx, lru_bx,
                lru_lambda, lru_w_out, sc_conv_w, sc_w_out, sgu_ln_g, sgu_ln_b,
                sgu_w_s, sgu_b, sgu_w_out, w_o):
    z = h @ w_in
    lru_gate, lru_x, sc_b, sc_c, sc_x, sgu_u, sgu_v, merge = jnp.split(z, SPLIT_POINTS, axis=-1)
    xc = depthwise_conv(lru_x, lru_conv_w, LRU_PAD) + lru_conv_b
    h_fwd = rg_lru(xc, lru_wa[0], lru_ba[0], lru_wx[0], lru_bx[0], lru_lambda[0], reverse=False)
    h_bwd = rg_lru(xc, lru_wa[1], lru_ba[1], lru_wx[1], lru_bx[1], lru_lambda[1], reverse=True)
    y_a = ((h_fwd + h_bwd) * jax.nn.gelu(lru_gate)) @ lru_w_out
    y_b = (sc_b * depthwise_conv(sc_c * sc_x, sc_conv_w, SC_PAD)) @ sc_w_out
    y_c = spatial_gating(sgu_u, sgu_v, sgu_ln_g, sgu_ln_b, sgu_w_s, sgu_b) @ sgu_w_out
    g = jax.nn.sigmoid(merge).reshape(*merge.shape[:-1], N_BRANCH, D_MODEL)
    m = g[..., 0, :] * y_a + g[..., 1, :] * y_b + g[..., 2, :] * y_c
    return m @ w_o


def _fwd_setup_inputs(seed: int = 0) -> dict:
    key = jax.random.key(seed)
    ks = iter(jax.random.split(key, 32))
    L, D, F = DEPTH, D_MODEL, D_FF

    def w(shape, fan_in):
        return jax.random.normal(next(ks), shape, jnp.float32) * (fan_in ** -0.5)

    def gain(shape):
        return 1.0 + 0.02 * jax.random.normal(next(ks), shape, jnp.float32)

    def bias(shape):
        return 0.02 * jax.random.normal(next(ks), shape, jnp.float32)

    a0 = jax.random.uniform(next(ks), (L, 2, LRU_WIDTH), jnp.float32, 0.9, 0.999)
    s0 = a0 ** (1.0 / LRU_C)
    lru_lambda = jnp.log(s0) - jnp.log1p(-s0)
    return {
        "x": jax.random.normal(next(ks), (BATCH, SEQ, D), jnp.float32),
        "ffn1_pre_g": gain((L, D)),
        "ffn1_w_gate": w((L, D, F), D),
        "ffn1_w_up": w((L, D, F), D),
        "ffn1_w_down": w((L, F, D), F),
        "ffn1_post_g": gain((L, D)),
        "mix_pre_g": gain((L, D)),
        "w_in": w((L, D, D_IN), D),
        "lru_conv_w": w((L, LRU_CONV, LRU_WIDTH), LRU_CONV),
        "lru_conv_b": bias((L, LRU_WIDTH)),
        "lru_wa": w((L, 2, LRU_HEADS, LRU_HEAD_DIM, LRU_HEAD_DIM), LRU_HEAD_DIM),
        "lru_ba": bias((L, 2, LRU_WIDTH)),
        "lru_wx": w((L, 2, LRU_HEADS, LRU_HEAD_DIM, LRU_HEAD_DIM), LRU_HEAD_DIM),
        "lru_bx": bias((L, 2, LRU_WIDTH)),
        "lru_lambda": lru_lambda,
        "lru_w_out": w((L, LRU_WIDTH, D), LRU_WIDTH),
        "sc_conv_w": w((L, SC_CONV, SC_WIDTH), SC_CONV),
        "sc_w_out": w((L, SC_WIDTH, D), SC_WIDTH),
        "sgu_ln_g": gain((L, SGU_WIDTH)),
        "sgu_ln_b": bias((L, SGU_WIDTH)),
        "sgu_w_s": w((L, SGU_HEADS, CHUNK, CHUNK), CHUNK),
        "sgu_b": bias((L, SGU_HEADS, CHUNK)),
        "sgu_w_out": w((L, SGU_WIDTH, D), SGU_WIDTH),
        "w_o": w((L, D, D), D),
        "mix_post_g": gain((L, D)),
        "ffn2_pre_g": gain((L, D)),
        "ffn2_w_gate": w((L, D, F), D),
        "ffn2_w_up": w((L, D, F), D),
        "ffn2_w_down": w((L, F, D), F),
        "ffn2_post_g": gain((L, D)),
    }


def _fwd_reference(x, ffn1_pre_g, ffn1_w_gate, ffn1_w_up, ffn1_w_down, ffn1_post_g,
              mix_pre_g, w_in, lru_conv_w, lru_conv_b, lru_wa, lru_ba, lru_wx, lru_bx,
              lru_lambda, lru_w_out, sc_conv_w, sc_w_out, sgu_ln_g, sgu_ln_b,
              sgu_w_s, sgu_b, sgu_w_out, w_o, mix_post_g,
              ffn2_pre_g, ffn2_w_gate, ffn2_w_up, ffn2_w_down, ffn2_post_g):
    for l in range(DEPTH):
        f1 = swiglu(rmsnorm(x, ffn1_pre_g[l]), ffn1_w_gate[l], ffn1_w_up[l], ffn1_w_down[l])
        x = x + 0.5 * rmsnorm(f1, ffn1_post_g[l])
        mx = mixer_block(rmsnorm(x, mix_pre_g[l]), w_in[l], lru_conv_w[l], lru_conv_b[l],
                         lru_wa[l], lru_ba[l], lru_wx[l], lru_bx[l], lru_lambda[l],
                         lru_w_out[l], sc_conv_w[l], sc_w_out[l], sgu_ln_g[l], sgu_ln_b[l],
                         sgu_w_s[l], sgu_b[l], sgu_w_out[l], w_o[l])
        x = x + rmsnorm(mx, mix_post_g[l])
        f2 = swiglu(rmsnorm(x, ffn2_pre_g[l]), ffn2_w_gate[l], ffn2_w_up[l], ffn2_w_down[l])
        x = x + 0.5 * rmsnorm(f2, ffn2_post_g[l])
    return x


import jax as _jax
import jax.numpy as _jnp

TWIN_FORMAT = 'train_step'
FWD_PARAMS = ['x', 'ffn1_pre_g', 'ffn1_w_gate', 'ffn1_w_up', 'ffn1_w_down', 'ffn1_post_g', 'mix_pre_g', 'w_in', 'lru_conv_w', 'lru_conv_b', 'lru_wa', 'lru_ba', 'lru_wx', 'lru_bx', 'lru_lambda', 'lru_w_out', 'sc_conv_w', 'sc_w_out', 'sgu_ln_g', 'sgu_ln_b', 'sgu_w_s', 'sgu_b', 'sgu_w_out', 'w_o', 'mix_post_g', 'ffn2_pre_g', 'ffn2_w_gate', 'ffn2_w_up', 'ffn2_w_down', 'ffn2_post_g']
TWIN_WEIGHTS = ['ffn1_pre_g', 'ffn1_w_gate', 'ffn1_w_up', 'ffn1_w_down', 'ffn1_post_g', 'mix_pre_g', 'w_in', 'lru_conv_w', 'lru_conv_b', 'lru_wa', 'lru_ba', 'lru_wx', 'lru_bx', 'lru_lambda', 'lru_w_out', 'sc_conv_w', 'sc_w_out', 'sgu_ln_g', 'sgu_ln_b', 'sgu_w_s', 'sgu_b', 'sgu_w_out', 'w_o', 'mix_post_g', 'ffn2_pre_g', 'ffn2_w_gate', 'ffn2_w_up', 'ffn2_w_down', 'ffn2_post_g']
TWIN_DIFF_INPUT = 'x'
TWIN_INPUTS = ['x', 'ffn1_pre_g', 'ffn1_w_gate', 'ffn1_w_up', 'ffn1_w_down', 'ffn1_post_g', 'mix_pre_g', 'w_in', 'lru_conv_w', 'lru_conv_b', 'lru_wa', 'lru_ba', 'lru_wx', 'lru_bx', 'lru_lambda', 'lru_w_out', 'sc_conv_w', 'sc_w_out', 'sgu_ln_g', 'sgu_ln_b', 'sgu_w_s', 'sgu_b', 'sgu_w_out', 'w_o', 'mix_post_g', 'ffn2_pre_g', 'ffn2_w_gate', 'ffn2_w_up', 'ffn2_w_down', 'ffn2_post_g', 'loss_target', 'm_ffn1_pre_g', 'm_ffn1_w_gate', 'm_ffn1_w_up', 'm_ffn1_w_down', 'm_ffn1_post_g', 'm_mix_pre_g', 'm_w_in', 'm_lru_conv_w', 'm_lru_conv_b', 'm_lru_wa', 'm_lru_ba', 'm_lru_wx', 'm_lru_bx', 'm_lru_lambda', 'm_lru_w_out', 'm_sc_conv_w', 'm_sc_w_out', 'm_sgu_ln_g', 'm_sgu_ln_b', 'm_sgu_w_s', 'm_sgu_b', 'm_sgu_w_out', 'm_w_o', 'm_mix_post_g', 'm_ffn2_pre_g', 'm_ffn2_w_gate', 'm_ffn2_w_up', 'm_ffn2_w_down', 'm_ffn2_post_g', 'v_ffn1_pre_g', 'v_ffn1_w_gate', 'v_ffn1_w_up', 'v_ffn1_w_down', 'v_ffn1_post_g', 'v_mix_pre_g', 'v_w_in', 'v_lru_conv_w', 'v_lru_conv_b', 'v_lru_wa', 'v_lru_ba', 'v_lru_wx', 'v_lru_bx', 'v_lru_lambda', 'v_lru_w_out', 'v_sc_conv_w', 'v_sc_w_out', 'v_sgu_ln_g', 'v_sgu_ln_b', 'v_sgu_w_s', 'v_sgu_b', 'v_sgu_w_out', 'v_w_o', 'v_mix_post_g', 'v_ffn2_pre_g', 'v_ffn2_w_gate', 'v_ffn2_w_up', 'v_ffn2_w_down', 'v_ffn2_post_g']
TWIN_OUTPUTS = ['loss', 'grad_x', 'grad_ffn1_pre_g', 'grad_ffn1_w_gate', 'grad_ffn1_w_up', 'grad_ffn1_w_down', 'grad_ffn1_post_g', 'grad_mix_pre_g', 'grad_w_in', 'grad_lru_conv_w', 'grad_lru_conv_b', 'grad_lru_wa', 'grad_lru_ba', 'grad_lru_wx', 'grad_lru_bx', 'grad_lru_lambda', 'grad_lru_w_out', 'grad_sc_conv_w', 'grad_sc_w_out', 'grad_sgu_ln_g', 'grad_sgu_ln_b', 'grad_sgu_w_s', 'grad_sgu_b', 'grad_sgu_w_out', 'grad_w_o', 'grad_mix_post_g', 'grad_ffn2_pre_g', 'grad_ffn2_w_gate', 'grad_ffn2_w_up', 'grad_ffn2_w_down', 'grad_ffn2_post_g', 'delta_ffn1_pre_g', 'delta_ffn1_w_gate', 'delta_ffn1_w_up', 'delta_ffn1_w_down', 'delta_ffn1_post_g', 'delta_mix_pre_g', 'delta_w_in', 'delta_lru_conv_w', 'delta_lru_conv_b', 'delta_lru_wa', 'delta_lru_ba', 'delta_lru_wx', 'delta_lru_bx', 'delta_lru_lambda', 'delta_lru_w_out', 'delta_sc_conv_w', 'delta_sc_w_out', 'delta_sgu_ln_g', 'delta_sgu_ln_b', 'delta_sgu_w_s', 'delta_sgu_b', 'delta_sgu_w_out', 'delta_w_o', 'delta_mix_post_g', 'delta_ffn2_pre_g', 'delta_ffn2_w_gate', 'delta_ffn2_w_up', 'delta_ffn2_w_down', 'delta_ffn2_post_g', 'new_m_ffn1_pre_g', 'new_m_ffn1_w_gate', 'new_m_ffn1_w_up', 'new_m_ffn1_w_down', 'new_m_ffn1_post_g', 'new_m_mix_pre_g', 'new_m_w_in', 'new_m_lru_conv_w', 'new_m_lru_conv_b', 'new_m_lru_wa', 'new_m_lru_ba', 'new_m_lru_wx', 'new_m_lru_bx', 'new_m_lru_lambda', 'new_m_lru_w_out', 'new_m_sc_conv_w', 'new_m_sc_w_out', 'new_m_sgu_ln_g', 'new_m_sgu_ln_b', 'new_m_sgu_w_s', 'new_m_sgu_b', 'new_m_sgu_w_out', 'new_m_w_o', 'new_m_mix_post_g', 'new_m_ffn2_pre_g', 'new_m_ffn2_w_gate', 'new_m_ffn2_w_up', 'new_m_ffn2_w_down', 'new_m_ffn2_post_g', 'new_v_ffn1_pre_g', 'new_v_ffn1_w_gate', 'new_v_ffn1_w_up', 'new_v_ffn1_w_down', 'new_v_ffn1_post_g', 'new_v_mix_pre_g', 'new_v_w_in', 'new_v_lru_conv_w', 'new_v_lru_conv_b', 'new_v_lru_wa', 'new_v_lru_ba', 'new_v_lru_wx', 'new_v_lru_bx', 'new_v_lru_lambda', 'new_v_lru_w_out', 'new_v_sc_conv_w', 'new_v_sc_w_out', 'new_v_sgu_ln_g', 'new_v_sgu_ln_b', 'new_v_sgu_w_s', 'new_v_sgu_b', 'new_v_sgu_w_out', 'new_v_w_o', 'new_v_mix_post_g', 'new_v_ffn2_pre_g', 'new_v_ffn2_w_gate', 'new_v_ffn2_w_up', 'new_v_ffn2_w_down', 'new_v_ffn2_post_g']
TWIN_LEAF_KINDS = {'loss': 'loss', 'grad_x': 'grad_x', 'grad_ffn1_pre_g': 'grad_w', 'grad_ffn1_w_gate': 'grad_w', 'grad_ffn1_w_up': 'grad_w', 'grad_ffn1_w_down': 'grad_w', 'grad_ffn1_post_g': 'grad_w', 'grad_mix_pre_g': 'grad_w', 'grad_w_in': 'grad_w', 'grad_lru_conv_w': 'grad_w', 'grad_lru_conv_b': 'grad_w', 'grad_lru_wa': 'grad_w', 'grad_lru_ba': 'grad_w', 'grad_lru_wx': 'grad_w', 'grad_lru_bx': 'grad_w', 'grad_lru_lambda': 'grad_w', 'grad_lru_w_out': 'grad_w', 'grad_sc_conv_w': 'grad_w', 'grad_sc_w_out': 'grad_w', 'grad_sgu_ln_g': 'grad_w', 'grad_sgu_ln_b': 'grad_w', 'grad_sgu_w_s': 'grad_w', 'grad_sgu_b': 'grad_w', 'grad_sgu_w_out': 'grad_w', 'grad_w_o': 'grad_w', 'grad_mix_post_g': 'grad_w', 'grad_ffn2_pre_g': 'grad_w', 'grad_ffn2_w_gate': 'grad_w', 'grad_ffn2_w_up': 'grad_w', 'grad_ffn2_w_down': 'grad_w', 'grad_ffn2_post_g': 'grad_w', 'delta_ffn1_pre_g': 'delta_w', 'delta_ffn1_w_gate': 'delta_w', 'delta_ffn1_w_up': 'delta_w', 'delta_ffn1_w_down': 'delta_w', 'delta_ffn1_post_g': 'delta_w', 'delta_mix_pre_g': 'delta_w', 'delta_w_in': 'delta_w', 'delta_lru_conv_w': 'delta_w', 'delta_lru_conv_b': 'delta_w', 'delta_lru_wa': 'delta_w', 'delta_lru_ba': 'delta_w', 'delta_lru_wx': 'delta_w', 'delta_lru_bx': 'delta_w', 'delta_lru_lambda': 'delta_w', 'delta_lru_w_out': 'delta_w', 'delta_sc_conv_w': 'delta_w', 'delta_sc_w_out': 'delta_w', 'delta_sgu_ln_g': 'delta_w', 'delta_sgu_ln_b': 'delta_w', 'delta_sgu_w_s': 'delta_w', 'delta_sgu_b': 'delta_w', 'delta_sgu_w_out': 'delta_w', 'delta_w_o': 'delta_w', 'delta_mix_post_g': 'delta_w', 'delta_ffn2_pre_g': 'delta_w', 'delta_ffn2_w_gate': 'delta_w', 'delta_ffn2_w_up': 'delta_w', 'delta_ffn2_w_down': 'delta_w', 'delta_ffn2_post_g': 'delta_w', 'new_m_ffn1_pre_g': 'new_m', 'new_m_ffn1_w_gate': 'new_m', 'new_m_ffn1_w_up': 'new_m', 'new_m_ffn1_w_down': 'new_m', 'new_m_ffn1_post_g': 'new_m', 'new_m_mix_pre_g': 'new_m', 'new_m_w_in': 'new_m', 'new_m_lru_conv_w': 'new_m', 'new_m_lru_conv_b': 'new_m', 'new_m_lru_wa': 'new_m', 'new_m_lru_ba': 'new_m', 'new_m_lru_wx': 'new_m', 'new_m_lru_bx': 'new_m', 'new_m_lru_lambda': 'new_m', 'new_m_lru_w_out': 'new_m', 'new_m_sc_conv_w': 'new_m', 'new_m_sc_w_out': 'new_m', 'new_m_sgu_ln_g': 'new_m', 'new_m_sgu_ln_b': 'new_m', 'new_m_sgu_w_s': 'new_m', 'new_m_sgu_b': 'new_m', 'new_m_sgu_w_out': 'new_m', 'new_m_w_o': 'new_m', 'new_m_mix_post_g': 'new_m', 'new_m_ffn2_pre_g': 'new_m', 'new_m_ffn2_w_gate': 'new_m', 'new_m_ffn2_w_up': 'new_m', 'new_m_ffn2_w_down': 'new_m', 'new_m_ffn2_post_g': 'new_m', 'new_v_ffn1_pre_g': 'new_v', 'new_v_ffn1_w_gate': 'new_v', 'new_v_ffn1_w_up': 'new_v', 'new_v_ffn1_w_down': 'new_v', 'new_v_ffn1_post_g': 'new_v', 'new_v_mix_pre_g': 'new_v', 'new_v_w_in': 'new_v', 'new_v_lru_conv_w': 'new_v', 'new_v_lru_conv_b': 'new_v', 'new_v_lru_wa': 'new_v', 'new_v_lru_ba': 'new_v', 'new_v_lru_wx': 'new_v', 'new_v_lru_bx': 'new_v', 'new_v_lru_lambda': 'new_v', 'new_v_lru_w_out': 'new_v', 'new_v_sc_conv_w': 'new_v', 'new_v_sc_w_out': 'new_v', 'new_v_sgu_ln_g': 'new_v', 'new_v_sgu_ln_b': 'new_v', 'new_v_sgu_w_s': 'new_v', 'new_v_sgu_b': 'new_v', 'new_v_sgu_w_out': 'new_v', 'new_v_w_o': 'new_v', 'new_v_mix_post_g': 'new_v', 'new_v_ffn2_pre_g': 'new_v', 'new_v_ffn2_w_gate': 'new_v', 'new_v_ffn2_w_up': 'new_v', 'new_v_ffn2_w_down': 'new_v', 'new_v_ffn2_post_g': 'new_v'}


def _forward(args):
    return _fwd_reference(*[args[k] for k in FWD_PARAMS])


def _output_shape():
    def fwd():
        inp = _fwd_setup_inputs(0)
        return _fwd_reference(*[inp[k] for k in FWD_PARAMS])
    out = _jax.eval_shape(fwd)
    return out.shape, out.dtype

N_MICROBATCH = 1
ADAM_LR = 0.001
ADAM_B1 = 0.9
ADAM_B2 = 0.999
ADAM_EPS = 1e-08
ADAM_WD = 0.01
ADAM_STEP = 10
PER_EXAMPLE_BATCH_AXIS = {'x': 0, 'loss_target': 0}
SHARED_INPUTS = []
_WEIGHT_DTYPES = {'ffn1_pre_g': _jnp.float32, 'ffn1_w_gate': _jnp.float32, 'ffn1_w_up': _jnp.float32, 'ffn1_w_down': _jnp.float32, 'ffn1_post_g': _jnp.float32, 'mix_pre_g': _jnp.float32, 'w_in': _jnp.float32, 'lru_conv_w': _jnp.float32, 'lru_conv_b': _jnp.float32, 'lru_wa': _jnp.float32, 'lru_ba': _jnp.float32, 'lru_wx': _jnp.float32, 'lru_bx': _jnp.float32, 'lru_lambda': _jnp.float32, 'lru_w_out': _jnp.float32, 'sc_conv_w': _jnp.float32, 'sc_w_out': _jnp.float32, 'sgu_ln_g': _jnp.float32, 'sgu_ln_b': _jnp.float32, 'sgu_w_s': _jnp.float32, 'sgu_b': _jnp.float32, 'sgu_w_out': _jnp.float32, 'w_o': _jnp.float32, 'mix_post_g': _jnp.float32, 'ffn2_pre_g': _jnp.float32, 'ffn2_w_gate': _jnp.float32, 'ffn2_w_up': _jnp.float32, 'ffn2_w_down': _jnp.float32, 'ffn2_post_g': _jnp.float32}
MOMENT_SCALE = {'ffn1_pre_g': 2.666173e+00, 'ffn1_w_gate': 1.092500e+00, 'ffn1_w_up': 1.120468e+00, 'ffn1_w_down': 1.854051e+00, 'ffn1_post_g': 2.885487e+01, 'mix_pre_g': 3.786674e+00, 'w_in': 1.448698e+00, 'lru_conv_w': 4.065386e+00, 'lru_conv_b': 9.116274e+01, 'lru_wa': 1.089144e+00, 'lru_ba': 7.359872e-01, 'lru_wx': 2.108579e+00, 'lru_bx': 8.595664e-01, 'lru_lambda': 1.227467e+00, 'lru_w_out': 5.322844e+00, 'sc_conv_w': 2.238295e+00, 'sc_w_out': 1.576057e+00, 'sgu_ln_g': 1.552182e+00, 'sgu_ln_b': 1.589806e+00, 'sgu_w_s': 1.484039e+00, 'sgu_b': 1.509384e+00, 'sgu_w_out': 1.047398e+00, 'w_o': 3.674425e+00, 'mix_post_g': 1.266770e+02, 'ffn2_pre_g': 1.569795e+00, 'ffn2_w_gate': 6.151872e-01, 'ffn2_w_up': 7.705860e-01, 'ffn2_w_down': 1.283175e+00, 'ffn2_post_g': 3.080683e+01}


def _to_microbatches(a, axis):
    t = _jnp.moveaxis(a, axis, 0)
    t = t.reshape((N_MICROBATCH, t.shape[0] // N_MICROBATCH) + t.shape[1:])
    return _jnp.moveaxis(t, 1, axis + 1)


def setup_inputs(seed: int = 0) -> dict:
    inp = _fwd_setup_inputs(seed)
    key = _jax.random.fold_in(_jax.random.key(seed), 7919)
    shape, _ = _output_shape()
    out = dict(inp)
    out["loss_target"] = _jax.random.normal(_jax.random.fold_in(key, 0), shape, _jnp.float32)
    for i, name in enumerate(TWIN_WEIGHTS):
        w = inp[name].astype(_jnp.float32)
        if MOMENT_SCALE is None:
            s = _jnp.sqrt(_jnp.mean(_jnp.square(w)) + 1e-30)
        else:
            s = MOMENT_SCALE[name]
        km, kv = _jax.random.split(_jax.random.fold_in(key, i + 1))
        out[name] = w
        out["m_" + name] = s * _jax.random.normal(km, w.shape, _jnp.float32)
        out["v_" + name] = (s * s) * _jax.random.uniform(kv, w.shape, _jnp.float32, 0.5, 1.5)
    if N_MICROBATCH > 1:
        for name, axis in PER_EXAMPLE_BATCH_AXIS.items():
            out[name] = _to_microbatches(out[name], axis)
    return {'x': out['x'], 'ffn1_pre_g': out['ffn1_pre_g'], 'ffn1_w_gate': out['ffn1_w_gate'], 'ffn1_w_up': out['ffn1_w_up'], 'ffn1_w_down': out['ffn1_w_down'], 'ffn1_post_g': out['ffn1_post_g'], 'mix_pre_g': out['mix_pre_g'], 'w_in': out['w_in'], 'lru_conv_w': out['lru_conv_w'], 'lru_conv_b': out['lru_conv_b'], 'lru_wa': out['lru_wa'], 'lru_ba': out['lru_ba'], 'lru_wx': out['lru_wx'], 'lru_bx': out['lru_bx'], 'lru_lambda': out['lru_lambda'], 'lru_w_out': out['lru_w_out'], 'sc_conv_w': out['sc_conv_w'], 'sc_w_out': out['sc_w_out'], 'sgu_ln_g': out['sgu_ln_g'], 'sgu_ln_b': out['sgu_ln_b'], 'sgu_w_s': out['sgu_w_s'], 'sgu_b': out['sgu_b'], 'sgu_w_out': out['sgu_w_out'], 'w_o': out['w_o'], 'mix_post_g': out['mix_post_g'], 'ffn2_pre_g': out['ffn2_pre_g'], 'ffn2_w_gate': out['ffn2_w_gate'], 'ffn2_w_up': out['ffn2_w_up'], 'ffn2_w_down': out['ffn2_w_down'], 'ffn2_post_g': out['ffn2_post_g'], 'loss_target': out['loss_target'], 'm_ffn1_pre_g': out['m_ffn1_pre_g'], 'm_ffn1_w_gate': out['m_ffn1_w_gate'], 'm_ffn1_w_up': out['m_ffn1_w_up'], 'm_ffn1_w_down': out['m_ffn1_w_down'], 'm_ffn1_post_g': out['m_ffn1_post_g'], 'm_mix_pre_g': out['m_mix_pre_g'], 'm_w_in': out['m_w_in'], 'm_lru_conv_w': out['m_lru_conv_w'], 'm_lru_conv_b': out['m_lru_conv_b'], 'm_lru_wa': out['m_lru_wa'], 'm_lru_ba': out['m_lru_ba'], 'm_lru_wx': out['m_lru_wx'], 'm_lru_bx': out['m_lru_bx'], 'm_lru_lambda': out['m_lru_lambda'], 'm_lru_w_out': out['m_lru_w_out'], 'm_sc_conv_w': out['m_sc_conv_w'], 'm_sc_w_out': out['m_sc_w_out'], 'm_sgu_ln_g': out['m_sgu_ln_g'], 'm_sgu_ln_b': out['m_sgu_ln_b'], 'm_sgu_w_s': out['m_sgu_w_s'], 'm_sgu_b': out['m_sgu_b'], 'm_sgu_w_out': out['m_sgu_w_out'], 'm_w_o': out['m_w_o'], 'm_mix_post_g': out['m_mix_post_g'], 'm_ffn2_pre_g': out['m_ffn2_pre_g'], 'm_ffn2_w_gate': out['m_ffn2_w_gate'], 'm_ffn2_w_up': out['m_ffn2_w_up'], 'm_ffn2_w_down': out['m_ffn2_w_down'], 'm_ffn2_post_g': out['m_ffn2_post_g'], 'v_ffn1_pre_g': out['v_ffn1_pre_g'], 'v_ffn1_w_gate': out['v_ffn1_w_gate'], 'v_ffn1_w_up': out['v_ffn1_w_up'], 'v_ffn1_w_down': out['v_ffn1_w_down'], 'v_ffn1_post_g': out['v_ffn1_post_g'], 'v_mix_pre_g': out['v_mix_pre_g'], 'v_w_in': out['v_w_in'], 'v_lru_conv_w': out['v_lru_conv_w'], 'v_lru_conv_b': out['v_lru_conv_b'], 'v_lru_wa': out['v_lru_wa'], 'v_lru_ba': out['v_lru_ba'], 'v_lru_wx': out['v_lru_wx'], 'v_lru_bx': out['v_lru_bx'], 'v_lru_lambda': out['v_lru_lambda'], 'v_lru_w_out': out['v_lru_w_out'], 'v_sc_conv_w': out['v_sc_conv_w'], 'v_sc_w_out': out['v_sc_w_out'], 'v_sgu_ln_g': out['v_sgu_ln_g'], 'v_sgu_ln_b': out['v_sgu_ln_b'], 'v_sgu_w_s': out['v_sgu_w_s'], 'v_sgu_b': out['v_sgu_b'], 'v_sgu_w_out': out['v_sgu_w_out'], 'v_w_o': out['v_w_o'], 'v_mix_post_g': out['v_mix_post_g'], 'v_ffn2_pre_g': out['v_ffn2_pre_g'], 'v_ffn2_w_gate': out['v_ffn2_w_gate'], 'v_ffn2_w_up': out['v_ffn2_w_up'], 'v_ffn2_w_down': out['v_ffn2_w_down'], 'v_ffn2_post_g': out['v_ffn2_post_g']}


def _loss(weights, diff, rest, loss_target):
    with _jax.named_scope("forward"):
        args = {**rest, TWIN_DIFF_INPUT: diff, **{k: w.astype(_WEIGHT_DTYPES[k]) for k, w in weights.items()}}
        y = _forward(args)
    with _jax.named_scope("loss_head"):
        err = _jnp.square(y.astype(_jnp.float32) - loss_target)
        return 0.5 * _jnp.sum(_jnp.mean(err, axis=-1)) if err.ndim else 0.5 * err


def _adamw(w, g, m, v):
    m = ADAM_B1 * m + (1.0 - ADAM_B1) * g
    v = ADAM_B2 * v + (1.0 - ADAM_B2) * _jnp.square(g)
    m_hat = m / (1.0 - ADAM_B1 ** ADAM_STEP)
    v_hat = v / (1.0 - ADAM_B2 ** ADAM_STEP)
    delta = -ADAM_LR * (m_hat / (_jnp.sqrt(v_hat) + ADAM_EPS) + ADAM_WD * w)
    return delta, m, v


def reference(x, ffn1_pre_g, ffn1_w_gate, ffn1_w_up, ffn1_w_down, ffn1_post_g, mix_pre_g, w_in, lru_conv_w, lru_conv_b, lru_wa, lru_ba, lru_wx, lru_bx, lru_lambda, lru_w_out, sc_conv_w, sc_w_out, sgu_ln_g, sgu_ln_b, sgu_w_s, sgu_b, sgu_w_out, w_o, mix_post_g, ffn2_pre_g, ffn2_w_gate, ffn2_w_up, ffn2_w_down, ffn2_post_g, loss_target, m_ffn1_pre_g, m_ffn1_w_gate, m_ffn1_w_up, m_ffn1_w_down, m_ffn1_post_g, m_mix_pre_g, m_w_in, m_lru_conv_w, m_lru_conv_b, m_lru_wa, m_lru_ba, m_lru_wx, m_lru_bx, m_lru_lambda, m_lru_w_out, m_sc_conv_w, m_sc_w_out, m_sgu_ln_g, m_sgu_ln_b, m_sgu_w_s, m_sgu_b, m_sgu_w_out, m_w_o, m_mix_post_g, m_ffn2_pre_g, m_ffn2_w_gate, m_ffn2_w_up, m_ffn2_w_down, m_ffn2_post_g, v_ffn1_pre_g, v_ffn1_w_gate, v_ffn1_w_up, v_ffn1_w_down, v_ffn1_post_g, v_mix_pre_g, v_w_in, v_lru_conv_w, v_lru_conv_b, v_lru_wa, v_lru_ba, v_lru_wx, v_lru_bx, v_lru_lambda, v_lru_w_out, v_sc_conv_w, v_sc_w_out, v_sgu_ln_g, v_sgu_ln_b, v_sgu_w_s, v_sgu_b, v_sgu_w_out, v_w_o, v_mix_post_g, v_ffn2_pre_g, v_ffn2_w_gate, v_ffn2_w_up, v_ffn2_w_down, v_ffn2_post_g):
    given = dict(x=x, ffn1_pre_g=ffn1_pre_g, ffn1_w_gate=ffn1_w_gate, ffn1_w_up=ffn1_w_up, ffn1_w_down=ffn1_w_down, ffn1_post_g=ffn1_post_g, mix_pre_g=mix_pre_g, w_in=w_in, lru_conv_w=lru_conv_w, lru_conv_b=lru_conv_b, lru_wa=lru_wa, lru_ba=lru_ba, lru_wx=lru_wx, lru_bx=lru_bx, lru_lambda=lru_lambda, lru_w_out=lru_w_out, sc_conv_w=sc_conv_w, sc_w_out=sc_w_out, sgu_ln_g=sgu_ln_g, sgu_ln_b=sgu_ln_b, sgu_w_s=sgu_w_s, sgu_b=sgu_b, sgu_w_out=sgu_w_out, w_o=w_o, mix_post_g=mix_post_g, ffn2_pre_g=ffn2_pre_g, ffn2_w_gate=ffn2_w_gate, ffn2_w_up=ffn2_w_up, ffn2_w_down=ffn2_w_down, ffn2_post_g=ffn2_post_g, loss_target=loss_target, m_ffn1_pre_g=m_ffn1_pre_g, m_ffn1_w_gate=m_ffn1_w_gate, m_ffn1_w_up=m_ffn1_w_up, m_ffn1_w_down=m_ffn1_w_down, m_ffn1_post_g=m_ffn1_post_g, m_mix_pre_g=m_mix_pre_g, m_w_in=m_w_in, m_lru_conv_w=m_lru_conv_w, m_lru_conv_b=m_lru_conv_b, m_lru_wa=m_lru_wa, m_lru_ba=m_lru_ba, m_lru_wx=m_lru_wx, m_lru_bx=m_lru_bx, m_lru_lambda=m_lru_lambda, m_lru_w_out=m_lru_w_out, m_sc_conv_w=m_sc_conv_w, m_sc_w_out=m_sc_w_out, m_sgu_ln_g=m_sgu_ln_g, m_sgu_ln_b=m_sgu_ln_b, m_sgu_w_s=m_sgu_w_s, m_sgu_b=m_sgu_b, m_sgu_w_out=m_sgu_w_out, m_w_o=m_w_o, m_mix_post_g=m_mix_post_g, m_ffn2_pre_g=m_ffn2_pre_g, m_ffn2_w_gate=m_ffn2_w_gate, m_ffn2_w_up=m_ffn2_w_up, m_ffn2_w_down=m_ffn2_w_down, m_ffn2_post_g=m_ffn2_post_g, v_ffn1_pre_g=v_ffn1_pre_g, v_ffn1_w_gate=v_ffn1_w_gate, v_ffn1_w_up=v_ffn1_w_up, v_ffn1_w_down=v_ffn1_w_down, v_ffn1_post_g=v_ffn1_post_g, v_mix_pre_g=v_mix_pre_g, v_w_in=v_w_in, v_lru_conv_w=v_lru_conv_w, v_lru_conv_b=v_lru_conv_b, v_lru_wa=v_lru_wa, v_lru_ba=v_lru_ba, v_lru_wx=v_lru_wx, v_lru_bx=v_lru_bx, v_lru_lambda=v_lru_lambda, v_lru_w_out=v_lru_w_out, v_sc_conv_w=v_sc_conv_w, v_sc_w_out=v_sc_w_out, v_sgu_ln_g=v_sgu_ln_g, v_sgu_ln_b=v_sgu_ln_b, v_sgu_w_s=v_sgu_w_s, v_sgu_b=v_sgu_b, v_sgu_w_out=v_sgu_w_out, v_w_o=v_w_o, v_mix_post_g=v_mix_post_g, v_ffn2_pre_g=v_ffn2_pre_g, v_ffn2_w_gate=v_ffn2_w_gate, v_ffn2_w_up=v_ffn2_w_up, v_ffn2_w_down=v_ffn2_w_down, v_ffn2_post_g=v_ffn2_post_g)
    weights = {n: given[n] for n in TWIN_WEIGHTS}
    shared = {n: given[n] for n in SHARED_INPUTS}
    per_example = {n: given[n] for n in ['x']}
    grad_fn = _jax.value_and_grad(_loss, argnums=(0, 1))

    def one_microbatch(ex, loss_target):
        ex = dict(ex)
        diff = ex.pop(TWIN_DIFF_INPUT)
        return grad_fn(weights, diff, {**shared, **ex}, loss_target)

    if N_MICROBATCH == 1:
        loss, (grad_w, grad_x) = one_microbatch(per_example, given["loss_target"])
    else:
        def body(carry, xs):
            loss_sum, grad_sum = carry
            l_k, (gw_k, gx_k) = one_microbatch(xs[0], xs[1])
            with _jax.named_scope("update"):
                return (loss_sum + l_k, _jax.tree.map(_jnp.add, grad_sum, gw_k)), gx_k

        init = (_jnp.zeros((), _jnp.float32), _jax.tree.map(_jnp.zeros_like, weights))
        (loss, grad_w), grad_x = _jax.lax.scan(body, init, (per_example, given["loss_target"]))
    with _jax.named_scope("update"):
        delta_w, new_m, new_v = {}, {}, {}
        for n in TWIN_WEIGHTS:
            delta_w[n], new_m[n], new_v[n] = _adamw(weights[n], grad_w[n], given["m_" + n], given["v_" + n])
    return (loss, grad_x, *[grad_w[n] for n in TWIN_WEIGHTS], *[delta_w[n] for n in TWIN_WEIGHTS],
            *[new_m[n] for n in TWIN_WEIGHTS], *[new_v[n] for n in TWIN_WEIGHTS])
```

```python
import functools
import math

import jax
import jax.numpy as jnp
from jax import lax
from jax.experimental import pallas as pl
from jax.experimental.pallas import tpu as pltpu

_F32 = jnp.float32
_MM = jnp.bfloat16
_EPS = 1e-6
_HEADS = 4
_CHUNK = 128
_LRU_C = 8.0
_HALO = 16
_LANES = 1024
_NDEV = 8
_VMEM_LIMIT = 56 * 1024 * 1024
_GELU_K = math.sqrt(2.0 / math.pi)
_GELU_C = 0.044715
_MESH = pl.DeviceIdType.MESH

_ADAM_LR, _ADAM_B1, _ADAM_B2, _ADAM_EPS, _ADAM_WD, _ADAM_STEP = 1e-3, 0.9, 0.999, 1e-8, 0.01, 10

_WEIGHTS = ['ffn1_pre_g', 'ffn1_w_gate', 'ffn1_w_up', 'ffn1_w_down', 'ffn1_post_g', 'mix_pre_g', 'w_in',
            'lru_conv_w', 'lru_conv_b', 'lru_wa', 'lru_ba', 'lru_wx', 'lru_bx', 'lru_lambda', 'lru_w_out',
            'sc_conv_w', 'sc_w_out', 'sgu_ln_g', 'sgu_ln_b', 'sgu_w_s', 'sgu_b', 'sgu_w_out', 'w_o',
            'mix_post_g', 'ffn2_pre_g', 'ffn2_w_gate', 'ffn2_w_up', 'ffn2_w_down', 'ffn2_post_g']
_BIG = {'ffn1_w_gate': 2, 'ffn1_w_up': 2, 'ffn1_w_down': 1, 'w_in': 2, 'lru_wa': 3, 'lru_wx': 3,
        'lru_w_out': 1, 'sc_w_out': 2, 'sgu_w_out': 2, 'w_o': 1,
        'ffn2_w_gate': 2, 'ffn2_w_up': 2, 'ffn2_w_down': 1}
_SMALL_SHARDED = {'lru_conv_w': 2, 'lru_ba': 2, 'lru_bx': 2, 'lru_lambda': 2, 'sc_conv_w': 2}
_REPLICATED = ['ffn1_pre_g', 'ffn1_post_g', 'mix_pre_g', 'lru_conv_b', 'sgu_ln_g', 'sgu_ln_b', 'sgu_w_s',
               'sgu_b', 'mix_post_g', 'ffn2_pre_g', 'ffn2_post_g']


def _dot(a, b):
    return jnp.dot(a, b, preferred_element_type=_F32)


def _dot_nt(a, b):
    return lax.dot_general(a, b, (((1,), (1,)), ((), ())), preferred_element_type=_F32)


def _dot_tn(a, b):
    return lax.dot_general(a, b, (((0,), (0,)), ((), ())), preferred_element_type=_F32)


def _sigmoid(x):
    return jax.nn.sigmoid(x)


def _gelu(x):
    t = jnp.tanh(_GELU_K * (x + _GELU_C * x * x * x))
    return 0.5 * x * (1.0 + t)


def _gelu_grad(x):
    x2 = x * x
    t = jnp.tanh(_GELU_K * (x + _GELU_C * x * x2))
    return 0.5 * (1.0 + t) + 0.5 * x * (1.0 - t * t) * (_GELU_K * (1.0 + 3.0 * _GELU_C * x2))


def _rms_fwd(x, g):
    r = lax.rsqrt(jnp.mean(x * x, axis=-1, keepdims=True) + _EPS)
    return x * r * g


def _rms_bwd(dy, x, g):
    r = lax.rsqrt(jnp.mean(x * x, axis=-1, keepdims=True) + _EPS)
    xh = x * r
    dxh = dy * g
    dx = r * (dxh - xh * jnp.mean(dxh * xh, axis=-1, keepdims=True))
    return dx, jnp.sum(dy * xh, axis=0, keepdims=True)


def _neg_softplus_neg(lam):
    e = jnp.exp(-jnp.abs(lam))
    l1p = jnp.where(e < 1e-2, e * (1.0 - e * (0.5 - e * (1.0 / 3.0 - 0.25 * e))), jnp.log(1.0 + e))
    return -_LRU_C * (jnp.maximum(-lam, 0.0) + l1p)


def _shift_rows(xe, d, tm):
    n = xe.shape[0]
    if d == 0:
        return xe[_HALO:_HALO + tm]
    return pltpu.roll(xe, (-d) % n, axis=0)[_HALO:_HALO + tm]


def _with_halo(cur, prev, nxt, first, last):
    p = jnp.where(first, 0.0, prev.astype(_F32))
    n = jnp.where(last, 0.0, nxt.astype(_F32))
    return jnp.concatenate([p, cur.astype(_F32), n], axis=0)


def _rows(arr, tm):
    c = arr.shape[1]
    return (arr, (tm, c), lambda ti: (ti, 0))


def _rows3(arr, k, tm):
    c = arr.shape[2]
    return (arr, (None, tm, c), lambda ti, k=k: (k, ti, 0))


def _halo_prev(arr, tm):
    c = arr.shape[1]
    return (arr, (_HALO, c), lambda ti: (jnp.maximum(ti * (tm // _HALO) - 1, 0), 0))


def _halo_next(arr, tm):
    c = arr.shape[1]
    nblk = arr.shape[0] // _HALO
    return (arr, (_HALO, c), lambda ti: (jnp.minimum((ti + 1) * (tm // _HALO), nblk - 1), 0))


def _full(arr):
    nd = arr.ndim
    return (arr, arr.shape, lambda ti, nd=nd: (0,) * nd)


def _out_rows(t, c, dtype, tm):
    return ((t, c), dtype, (tm, c), lambda ti: (ti, 0))


def _row_call(name, body, n_tiles, ins, outs, accs=(), hbm=(), scratch=(), reverse=False):
    n_in, n_hbm, n_out, n_acc = len(ins), len(hbm), len(outs), len(accs)

    def tile_of(step):
        return (n_tiles - 1 - step) if reverse else step

    def spec(block, index_fn):
        return pl.BlockSpec(block, lambda s, f=index_fn: f(tile_of(s)))

    def kern(*refs):
        in_refs = refs[:n_in]
        hbm_refs = refs[n_in:n_in + n_hbm]
        out_refs = refs[n_in + n_hbm:n_in + n_hbm + n_out]
        acc_refs = refs[n_in + n_hbm + n_out:n_in + n_hbm + n_out + n_acc]
        rest = refs[n_in + n_hbm + n_out + n_acc:]
        w_refs, scr = rest[:n_hbm], rest[n_hbm:]
        step = pl.program_id(0)

        @pl.when(step == 0)
        def _():
            for src, dst in zip(hbm_refs, w_refs):
                pltpu.sync_copy(src, dst)
            for a in acc_refs:
                a[...] = jnp.zeros(a.shape, a.dtype)

        body(step, tile_of(step), in_refs, w_refs, out_refs, acc_refs, scr)

    in_specs = [spec(b, f) for (_, b, f) in ins] + [pl.BlockSpec(memory_space=pl.ANY)] * n_hbm
    out_specs = [spec(b, f) for (_, _, b, f) in outs]
    out_specs += [pl.BlockSpec(s, lambda st, nd=len(s): (0,) * nd) for s in accs]
    out_shape = [jax.ShapeDtypeStruct(s, d) for (s, d, _, _) in outs]
    out_shape += [jax.ShapeDtypeStruct(s, _F32) for s in accs]
    scratch_shapes = [pltpu.VMEM(w.shape, w.dtype) for w in hbm] + list(scratch)
    res = pl.pallas_call(
        kern, name=name, grid=(n_tiles,), in_specs=in_specs, out_specs=out_specs, out_shape=out_shape,
        scratch_shapes=scratch_shapes,
        compiler_params=pltpu.CompilerParams(dimension_semantics=("arbitrary",), vmem_limit_bytes=_VMEM_LIMIT),
    )(*[a for (a, _, _) in ins], *hbm)
    return list(res)


def _xty(name, x, y, tk):
    t, k1 = x.shape
    k2 = y.shape[1]

    def kern(x_ref, y_ref, o_ref):
        @pl.when(pl.program_id(0) == 0)
        def _():
            o_ref[...] = jnp.zeros(o_ref.shape, o_ref.dtype)

        o_ref[...] += _dot_tn(x_ref[...], y_ref[...])

    return pl.pallas_call(
        kern, name=name, grid=(t // tk,),
        in_specs=[pl.BlockSpec((tk, k1), lambda k: (k, 0)), pl.BlockSpec((tk, k2), lambda k: (k, 0))],
        out_specs=pl.BlockSpec((k1, k2), lambda k: (0, 0)),
        out_shape=jax.ShapeDtypeStruct((k1, k2), _F32),
        compiler_params=pltpu.CompilerParams(dimension_semantics=("arbitrary",), vmem_limit_bytes=_VMEM_LIMIT),
    )(x, y)


def _ffn_up(name, x, pre_g, wg, wu, tm):
    t, d = x.shape
    f = wg.shape[1]

    def body(step, ti, ins, ws, outs, accs, scr):
        x_ref, g_ref = ins
        h = _rms_fwd(x_ref[...], g_ref[...]).astype(_MM)
        a = _dot(h, ws[0][...])
        b = _dot(h, ws[1][...])
        outs[0][...] = h
        outs[1][...] = a.astype(_MM)
        outs[2][...] = b.astype(_MM)
        outs[3][...] = (a * _sigmoid(a) * b).astype(_MM)

    return _row_call(name, body, t // tm, [_rows(x, tm), _full(pre_g)],
                     [_out_rows(t, d, _MM, tm), _out_rows(t, f, _MM, tm), _out_rows(t, f, _MM, tm),
                      _out_rows(t, f, _MM, tm)], hbm=[wg, wu])


def _proj_norm_res(name, lhs, x, post_g, w, scale, tm):
    t, d = x.shape

    def body(step, ti, ins, ws, outs, accs, scr):
        l_ref, x_ref, g_ref = ins
        f = _dot(l_ref[...], ws[0][...])
        outs[0][...] = f
        outs[1][...] = x_ref[...] + scale * _rms_fwd(f, g_ref[...])

    return _row_call(name, body, t // tm, [_rows(lhs, tm), _rows(x, tm), _full(post_g)],
                     [_out_rows(t, d, _F32, tm), _out_rows(t, d, _F32, tm)], hbm=[w])


def _ffn_bwd_post(name, dxo, f, a, b, post_g, wd, scale, tm):
    t, d = dxo.shape
    ff = a.shape[1]

    def body(step, ti, ins, ws, outs, accs, scr):
        dxo_ref, f_ref, a_ref, b_ref, g_ref = ins
        df, dg = _rms_bwd(scale * dxo_ref[...], f_ref[...], g_ref[...])
        accs[0][...] += dg
        dfb = df.astype(_MM)
        ds = _dot_nt(dfb, ws[0][...])
        a32 = a_ref[...].astype(_F32)
        b32 = b_ref[...].astype(_F32)
        sg = _sigmoid(a32)
        outs[0][...] = (ds * b32 * (sg * (1.0 + a32 * (1.0 - sg)))).astype(_MM)
        outs[1][...] = (ds * (a32 * sg)).astype(_MM)
        outs[2][...] = dfb

    return _row_call(name, body, t // tm,
                     [_rows(dxo, tm), _rows(f, tm), _rows(a, tm), _rows(b, tm), _full(post_g)],
                     [_out_rows(t, ff, _MM, tm), _out_rows(t, ff, _MM, tm), _out_rows(t, d, _MM, tm)],
                     accs=[(1, d)], hbm=[wd])


def _bwd_in_norm(name, dzs, ws_list, x, dxo, pre_g, tm):
    t, d = x.shape
    nz = len(dzs)

    def body(step, ti, ins, ws, outs, accs, scr):
        dh = _dot_nt(ins[0][...], ws[0][...])
        for k in range(1, nz):
            dh = dh + _dot_nt(ins[k][...], ws[k][...])
        x_ref, dxo_ref, g_ref = ins[nz:]
        dx, dg = _rms_bwd(dh, x_ref[...], g_ref[...])
        accs[0][...] += dg
        outs[0][...] = dxo_ref[...] + dx

    return _row_call(name, body, t // tm,
                     [_rows(z, tm) for z in dzs] + [_rows(x, tm), _rows(dxo, tm), _full(pre_g)],
                     [_out_rows(t, d, _F32, tm)], accs=[(1, d)], hbm=list(ws_list))


def _mix_in(name, x, pre_g, w_parts, tm):
    t, d = x.shape

    def body(step, ti, ins, ws, outs, accs, scr):
        x_ref, g_ref = ins
        h = _rms_fwd(x_ref[...], g_ref[...]).astype(_MM)
        outs[0][...] = h
        for k in range(len(ws)):
            outs[1 + k][...] = _dot(h, ws[k][...]).astype(_MM)

    return _row_call(name, body, t // tm, [_rows(x, tm), _full(pre_g)],
                     [_out_rows(t, d, _MM, tm)] + [_out_rows(t, w.shape[1], _MM, tm) for w in w_parts],
                     hbm=list(w_parts))


def _lru_conv(xe, cw, cb, tm):
    xc = cb
    for k in range(4):
        xc = xc + _shift_rows(xe, k - 2, tm) * cw[k:k + 1, :]
    return xc


def _lru_gates(xc, wa_ref, wx_ref, ba, bx, c):
    dh = xc.shape[1] // _HEADS
    gas, gxs = [], []
    for hh in range(_HEADS):
        xs = xc[:, hh * dh:(hh + 1) * dh].astype(_MM)
        gas.append(_dot(xs, wa_ref[hh]))
        gxs.append(_dot(xs, wx_ref[hh]))
    r = _sigmoid(jnp.concatenate(gas, axis=1) + ba)
    i = _sigmoid(jnp.concatenate(gxs, axis=1) + bx)
    la = c * r
    a = jnp.exp(la)
    y = 2.0 * la
    em = jnp.where(y > -0.05, -y * (1.0 + y * (0.5 + y * (1.0 / 6.0 + y * (1.0 / 24.0)))), 1.0 - a * a)
    return r, i, a, em


def _tile_scan(a_scr, u_scr, h_ref, carry, tm, descending):
    ng = tm // 8
    w = a_scr.shape[1]
    row = lax.broadcasted_iota(jnp.int32, (8, w), 0)

    def grp(j, carry):
        g = (ng - 1 - j) if descending else j
        r0 = pl.multiple_of(g * 8, 8)
        a8 = a_scr[pl.ds(r0, 8), :]
        u8 = u_scr[pl.ds(r0, 8), :]
        for dd in (1, 2, 4):
            if descending:
                ok = row < 8 - dd
                sh = 8 - dd
            else:
                ok = row >= dd
                sh = dd
            a_s = jnp.where(ok, pltpu.roll(a8, sh, axis=0), 1.0)
            u_s = jnp.where(ok, pltpu.roll(u8, sh, axis=0), 0.0)
            u8 = a8 * u_s + u8
            a8 = a8 * a_s
        h8 = u8 + a8 * carry
        h_ref[pl.ds(r0, 8), :] = h8
        return h8[0:1, :] if descending else h8[7:8, :]

    return lax.fori_loop(0, ng, grp, carry)


def _lru_fwd(name, zx, cw, cb, wa, wx, ba, bx, lam, tm, descending, hf=None, zg=None):
    t, w = zx.shape
    n = t // tm

    def body(step, ti, ins, ws, outs, accs, scr):
        zc, zp, zn, cw_r, cb_r, wa_r, wx_r, ba_r, bx_r, lam_r = ins[:10]
        a_scr, u_scr, carry_scr = scr
        xe = _with_halo(zc[...], zp[...], zn[...], ti == 0, ti == n - 1)
        xc = _lru_conv(xe, cw_r[...], cb_r[...], tm)
        c = _neg_softplus_neg(lam_r[...])
        r, i, a, em = _lru_gates(xc, wa_r, wx_r, ba_r[...], bx_r[...], c)
        a_scr[...] = a
        u_scr[...] = i * xc * jnp.sqrt(em)

        @pl.when(step == 0)
        def _():
            carry_scr[...] = jnp.zeros(carry_scr.shape, _F32)

        carry_scr[...] = _tile_scan(a_scr, u_scr, outs[0], carry_scr[...], tm, descending)
        if descending:
            hf_r, zg_r = ins[10:]
            outs[1][...] = ((hf_r[...] + outs[0][...]) * _gelu(zg_r[...].astype(_F32))).astype(_MM)

    ins = [_rows(zx, tm), _halo_prev(zx, tm), _halo_next(zx, tm), _full(cw), _full(cb), _full(wa), _full(wx),
           _full(ba), _full(bx), _full(lam)]
    outs = [_out_rows(t, w, _F32, tm)]
    if descending:
        ins += [_rows(hf, tm), _rows(zg, tm)]
        outs += [_out_rows(t, w, _MM, tm)]
    scratch = [pltpu.VMEM((tm, w), _F32), pltpu.VMEM((tm, w), _F32), pltpu.VMEM((1, w), _F32)]
    return _row_call(name, body, n, ins, outs, scratch=scratch, reverse=descending)


def _lru_bwd(name, zx, zg, dpa, h_own, h_other, cw, cb, wa, wx, ba, bx, lam, tm, direction, dxc_in=None):
    t, w = zx.shape
    n = t // tm
    dh_ = w // _HEADS
    adj_desc = direction == 0

    def body(step, ti, ins, ws, outs, accs, scr):
        (zc, zp, zn, zg_r, dpa_r, ho_r, hh_r, hoth_r,
         cw_r, cb_r, wa_r, wx_r, ba_r, bx_r, lam_r) = ins[:15]
        a_scr, u_scr, p_scr, carry_scr = scr
        first, last = ti == 0, ti == n - 1
        xe = _with_halo(zc[...], zp[...], zn[...], first, last)
        xc = _lru_conv(xe, cw_r[...], cb_r[...], tm)
        lam_v = lam_r[...]
        c = _neg_softplus_neg(lam_v)
        r, i, a, em = _lru_gates(xc, wa_r, wx_r, ba_r[...], bx_r[...], c)
        m = jnp.sqrt(em)
        zg32 = zg_r[...].astype(_F32)
        dpa_v = dpa_r[...]
        d_h = dpa_v * _gelu(zg32)
        a_scr[...] = a
        u_scr[...] = a * d_h

        @pl.when(step == 0)
        def _():
            carry_scr[...] = jnp.zeros(carry_scr.shape, _F32)

        carry_in = carry_scr[...]
        carry_scr[...] = _tile_scan(a_scr, u_scr, p_scr, carry_in, tm, adj_desc)
        p = p_scr[...]
        row = lax.broadcasted_iota(jnp.int32, (tm, w), 0)
        h_t = ho_r[...]
        if adj_desc:
            p_nb = jnp.where(row == tm - 1, carry_in, pltpu.roll(p, tm - 1, axis=0))
            edge = jnp.where(first, 0.0, hh_r[_HALO - 1:_HALO, :])
            h_nb = jnp.where(row == 0, edge, pltpu.roll(h_t, 1, axis=0))
        else:
            p_nb = jnp.where(row == 0, carry_in, pltpu.roll(p, 1, axis=0))
            edge = jnp.where(last, 0.0, hh_r[0:1, :])
            h_nb = jnp.where(row == tm - 1, edge, pltpu.roll(h_t, tm - 1, axis=0))
        g = d_h + p_nb
        gi = g * i
        d_i = g * xc * m
        dxc = gi * m
        d_m = gi * xc
        d_l = g * h_nb * a - d_m * (1.0 - em) / m
        accs[4][...] += jnp.sum(d_l * r, axis=0, keepdims=True)
        dga = d_l * c * r * (1.0 - r)
        dgx = d_i * i * (1.0 - i)
        accs[2][...] += jnp.sum(dga, axis=0, keepdims=True)
        accs[3][...] += jnp.sum(dgx, axis=0, keepdims=True)
        parts = []
        for hh in range(_HEADS):
            sl = slice(hh * dh_, (hh + 1) * dh_)
            xs = xc[:, sl].astype(_MM)
            da_h = dga[:, sl].astype(_MM)
            dx_h = dgx[:, sl].astype(_MM)
            accs[0][hh] += _dot_tn(xs, da_h)
            accs[1][hh] += _dot_tn(xs, dx_h)
            parts.append(_dot_nt(da_h, wa_r[hh]) + _dot_nt(dx_h, wx_r[hh]))
        dxc = dxc + jnp.concatenate(parts, axis=1)
        if direction == 0:
            outs[0][...] = dxc
            outs[1][...] = (dpa_v * (h_t + hoth_r[...]) * _gelu_grad(zg32)).astype(_MM)
        else:
            outs[0][...] = dxc + ins[15][...]

        @pl.when(step == n - 1)
        def _():
            accs[4][...] = accs[4][...] * (_LRU_C * _sigmoid(-lam_v))

    halo_h = _halo_prev(h_own, tm) if adj_desc else _halo_next(h_own, tm)
    ins = [_rows(zx, tm), _halo_prev(zx, tm), _halo_next(zx, tm), _rows(zg, tm), _rows(dpa, tm),
           _rows(h_own, tm), halo_h, _rows(h_other, tm),
           _full(cw), _full(cb), _full(wa), _full(wx), _full(ba), _full(bx), _full(lam)]
    outs = [_out_rows(t, w, _F32, tm)]
    if direction == 0:
        outs += [_out_rows(t, w, _MM, tm)]
    else:
        ins += [_rows(dxc_in, tm)]
    accs = [(_HEADS, dh_, dh_), (_HEADS, dh_, dh_), (1, w), (1, w), (1, w)]
    scratch = [pltpu.VMEM((tm, w), _F32), pltpu.VMEM((tm, w), _F32), pltpu.VMEM((tm, w), _F32),
               pltpu.VMEM((1, w), _F32)]
    return _row_call(name, body, n, ins, outs, accs=accs, scratch=scratch, reverse=adj_desc)


def _lru_conv_bwd(name, dxc, zx, cw, tm):
    t, w = zx.shape
    n = t // tm

    def body(step, ti, ins, ws, outs, accs, scr):
        dc, dp, dn, zc, zp, zn, cw_r = ins
        first, last = ti == 0, ti == n - 1
        de = _with_halo(dc[...], dp[...], dn[...], first, last)
        ze = _with_halo(zc[...], zp[...], zn[...], first, last)
        cw_v = cw_r[...]
        d_cur = dc[...]
        dz = None
        for k in range(4):
            term = _shift_rows(de, 2 - k, tm) * cw_v[k:k + 1, :]
            dz = term if dz is None else dz + term
            accs[0][k:k + 1, :] += jnp.sum(d_cur * _shift_rows(ze, k - 2, tm), axis=0, keepdims=True)
        accs[1][...] += jnp.sum(d_cur, axis=0, keepdims=True)
        outs[0][...] = dz.astype(_MM)

    ins = [_rows(dxc, tm), _halo_prev(dxc, tm), _halo_next(dxc, tm),
           _rows(zx, tm), _halo_prev(zx, tm), _halo_next(zx, tm), _full(cw)]
    return _row_call(name, body, n, ins, [_out_rows(t, w, _MM, tm)], accs=[(4, w), (1, w)])


def _sgu_mix(v2, ws_ref, bias, mixed_scr, tm):
    gw = v2.shape[1]
    gh = gw // _HEADS
    for nn in range(tm // _CHUNK):
        rs = slice(nn * _CHUNK, (nn + 1) * _CHUNK)
        for g in range(_HEADS):
            cs = slice(g * gh, (g + 1) * gh)
            mixed_scr[rs, cs] = _dot(ws_ref[g], v2[rs, cs].astype(_MM)) + bias[:, cs]
    return mixed_scr[...]


def _ln_fwd(v1, lg, lb):
    mu = jnp.mean(v1, axis=-1, keepdims=True)
    vc = v1 - mu
    rs = lax.rsqrt(jnp.mean(vc * vc, axis=-1, keepdims=True) + _EPS)
    vn = vc * rs
    return vn * lg + lb, vn, rs


def _bc_fwd(name, zmid, scw, lg, lb, ws_mm, bias, sw, gw, tm):
    t = zmid.shape[0]
    n = t // tm

    def body(step, ti, ins, ws, outs, accs, scr):
        zc, zp, zn, scw_r, lg_r, lb_r, ws_r, bias_r = ins
        first, last = ti == 0, ti == n - 1
        z = zc[...].astype(_F32)
        zb, zcc, zxx = z[:, 0:sw], z[:, sw:2 * sw], z[:, 2 * sw:3 * sw]
        zu, zv = z[:, 3 * sw:3 * sw + gw], z[:, 3 * sw + gw:3 * sw + 2 * gw]
        zpv, znv = zp[...].astype(_F32), zn[...].astype(_F32)
        qe = _with_halo(zcc * zxx, zpv[:, sw:2 * sw] * zpv[:, 2 * sw:3 * sw],
                        znv[:, sw:2 * sw] * znv[:, 2 * sw:3 * sw], first, last)
        scw_v = scw_r[...]
        cq = None
        for k in range(3):
            term = _shift_rows(qe, k - 1, tm) * scw_v[k:k + 1, :]
            cq = term if cq is None else cq + term
        outs[0][...] = (zb * cq).astype(_MM)
        v2, _, _ = _ln_fwd(_gelu(zv), lg_r[...], lb_r[...])
        mixed = _sgu_mix(v2, ws_r, bias_r[...], scr[0], tm)
        outs[1][...] = (_gelu(zu) * mixed).astype(_MM)

    ins = [_rows(zmid, tm), _halo_prev(zmid, tm), _halo_next(zmid, tm), _full(scw), _full(lg), _full(lb),
           _full(ws_mm), _full(bias)]
    return _row_call(name, body, n, ins, [_out_rows(t, sw, _MM, tm), _out_rows(t, gw, _MM, tm)],
                     scratch=[pltpu.VMEM((tm, gw), _F32)])


def _bc_bwd(name, zmid, dpb, dpc, scw, lg, lb, ws_mm, wst_mm, bias, sw, gw, tm):
    t = zmid.shape[0]
    n = t // tm
    gh = gw // _HEADS

    def body(step, ti, ins, ws, outs, accs, scr):
        zc, zp, zn, db_c, db_p, db_n, dc_r, scw_r, lg_r, lb_r, ws_r, wst_r, bias_r = ins
        mixed_scr, dv2_scr = scr
        first, last = ti == 0, ti == n - 1
        z = zc[...].astype(_F32)
        zb, zcc, zxx = z[:, 0:sw], z[:, sw:2 * sw], z[:, 2 * sw:3 * sw]
        zu, zv = z[:, 3 * sw:3 * sw + gw], z[:, 3 * sw + gw:3 * sw + 2 * gw]
        zpv, znv = zp[...].astype(_F32), zn[...].astype(_F32)
        qe = _with_halo(zcc * zxx, zpv[:, sw:2 * sw] * zpv[:, 2 * sw:3 * sw],
                        znv[:, sw:2 * sw] * znv[:, 2 * sw:3 * sw], first, last)
        dpb_v = db_c[...]
        dcq = dpb_v * zb
        dcqe = _with_halo(dcq, db_p[...] * zpv[:, 0:sw], db_n[...] * znv[:, 0:sw], first, last)
        scw_v = scw_r[...]
        cq, dq = None, None
        for k in range(3):
            qk = _shift_rows(qe, k - 1, tm)
            term = qk * scw_v[k:k + 1, :]
            cq = term if cq is None else cq + term
            dterm = _shift_rows(dcqe, 1 - k, tm) * scw_v[k:k + 1, :]
            dq = dterm if dq is None else dq + dterm
            accs[0][k:k + 1, :] += jnp.sum(dcq * qk, axis=0, keepdims=True)
        outs[0][:, 0:sw] = (dpb_v * cq).astype(_MM)
        outs[0][:, sw:2 * sw] = (dq * zxx).astype(_MM)
        outs[0][:, 2 * sw:3 * sw] = (dq * zcc).astype(_MM)
        lg_v = lg_r[...]
        v2, vn, rs = _ln_fwd(_gelu(zv), lg_v, lb_r[...])
        mixed = _sgu_mix(v2, ws_r, bias_r[...], mixed_scr, tm)
        dpc_v = dc_r[...]
        outs[0][:, 3 * sw:3 * sw + gw] = (dpc_v * mixed * _gelu_grad(zu)).astype(_MM)
        dmix = dpc_v * _gelu(zu)
        for nn in range(tm // _CHUNK):
            rsl = slice(nn * _CHUNK, (nn + 1) * _CHUNK)
            accs[4][...] += dmix[rsl, :]
            for g in range(_HEADS):
                cs = slice(g * gh, (g + 1) * gh)
                dm_b = dmix[rsl, cs].astype(_MM)
                accs[3][g] += _dot_nt(dm_b, v2[rsl, cs].astype(_MM))
                dv2_scr[rsl, cs] = _dot(wst_r[g], dm_b)
        dv2 = dv2_scr[...]
        accs[1][...] += jnp.sum(dv2 * vn, axis=0, keepdims=True)
        accs[2][...] += jnp.sum(dv2, axis=0, keepdims=True)
        dvn = dv2 * lg_v
        dv1 = rs * (dvn - jnp.mean(dvn, axis=-1, keepdims=True)
                    - vn * jnp.mean(dvn * vn, axis=-1, keepdims=True))
        outs[0][:, 3 * sw + gw:3 * sw + 2 * gw] = (dv1 * _gelu_grad(zv)).astype(_MM)

    ins = [_rows(zmid, tm), _halo_prev(zmid, tm), _halo_next(zmid, tm),
           _rows(dpb, tm), _halo_prev(dpb, tm), _halo_next(dpb, tm), _rows(dpc, tm),
           _full(scw), _full(lg), _full(lb), _full(ws_mm), _full(wst_mm), _full(bias)]
    accs = [(3, sw), (1, gw), (1, gw), (_HEADS, _CHUNK, _CHUNK), (_CHUNK, gw)]
    return _row_call(name, body, n, ins, [_out_rows(t, 3 * sw + 2 * gw, _MM, tm)], accs=accs,
                     scratch=[pltpu.VMEM((tm, gw), _F32), pltpu.VMEM((tm, gw), _F32)])


def _mix_proj(name, pa, pb, pc, zm, wlo, wsc, wsg, tm):
    t = pa.shape[0]
    d = wlo.shape[1]

    def body(step, ti, ins, ws, outs, accs, scr):
        ys = [_dot(ins[k][...], ws[k][...]) for k in range(3)]
        gm = _sigmoid(ins[3][...].astype(_F32))
        m = None
        for k in range(3):
            outs[k][...] = ys[k].astype(_MM)
            term = gm[:, k * d:(k + 1) * d] * ys[k]
            m = term if m is None else m + term
        outs[3][...] = m.astype(_MM)

    return _row_call(name, body, t // tm, [_rows(pa, tm), _rows(pb, tm), _rows(pc, tm), _rows(zm, tm)],
                     [_out_rows(t, d, _MM, tm)] * 4, hbm=[wlo, wsc, wsg])


def _mix_bwd_out(name, dxo, mo, ya, yb, yc, zm, post_g, wo, tm):
    t, d = dxo.shape

    def body(step, ti, ins, ws, outs, accs, scr):
        dxo_ref, mo_ref, ya_r, yb_r, yc_r, zm_r, g_ref = ins
        dmo, dg = _rms_bwd(dxo_ref[...], mo_ref[...], g_ref[...])
        accs[0][...] += dg
        dmob = dmo.astype(_MM)
        outs[0][...] = dmob
        dm = _dot_nt(dmob, ws[0][...])
        gm = _sigmoid(zm_r[...].astype(_F32))
        for k, y_r in enumerate((ya_r, yb_r, yc_r)):
            gk = gm[:, k * d:(k + 1) * d]
            outs[1 + k][...] = (dm * gk).astype(_MM)
            outs[4][:, k * d:(k + 1) * d] = (dm * y_r[...].astype(_F32) * gk * (1.0 - gk)).astype(_MM)

    ins = [_rows(dxo, tm), _rows(mo, tm), _rows(ya, tm), _rows(yb, tm), _rows(yc, tm), _rows(zm, tm),
           _full(post_g)]
    return _row_call(name, body, t // tm, ins,
                     [_out_rows(t, d, _MM, tm)] * 4 + [_out_rows(t, 3 * d, _MM, tm)], accs=[(1, d)], hbm=[wo])


def _mix_bwd_proj(name, dya, dyb, dyc, wlo, wsc, wsg, tm):
    t = dya.shape[0]

    def body(step, ti, ins, ws, outs, accs, scr):
        for k in range(3):
            outs[k][...] = _dot_nt(ins[k][...], ws[k][...])

    return _row_call(name, body, t // tm, [_rows(dya, tm), _rows(dyb, tm), _rows(dyc, tm)],
                     [_out_rows(t, w.shape[0], _F32, tm) for w in (wlo, wsc, wsg)], hbm=[wlo, wsc, wsg])


def _loss_grad(name, y, target, tm):
    t, d = y.shape

    def body(step, ti, ins, ws, outs, accs, scr):
        err = ins[0][...] - ins[1][...]
        outs[0][...] = err * (1.0 / d)
        accs[0][...] += (0.5 / d) * jnp.sum(err * err)

    dy, acc = _row_call(name, body, t // tm, [_rows(y, tm), _rows(target, tm)], [_out_rows(t, d, _F32, tm)],
                        accs=[(1, 128)])
    return acc[0, 0], dy


def _adamw(w, g, m, v):
    m = _ADAM_B1 * m + (1.0 - _ADAM_B1) * g
    v = _ADAM_B2 * v + (1.0 - _ADAM_B2) * (g * g)
    m_hat = m / (1.0 - _ADAM_B1 ** _ADAM_STEP)
    v_hat = v / (1.0 - _ADAM_B2 ** _ADAM_STEP)
    delta = -_ADAM_LR * (m_hat / (jnp.sqrt(v_hat) + _ADAM_EPS) + _ADAM_WD * w)
    return delta, m, v


def _flat_tm(rows):
    return 512 if rows % 512 == 0 else rows


def _sum_adamw(name, parts, w, m, v):
    rows, c = w.shape
    tm = _flat_tm(rows)
    np_ = len(parts)

    def body(step, ti, ins, ws, outs, accs, scr):
        g = ins[0][...].astype(_F32)
        for k in range(1, np_):
            g = g + ins[k][...].astype(_F32)
        delta, nm, nv = _adamw(ins[np_][...], g, ins[np_ + 1][...], ins[np_ + 2][...])
        outs[0][...] = g
        outs[1][...] = delta
        outs[2][...] = nm
        outs[3][...] = nv

    return _row_call(name, body, rows // tm, list(parts) + [_rows(w, tm), _rows(m, tm), _rows(v, tm)],
                     [_out_rows(rows, c, _F32, tm)] * 4)


def _sum_parts(name, parts, rows, c, out_dtype):
    tm = _flat_tm(rows)

    def body(step, ti, ins, ws, outs, accs, scr):
        g = ins[0][...].astype(_F32)
        for k in range(1, len(ins)):
            g = g + ins[k][...].astype(_F32)
        outs[0][...] = g.astype(out_dtype)

    return _row_call(name, body, rows // tm, list(parts), [_out_rows(rows, c, out_dtype, tm)])[0]


def _pack(arrs, dtype):
    flat = jnp.concatenate([a.reshape(-1).astype(dtype) for a in arrs])
    n = flat.shape[0]
    rows = -(-n // _LANES)
    rows = -(-rows // 512) * 512 if rows >= 512 else -(-rows // 16) * 16
    return jnp.pad(flat, (0, rows * _LANES - n)).reshape(rows, _LANES)


def _unpack(flat, shapes, lead=()):
    v = flat.reshape(tuple(lead) + (-1,))
    out, off = [], 0
    for s in shapes:
        n = math.prod(s)
        out.append(v[..., off:off + n].reshape(tuple(lead) + tuple(s)))
        off += n
    return out


def _place():
    return lax.axis_index("x"), lax.axis_index("y"), lax.axis_index("c")


def _all_gather(name, block):
    r, c_ = block.shape

    def body(x_ref, out_ref, send_sems, recv_sems, local_sem):
        x, y, c = _place()
        me, sibling = (x, y, c), (x, y, 1 - c)
        chips = [(1 - x, y), (x, 1 - y), (1 - x, 1 - y)]

        def rows(px, py, pc):
            return out_ref.at[4 * px + 2 * py + pc]

        def copy(k, blk, to, src=None):
            return pltpu.make_async_remote_copy(
                src_ref=rows(*blk) if src is None else src, dst_ref=rows(*blk),
                send_sem=send_sems.at[k], recv_sem=recv_sems.at[k], device_id=to, device_id_type=_MESH)

        mine = pltpu.make_async_copy(x_ref, rows(*me), local_sem)
        mine.start()
        first = [copy(0, me, sibling, src=x_ref)]
        first += [copy(1 + j, me, (*chip, c), src=x_ref) for j, chip in enumerate(chips)]
        for cp in first:
            cp.start()
        passed = [copy(4 + j, (*chip, c), sibling) for j, chip in enumerate(chips)]
        for j, chip in enumerate(chips):
            copy(1 + j, (*chip, c), me).wait_recv()
            passed[j].start()
        copy(0, sibling, me).wait_recv()
        for j, chip in enumerate(chips):
            copy(4 + j, (*chip, 1 - c), me).wait_recv()
        for cp in first + passed:
            cp.wait_send()
        mine.wait()

    return pl.pallas_call(
        body, name=name, out_shape=jax.ShapeDtypeStruct((_NDEV, r, c_), block.dtype),
        in_specs=[pl.BlockSpec(memory_space=pl.ANY)], out_specs=pl.BlockSpec(memory_space=pl.ANY),
        scratch_shapes=[pltpu.SemaphoreType.DMA((7,)), pltpu.SemaphoreType.DMA((7,)), pltpu.SemaphoreType.DMA(())],
    )(block)


def _sibling_exchange(name, send):
    def body(s_ref, r_ref, send_sem, recv_sem):
        x, y, c = _place()
        cp = pltpu.make_async_remote_copy(src_ref=s_ref, dst_ref=r_ref, send_sem=send_sem, recv_sem=recv_sem,
                                          device_id=(x, y, 1 - c), device_id_type=_MESH)
        cp.start()
        cp.wait()

    return pl.pallas_call(
        body, name=name, out_shape=jax.ShapeDtypeStruct(send.shape, send.dtype),
        in_specs=[pl.BlockSpec(memory_space=pl.ANY)], out_specs=pl.BlockSpec(memory_space=pl.ANY),
        scratch_shapes=[pltpu.SemaphoreType.DMA(()), pltpu.SemaphoreType.DMA(())],
    )(send)


def _chip_exchange(name, p):
    _, r, c_ = p.shape

    def body(p_ref, r_ref, send_sems, recv_sems):
        x, y, c = _place()
        chips = [(1 - x, y), (x, 1 - y), (1 - x, 1 - y)]
        cps = []
        for j, (px, py) in enumerate(chips):
            cp = pltpu.make_async_remote_copy(
                src_ref=p_ref.at[2 * px + py], dst_ref=r_ref.at[j], send_sem=send_sems.at[j],
                recv_sem=recv_sems.at[j], device_id=(px, py, c), device_id_type=_MESH)
            cp.start()
            cps.append(cp)
        for cp in cps:
            cp.wait()

    return pl.pallas_call(
        body, name=name, out_shape=jax.ShapeDtypeStruct((3, r, c_), p.dtype),
        in_specs=[pl.BlockSpec(memory_space=pl.ANY)], out_specs=pl.BlockSpec(memory_space=pl.ANY),
        scratch_shapes=[pltpu.SemaphoreType.DMA((3,)), pltpu.SemaphoreType.DMA((3,))],
    )(p)


def _gather_full(gathered, names, shard_shapes, axes):
    parts = _unpack(gathered.reshape(_NDEV, -1), shard_shapes, lead=(_NDEV,))
    out = {}
    for nme, p, ax in zip(names, parts, axes):
        p = jnp.moveaxis(p, 0, ax)
        shp = p.shape
        out[nme] = p.reshape(shp[:ax] + (shp[ax] * shp[ax + 1],) + shp[ax + 2:])
    return out


def _dest_major(full, ax):
    shp = full.shape
    p = full.reshape(shp[:ax] + (_NDEV, shp[ax] // _NDEV) + shp[ax + 1:])
    return jnp.moveaxis(p, ax, 0)


def _tm(t, want):
    return min(t, want)


def _ffn_forward(tag, x, pre_g, wg, wu, wd, post_g):
    t = x.shape[0]
    h, a, b, s = _ffn_up(tag + "_up", x, pre_g, wg, wu, _tm(t, 256))
    f, x_out = _proj_norm_res(tag + "_down", s, x, post_g, wd, 0.5, _tm(t, 512))
    return x_out, dict(x=x, h=h, a=a, b=b, s=s, f=f)


def _ffn_backward(tag, dxo, sv, pre_g, wg, wu, wd, post_g):
    t = dxo.shape[0]
    da, db, df, dpost = _ffn_bwd_post(tag + "_bwd_post", dxo, sv['f'], sv['a'], sv['b'], post_g, wd, 0.5,
                                      _tm(t, 256))
    dx, dpre = _bwd_in_norm(tag + "_bwd_pre", [da, db], [wg, wu], sv['x'], dxo, pre_g, _tm(t, 512))
    tk = _tm(t, 512)
    grads = dict(pre_g=dpre[0], post_g=dpost[0],
                 w_gate=_xty(tag + "_dwg", sv['h'], da, tk), w_up=_xty(tag + "_dwu", sv['h'], db, tk),
                 w_down=_xty(tag + "_dwd", sv['s'], df, tk))
    return dx, grads


def _mixer_weights(w, l):
    lw = w['lru_conv_w'].shape[-1]
    sw = w['sc_conv_w'].shape[-1]
    gw = w['sgu_ln_g'].shape[-1]
    win = w['w_in'][l]
    cuts = [0, lw, 2 * lw, 2 * lw + 3 * sw + 2 * gw, win.shape[1]]
    p = dict(lw=lw, sw=sw, gw=gw,
             win=[win[:, cuts[k]:cuts[k + 1]] for k in range(4)],
             cw=w['lru_conv_w'][l], cb=w['lru_conv_b'][l][None, :],
             wa=w['lru_wa'][l], wx=w['lru_wx'][l],
             ba=w['lru_ba'][l], bx=w['lru_bx'][l], lam=w['lru_lambda'][l],
             wlo=w['lru_w_out'][l], scw=w['sc_conv_w'][l], wsc=w['sc_w_out'][l],
             lg=w['sgu_ln_g'][l][None, :], lb=w['sgu_ln_b'][l][None, :],
             ws=w['sgu_w_s'][l].astype(_MM), wst=jnp.swapaxes(w['sgu_w_s'][l], 1, 2).astype(_MM),
             bias=jnp.repeat(w['sgu_b'][l].T, gw // _HEADS, axis=1),
             wsg=w['sgu_w_out'][l], wo=w['w_o'][l],
             pre_g=w['mix_pre_g'][l][None, :], post_g=w['mix_post_g'][l][None, :])
    return p


def _mixer_forward(tag, x, p):
    t = x.shape[0]
    tl = _tm(t, 256)
    hm, zg, zx, zmid, zm = _mix_in(tag + "_in", x, p['pre_g'], p['win'], _tm(t, 512))
    lru = lambda d: (p['cw'], p['cb'], p['wa'][d], p['wx'][d], p['ba'][d:d + 1], p['bx'][d:d + 1],
                     p['lam'][d:d + 1])
    hf, = _lru_fwd(tag + "_lru_f", zx, *lru(0), tl, False)
    hb, pa = _lru_fwd(tag + "_lru_b", zx, *lru(1), tl, True, hf=hf, zg=zg)
    pb, pc = _bc_fwd(tag + "_bc", zmid, p['scw'], p['lg'], p['lb'], p['ws'], p['bias'], p['sw'], p['gw'], tl)
    ya, yb, yc, m = _mix_proj(tag + "_proj", pa, pb, pc, zm, p['wlo'], p['wsc'], p['wsg'], _tm(t, 512))
    mo, x_out = _proj_norm_res(tag + "_out", m, x, p['post_g'], p['wo'], 1.0, _tm(t, 512))
    sv = dict(x=x, hm=hm, zg=zg, zx=zx, zmid=zmid, zm=zm, hf=hf, hb=hb, pa=pa, pb=pb, pc=pc,
              ya=ya, yb=yb, yc=yc, m=m, mo=mo)
    return x_out, sv


def _mixer_backward(tag, dxo, sv, p):
    t = dxo.shape[0]
    tl = _tm(t, 256)
    tk = _tm(t, 512)
    dmo, dya, dyb, dyc, dzm, dpost = _mix_bwd_out(tag + "_bwd_out", dxo, sv['mo'], sv['ya'], sv['yb'], sv['yc'],
                                                  sv['zm'], p['post_g'], p['wo'], _tm(t, 512))
    dpa, dpb, dpc = _mix_bwd_proj(tag + "_bwd_proj", dya, dyb, dyc, p['wlo'], p['wsc'], p['wsg'], _tm(t, 512))
    dzmid, dscw, dlg, dlb, dws, dbias = _bc_bwd(tag + "_bc_bwd", sv['zmid'], dpb, dpc, p['scw'], p['lg'], p['lb'],
                                                p['ws'], p['wst'], p['bias'], p['sw'], p['gw'], tl)
    lru = lambda d: (p['cw'], p['cb'], p['wa'][d], p['wx'][d], p['ba'][d:d + 1], p['bx'][d:d + 1],
                     p['lam'][d:d + 1])
    dxc0, dzg, dwa0, dwx0, dba0, dbx0, dlam0 = _lru_bwd(tag + "_lru_bwd_f", sv['zx'], sv['zg'], dpa, sv['hf'],
                                                         sv['hb'], *lru(0), tl, 0)
    dxc, dwa1, dwx1, dba1, dbx1, dlam1 = _lru_bwd(tag + "_lru_bwd_b", sv['zx'], sv['zg'], dpa, sv['hb'],
                                                  sv['hf'], *lru(1), tl, 1, dxc_in=dxc0)
    dzx, dcw, dcb = _lru_conv_bwd(tag + "_conv_bwd", dxc, sv['zx'], p['cw'], tl)
    dzs = [dzg, dzx, dzmid, dzm]
    dx, dpre = _bwd_in_norm(tag + "_bwd_in", dzs, p['win'], sv['x'], dxo, p['pre_g'], _tm(t, 256))
    gh = p['gw'] // _HEADS
    grads = dict(
        mix_pre_g=dpre[0], mix_post_g=dpost[0],
        w_in=jnp.concatenate([_xty(tag + "_dwin%d" % k, sv['hm'], dz, tk) for k, dz in enumerate(dzs)], axis=1),
        lru_conv_w=dcw, lru_conv_b=dcb[0],
        lru_wa=jnp.stack([dwa0, dwa1]), lru_wx=jnp.stack([dwx0, dwx1]),
        lru_ba=jnp.concatenate([dba0, dba1]), lru_bx=jnp.concatenate([dbx0, dbx1]),
        lru_lambda=jnp.concatenate([dlam0, dlam1]),
        lru_w_out=_xty(tag + "_dwlo", sv['pa'], dya, tk),
        sc_conv_w=dscw, sc_w_out=_xty(tag + "_dwsc", sv['pb'], dyb, tk),
        sgu_ln_g=dlg[0], sgu_ln_b=dlb[0], sgu_w_s=dws,
        sgu_b=jnp.sum(dbias.reshape(_CHUNK, _HEADS, gh), axis=2).T,
        sgu_w_out=_xty(tag + "_dwsg", sv['pc'], dyc, tk),
        w_o=_xty(tag + "_dwo", sv['m'], dmo, tk))
    return dx, grads


def _local_step(x, target, w):
    depth = w['w_in'].shape[0]
    saved = []
    for l in range(depth):
        g = lambda nme: w[nme][l][None, :]
        x, s1 = _ffn_forward("l%d_ffn1" % l, x, g('ffn1_pre_g'), w['ffn1_w_gate'][l], w['ffn1_w_up'][l],
                             w['ffn1_w_down'][l], g('ffn1_post_g'))
        p = _mixer_weights(w, l)
        x, sm = _mixer_forward("l%d_mix" % l, x, p)
        x, s2 = _ffn_forward("l%d_ffn2" % l, x, g('ffn2_pre_g'), w['ffn2_w_gate'][l], w['ffn2_w_up'][l],
                             w['ffn2_w_down'][l], g('ffn2_post_g'))
        saved.append((s1, sm, s2, p))
    loss, dx = _loss_grad("loss", x, target, _tm(x.shape[0], 512))
    per_layer = []
    for l in reversed(range(depth)):
        s1, sm, s2, p = saved[l]
        g = lambda nme: w[nme][l][None, :]
        grads = {}
        dx, g2 = _ffn_backward("l%d_ffn2" % l, dx, s2, g('ffn2_pre_g'), w['ffn2_w_gate'][l], w['ffn2_w_up'][l],
                               w['ffn2_w_down'][l], g('ffn2_post_g'))
        grads.update({'ffn2_' + k: v for k, v in g2.items()})
        dx, gm = _mixer_backward("l%d_mix" % l, dx, sm, p)
        grads.update(gm)
        dx, g1 = _ffn_backward("l%d_ffn1" % l, dx, s1, g('ffn1_pre_g'), w['ffn1_w_gate'][l], w['ffn1_w_up'][l],
                               w['ffn1_w_down'][l], g('ffn1_post_g'))
        grads.update({'ffn1_' + k: v for k, v in g1.items()})
        per_layer.append(grads)
    per_layer.reverse()
    grads = {nme: jnp.stack([pg[nme] for pg in per_layer]) for nme in _WEIGHTS}
    return loss, dx, grads


def kernel(x, ffn1_pre_g, ffn1_w_gate, ffn1_w_up, ffn1_w_down, ffn1_post_g, mix_pre_g, w_in, lru_conv_w, lru_conv_b, lru_wa, lru_ba, lru_wx, lru_bx, lru_lambda, lru_w_out, sc_conv_w, sc_w_out, sgu_ln_g, sgu_ln_b, sgu_w_s, sgu_b, sgu_w_out, w_o, mix_post_g, ffn2_pre_g, ffn2_w_gate, ffn2_w_up, ffn2_w_down, ffn2_post_g, loss_target, m_ffn1_pre_g, m_ffn1_w_gate, m_ffn1_w_up, m_ffn1_w_down, m_ffn1_post_g, m_mix_pre_g, m_w_in, m_lru_conv_w, m_lru_conv_b, m_lru_wa, m_lru_ba, m_lru_wx, m_lru_bx, m_lru_lambda, m_lru_w_out, m_sc_conv_w, m_sc_w_out, m_sgu_ln_g, m_sgu_ln_b, m_sgu_w_s, m_sgu_b, m_sgu_w_out, m_w_o, m_mix_post_g, m_ffn2_pre_g, m_ffn2_w_gate, m_ffn2_w_up, m_ffn2_w_down, m_ffn2_post_g, v_ffn1_pre_g, v_ffn1_w_gate, v_ffn1_w_up, v_ffn1_w_down, v_ffn1_post_g, v_mix_pre_g, v_w_in, v_lru_conv_w, v_lru_conv_b, v_lru_wa, v_lru_ba, v_lru_wx, v_lru_bx, v_lru_lambda, v_lru_w_out, v_sc_conv_w, v_sc_w_out, v_sgu_ln_g, v_sgu_ln_b, v_sgu_w_s, v_sgu_b, v_sgu_w_out, v_w_o, v_mix_post_g, v_ffn2_pre_g, v_ffn2_w_gate, v_ffn2_w_up, v_ffn2_w_down, v_ffn2_post_g):
    args = locals()
    wts = {n: args[n] for n in _WEIGHTS}
    mom = {n: args['m_' + n] for n in _WEIGHTS}
    var = {n: args['v_' + n] for n in _WEIGHTS}
    cx, cy, cc = _place()
    dev = 4 * cx + 2 * cy + cc
    big, small = list(_BIG), list(_SMALL_SHARDED)

    g_big = _all_gather("gather_matrices", _pack([wts[n] for n in big], _MM))
    g_small = _all_gather("gather_vectors", _pack([wts[n] for n in small], _F32))
    full = dict(wts)
    full.update(_gather_full(g_big, big, [wts[n].shape for n in big], [_BIG[n] for n in big]))
    full.update(_gather_full(g_small, small, [wts[n].shape for n in small], [_SMALL_SHARDED[n] for n in small]))

    loss, grad_x, grads = _local_step(x[0], loss_target[0], full)
    loss = lax.psum(loss, ("x", "y", "c"))

    dm = jnp.concatenate([_dest_major(grads[n], _BIG[n]).reshape(_NDEV, -1) for n in big], axis=1)
    rows = _pack([wts[n] for n in big], _MM).shape[0]
    dm = jnp.pad(dm, ((0, 0), (0, rows * _LANES - dm.shape[1]))).reshape(4, 2, rows, _LANES)
    keep = lax.dynamic_index_in_dim(dm, cc, axis=1, keepdims=False)
    send = lax.dynamic_index_in_dim(dm, 1 - cc, axis=1, keepdims=False).astype(_MM)
    got = _sibling_exchange("reduce_sibling", send)
    tmf = _flat_tm(4 * rows)
    p = _sum_parts("reduce_pair_sum", [_rows(keep.reshape(4 * rows, _LANES), tmf),
                                       _rows(got.reshape(4 * rows, _LANES), tmf)], 4 * rows, _LANES, _MM)
    p = p.reshape(4, rows, _LANES)
    others = _chip_exchange("reduce_chips", p)
    own = lax.dynamic_index_in_dim(p, 2 * cx + cy, axis=0, keepdims=False)
    tmr = _flat_tm(rows)
    parts = [_rows(own, tmr)] + [_rows3(others, k, tmr) for k in range(3)]
    g_b, d_b, m_b, v_b = _sum_adamw("update_matrices", parts, _pack([wts[n] for n in big], _F32),
                                    _pack([mom[n] for n in big], _F32), _pack([var[n] for n in big], _F32))
    shapes_b = [wts[n].shape for n in big]
    out = {}
    for key, flat in (('grad_', g_b), ('delta_', d_b), ('new_m_', m_b), ('new_v_', v_b)):
        for n, a in zip(big, _unpack(flat, shapes_b)):
            out[key + n] = a

    vec = _REPLICATED + small
    part = _pack([grads[n] for n in vec], _F32)
    allp = _all_gather("gather_vector_grads", part)
    rv = part.shape[0]
    tmv = _flat_tm(rv)
    gsum = _sum_parts("reduce_vector_grads", [_rows3(allp, k, tmv) for k in range(_NDEV)], rv, _LANES, _F32)
    gfull = dict(zip(vec, _unpack(gsum, [grads[n].shape for n in vec])))
    gloc = []
    for n in vec:
        if n in _SMALL_SHARDED:
            ax = _SMALL_SHARDED[n]
            sz = wts[n].shape[ax]
            gloc.append(lax.dynamic_slice_in_dim(gfull[n], dev * sz, sz, axis=ax))
        else:
            gloc.append(gfull[n])
    gl = _pack(gloc, _F32)
    g_s, d_s, m_s, v_s = _sum_adamw("update_vectors", [_rows(gl, _flat_tm(gl.shape[0]))],
                                    _pack([wts[n] for n in vec], _F32), _pack([mom[n] for n in vec], _F32),
                                    _pack([var[n] for n in vec], _F32))
    shapes_s = [wts[n].shape for n in vec]
    for key, flat in (('grad_', g_s), ('delta_', d_s), ('new_m_', m_s), ('new_v_', v_s)):
        for n, a in zip(vec, _unpack(flat, shapes_s)):
            out[key + n] = a

    res = [loss, grad_x[None]]
    for key in ('grad_', 'delta_', 'new_m_', 'new_v_'):
        res += [out[key + n] for n in _WEIGHTS]
    return tuple(res)
```

```python
import functools
import math

import jax
import jax.numpy as jnp
from jax import lax
from jax.experimental import pallas as pl
from jax.experimental.pallas import tpu as pltpu

_F32 = jnp.float32
_MM = jnp.bfloat16
_EPS = 1e-6
_HEADS = 4
_CHUNK = 128
_LRU_C = 8.0
_HALO = 16
_LANES = 1024
_NDEV = 8
_VMEM_LIMIT = 56 * 1024 * 1024
_GELU_K = math.sqrt(2.0 / math.pi)
_GELU_C = 0.044715
_MESH = pl.DeviceIdType.MESH

_ADAM_LR, _ADAM_B1, _ADAM_B2, _ADAM_EPS, _ADAM_WD, _ADAM_STEP = 1e-3, 0.9, 0.999, 1e-8, 0.01, 10

_WEIGHTS = ['ffn1_pre_g', 'ffn1_w_gate', 'ffn1_w_up', 'ffn1_w_down', 'ffn1_post_g', 'mix_pre_g', 'w_in',
            'lru_conv_w', 'lru_conv_b', 'lru_wa', 'lru_ba', 'lru_wx', 'lru_bx', 'lru_lambda', 'lru_w_out',
            'sc_conv_w', 'sc_w_out', 'sgu_ln_g', 'sgu_ln_b', 'sgu_w_s', 'sgu_b', 'sgu_w_out', 'w_o',
            'mix_post_g', 'ffn2_pre_g', 'ffn2_w_gate', 'ffn2_w_up', 'ffn2_w_down', 'ffn2_post_g']
_BIG = {'ffn1_w_gate': 2, 'ffn1_w_up': 2, 'ffn1_w_down': 1, 'w_in': 2, 'lru_wa': 3, 'lru_wx': 3,
        'lru_w_out': 1, 'sc_w_out': 2, 'sgu_w_out': 2, 'w_o': 1,
        'ffn2_w_gate': 2, 'ffn2_w_up': 2, 'ffn2_w_down': 1}
_SMALL_SHARDED = {'lru_conv_w': 2, 'lru_ba': 2, 'lru_bx': 2, 'lru_lambda': 2, 'sc_conv_w': 2}
_REPLICATED = ['ffn1_pre_g', 'ffn1_post_g', 'mix_pre_g', 'lru_conv_b', 'sgu_ln_g', 'sgu_ln_b', 'sgu_w_s',
               'sgu_b', 'mix_post_g', 'ffn2_pre_g', 'ffn2_post_g']


def _dot(a, b):
    return jnp.dot(a, b, preferred_element_type=_F32)


def _dot_nt(a, b):
    return lax.dot_general(a, b, (((1,), (1,)), ((), ())), preferred_element_type=_F32)


def _dot_tn(a, b):
    return lax.dot_general(a, b, (((0,), (0,)), ((), ())), preferred_element_type=_F32)


def _sigmoid(x):
    return jax.nn.sigmoid(x)


def _gelu(x):
    t = jnp.tanh(_GELU_K * (x + _GELU_C * x * x * x))
    return 0.5 * x * (1.0 + t)


def _gelu_grad(x):
    x2 = x * x
    t = jnp.tanh(_GELU_K * (x + _GELU_C * x * x2))
    return 0.5 * (1.0 + t) + 0.5 * x * (1.0 - t * t) * (_GELU_K * (1.0 + 3.0 * _GELU_C * x2))


def _rms_fwd(x, g):
    r = lax.rsqrt(jnp.mean(x * x, axis=-1, keepdims=True) + _EPS)
    return x * r * g


def _rms_bwd(dy, x, g):
    r = lax.rsqrt(jnp.mean(x * x, axis=-1, keepdims=True) + _EPS)
    xh = x * r
    dxh = dy * g
    dx = r * (dxh - xh * jnp.mean(dxh * xh, axis=-1, keepdims=True))
    return dx, jnp.sum(dy * xh, axis=0, keepdims=True)


def _neg_softplus_neg(lam):
    e = jnp.exp(-jnp.abs(lam))
    l1p = jnp.where(e < 1e-2, e * (1.0 - e * (0.5 - e * (1.0 / 3.0 - 0.25 * e))), jnp.log(1.0 + e))
    return -_LRU_C * (jnp.maximum(-lam, 0.0) + l1p)


def _shift_rows(xe, d, tm):
    n = xe.shape[0]
    if d == 0:
        return xe[_HALO:_HALO + tm]
    return pltpu.roll(xe, (-d) % n, axis=0)[_HALO:_HALO + tm]


def _with_halo(cur, prev, nxt, first, last):
    p = jnp.where(first, 0.0, prev.astype(_F32))
    n = jnp.where(last, 0.0, nxt.astype(_F32))
    return jnp.concatenate([p, cur.astype(_F32), n], axis=0)


def _rows(arr, tm):
    c = arr.shape[1]
    return (arr, (tm, c), lambda ti: (ti, 0))


def _rows3(arr, k, tm):
    c = arr.shape[2]
    return (arr, (None, tm, c), lambda ti, k=k: (k, ti, 0))


def _halo_prev(arr, tm):
    c = arr.shape[1]
    return (arr, (_HALO, c), lambda ti: (jnp.maximum(ti * (tm // _HALO) - 1, 0), 0))


def _halo_next(arr, tm):
    c = arr.shape[1]
    nblk = arr.shape[0] // _HALO
    return (arr, (_HALO, c), lambda ti: (jnp.minimum((ti + 1) * (tm // _HALO), nblk - 1), 0))


def _full(arr):
    nd = arr.ndim
    return (arr, arr.shape, lambda ti, nd=nd: (0,) * nd)


def _out_rows(t, c, dtype, tm):
    return ((t, c), dtype, (tm, c), lambda ti: (ti, 0))


def _row_call(name, body, n_tiles, ins, outs, accs=(), hbm=(), scratch=(), reverse=False):
    n_in, n_hbm, n_out, n_acc = len(ins), len(hbm), len(outs), len(accs)

    def tile_of(step):
        return (n_tiles - 1 - step) if reverse else step

    def spec(block, index_fn):
        return pl.BlockSpec(block, lambda s, f=index_fn: f(tile_of(s)))

    def kern(*refs):
        in_refs = refs[:n_in]
        hbm_refs = refs[n_in:n_in + n_hbm]
        out_refs = refs[n_in + n_hbm:n_in + n_hbm + n_out]
        acc_refs = refs[n_in + n_hbm + n_out:n_in + n_hbm + n_out + n_acc]
        rest = refs[n_in + n_hbm + n_out + n_acc:]
        w_refs, scr = rest[:n_hbm], rest[n_hbm:]
        step = pl.program_id(0)

        @pl.when(step == 0)
        def _():
            for src, dst in zip(hbm_refs, w_refs):
                pltpu.sync_copy(src, dst)
            for a in acc_refs:
                a[...] = jnp.zeros(a.shape, a.dtype)

        body(step, tile_of(step), in_refs, w_refs, out_refs, acc_refs, scr)

    in_specs = [spec(b, f) for (_, b, f) in ins] + [pl.BlockSpec(memory_space=pl.ANY)] * n_hbm
    out_specs = [spec(b, f) for (_, _, b, f) in outs]
    out_specs += [pl.BlockSpec(s, lambda st, nd=len(s): (0,) * nd) for s in accs]
    out_shape = [jax.ShapeDtypeStruct(s, d) for (s, d, _, _) in outs]
    out_shape += [jax.ShapeDtypeStruct(s, _F32) for s in accs]
    scratch_shapes = [pltpu.VMEM(w.shape, w.dtype) for w in hbm] + list(scratch)
    res = pl.pallas_call(
        kern, name=name, grid=(n_tiles,), in_specs=in_specs, out_specs=out_specs, out_shape=out_shape,
        scratch_shapes=scratch_shapes,
        compiler_params=pltpu.CompilerParams(dimension_semantics=("arbitrary",), vmem_limit_bytes=_VMEM_LIMIT),
    )(*[a for (a, _, _) in ins], *hbm)
    return list(res)


def _xty(name, x, y, tk):
    t, k1 = x.shape
    k2 = y.shape[1]

    def kern(x_ref, y_ref, o_ref):
        @pl.when(pl.program_id(0) == 0)
        def _():
            o_ref[...] = jnp.zeros(o_ref.shape, o_ref.dtype)

        o_ref[...] += _dot_tn(x_ref[...], y_ref[...])

    return pl.pallas_call(
        kern, name=name, grid=(t // tk,),
        in_specs=[pl.BlockSpec((tk, k1), lambda k: (k, 0)), pl.BlockSpec((tk, k2), lambda k: (k, 0))],
        out_specs=pl.BlockSpec((k1, k2), lambda k: (0, 0)),
        out_shape=jax.ShapeDtypeStruct((k1, k2), _F32),
        compiler_params=pltpu.CompilerParams(dimension_semantics=("arbitrary",), vmem_limit_bytes=_VMEM_LIMIT),
    )(x, y)


def _ffn_up(name, x, pre_g, wg, wu, tm):
    t, d = x.shape
    f = wg.shape[1]

    def body(step, ti, ins, ws, outs, accs, scr):
        x_ref, g_ref = ins
        h = _rms_fwd(x_ref[...], g_ref[...]).astype(_MM)
        a = _dot(h, ws[0][...])
        b = _dot(h, ws[1][...])
        outs[0][...] = h
        outs[1][...] = a.astype(_MM)
        outs[2][...] = b.astype(_MM)
        outs[3][...] = (a * _sigmoid(a) * b).astype(_MM)

    return _row_call(name, body, t // tm, [_rows(x, tm), _full(pre_g)],
                     [_out_rows(t, d, _MM, tm), _out_rows(t, f, _MM, tm), _out_rows(t, f, _MM, tm),
                      _out_rows(t, f, _MM, tm)], hbm=[wg, wu])


def _proj_norm_res(name, lhs, x, post_g, w, scale, tm):
    t, d = x.shape

    def body(step, ti, ins, ws, outs, accs, scr):
        l_ref, x_ref, g_ref = ins
        f = _dot(l_ref[...], ws[0][...])
        outs[0][...] = f
        outs[1][...] = x_ref[...] + scale * _rms_fwd(f, g_ref[...])

    return _row_call(name, body, t // tm, [_rows(lhs, tm), _rows(x, tm), _full(post_g)],
                     [_out_rows(t, d, _F32, tm), _out_rows(t, d, _F32, tm)], hbm=[w])


def _ffn_bwd_post(name, dxo, f, a, b, post_g, wd, scale, tm):
    t, d = dxo.shape
    ff = a.shape[1]

    def body(step, ti, ins, ws, outs, accs, scr):
        dxo_ref, f_ref, a_ref, b_ref, g_ref = ins
        df, dg = _rms_bwd(scale * dxo_ref[...], f_ref[...], g_ref[...])
        accs[0][...] += dg
        dfb = df.astype(_MM)
        ds = _dot_nt(dfb, ws[0][...])
        a32 = a_ref[...].astype(_F32)
        b32 = b_ref[...].astype(_F32)
        sg = _sigmoid(a32)
        outs[0][...] = (ds * b32 * (sg * (1.0 + a32 * (1.0 - sg)))).astype(_MM)
        outs[1][...] = (ds * (a32 * sg)).astype(_MM)
        outs[2][...] = dfb

    return _row_call(name, body, t // tm,
                     [_rows(dxo, tm), _rows(f, tm), _rows(a, tm), _rows(b, tm), _full(post_g)],
                     [_out_rows(t, ff, _MM, tm), _out_rows(t, ff, _MM, tm), _out_rows(t, d, _MM, tm)],
                     accs=[(1, d)], hbm=[wd])


def _bwd_in_norm(name, dzs, ws_list, x, dxo, pre_g, tm):
    t, d = x.shape
    nz = len(dzs)

    def body(step, ti, ins, ws, outs, accs, scr):
        dh = _dot_nt(ins[0][...], ws[0][...])
        for k in range(1, nz):
            dh = dh + _dot_nt(ins[k][...], ws[k][...])
        x_ref, dxo_ref, g_ref = ins[nz:]
        dx, dg = _rms_bwd(dh, x_ref[...], g_ref[...])
        accs[0][...] += dg
        outs[0][...] = dxo_ref[...] + dx

    return _row_call(name, body, t // tm,
                     [_rows(z, tm) for z in dzs] + [_rows(x, tm), _rows(dxo, tm), _full(pre_g)],
                     [_out_rows(t, d, _F32, tm)], accs=[(1, d)], hbm=list(ws_list))


def _mix_in(name, x, pre_g, w_parts, tm):
    t, d = x.shape

    def body(step, ti, ins, ws, outs, accs, scr):
        x_ref, g_ref = ins
        h = _rms_fwd(x_ref[...], g_ref[...]).astype(_MM)
        outs[0][...] = h
        for k in range(len(ws)):
            outs[1 + k][...] = _dot(h, ws[k][...]).astype(_MM)

    return _row_call(name, body, t // tm, [_rows(x, tm), _full(pre_g)],
                     [_out_rows(t, d, _MM, tm)] + [_out_rows(t, w.shape[1], _MM, tm) for w in w_parts],
                     hbm=list(w_parts))


def _lru_conv(xe, cw, cb, tm):
    xc = cb
    for k in range(4):
        xc = xc + _shift_rows(xe, k - 2, tm) * cw[k:k + 1, :]
    return xc


def _lru_gates(xc, wa_ref, wx_ref, ba, bx, c):
    dh = xc.shape[1] // _HEADS
    gas, gxs = [], []
    for hh in range(_HEADS):
        xs = xc[:, hh * dh:(hh + 1) * dh].astype(_MM)
        gas.append(_dot(xs, wa_ref[hh]))
        gxs.append(_dot(xs, wx_ref[hh]))
    r = _sigmoid(jnp.concatenate(gas, axis=1) + ba)
    i = _sigmoid(jnp.concatenate(gxs, axis=1) + bx)
    la = c * r
    a = jnp.exp(la)
    y = 2.0 * la
    em = jnp.where(y > -0.05, -y * (1.0 + y * (0.5 + y * (1.0 / 6.0 + y * (1.0 / 24.0)))), 1.0 - a * a)
    return r, i, a, em


def _tile_scan(a_scr, u_scr, h_ref, carry, tm, descending):
    ng = tm // 8
    w = a_scr.shape[1]
    row = lax.broadcasted_iota(jnp.int32, (8, w), 0)

    def grp(j, carry):
        g = (ng - 1 - j) if descending else j
        r0 = pl.multiple_of(g * 8, 8)
        a8 = a_scr[pl.ds(r0, 8), :]
        u8 = u_scr[pl.ds(r0, 8), :]
        for dd in (1, 2, 4):
            if descending:
                ok = row < 8 - dd
                sh = 8 - dd
            else:
                ok = row >= dd
                sh = dd
            a_s = jnp.where(ok, pltpu.roll(a8, sh, axis=0), 1.0)
            u_s = jnp.where(ok, pltpu.roll(u8, sh, axis=0), 0.0)
            u8 = a8 * u_s + u8
            a8 = a8 * a_s
        h8 = u8 + a8 * carry
        h_ref[pl.ds(r0, 8), :] = h8
        return h8[0:1, :] if descending else h8[7:8, :]

    return lax.fori_loop(0, ng, grp, carry)


def _lru_fwd(name, zx, cw, cb, wa, wx, ba, bx, lam, tm, descending, hf=None, zg=None):
    t, w = zx.shape
    n = t // tm

    def body(step, ti, ins, ws, outs, accs, scr):
        zc, zp, zn, cw_r, cb_r, wa_r, wx_r, ba_r, bx_r, lam_r = ins[:10]
        a_scr, u_scr, carry_scr = scr
        xe = _with_halo(zc[...], zp[...], zn[...], ti == 0, ti == n - 1)
        xc = _lru_conv(xe, cw_r[...], cb_r[...], tm)
        c = _neg_softplus_neg(lam_r[...])
        r, i, a, em = _lru_gates(xc, wa_r, wx_r, ba_r[...], bx_r[...], c)
        a_scr[...] = a
        u_scr[...] = i * xc * jnp.sqrt(em)

        @pl.when(step == 0)
        def _():
            carry_scr[...] = jnp.zeros(carry_scr.shape, _F32)

        carry_scr[...] = _tile_scan(a_scr, u_scr, outs[0], carry_scr[...], tm, descending)
        if descending:
            hf_r, zg_r = ins[10:]
            outs[1][...] = ((hf_r[...] + outs[0][...]) * _gelu(zg_r[...].astype(_F32))).astype(_MM)

    ins = [_rows(zx, tm), _halo_prev(zx, tm), _halo_next(zx, tm), _full(cw), _full(cb), _full(wa), _full(wx),
           _full(ba), _full(bx), _full(lam)]
    outs = [_out_rows(t, w, _F32, tm)]
    if descending:
        ins += [_rows(hf, tm), _rows(zg, tm)]
        outs += [_out_rows(t, w, _MM, tm)]
    scratch = [pltpu.VMEM((tm, w), _F32), pltpu.VMEM((tm, w), _F32), pltpu.VMEM((1, w), _F32)]
    return _row_call(name, body, n, ins, outs, scratch=scratch, reverse=descending)


def _lru_bwd(name, zx, zg, dpa, h_own, h_other, cw, cb, wa, wx, ba, bx, lam, tm, direction, dxc_in=None):
    t, w = zx.shape
    n = t // tm
    dh_ = w // _HEADS
    adj_desc = direction == 0

    def body(step, ti, ins, ws, outs, accs, scr):
        (zc, zp, zn, zg_r, dpa_r, ho_r, hh_r, hoth_r,
         cw_r, cb_r, wa_r, wx_r, ba_r, bx_r, lam_r) = ins[:15]
        a_scr, u_scr, p_scr, carry_scr = scr
        first, last = ti == 0, ti == n - 1
        xe = _with_halo(zc[...], zp[...], zn[...], first, last)
        xc = _lru_conv(xe, cw_r[...], cb_r[...], tm)
        lam_v = lam_r[...]
        c = _neg_softplus_neg(lam_v)
        r, i, a, em = _lru_gates(xc, wa_r, wx_r, ba_r[...], bx_r[...], c)
        m = jnp.sqrt(em)
        zg32 = zg_r[...].astype(_F32)
        dpa_v = dpa_r[...]
        d_h = dpa_v * _gelu(zg32)
        a_scr[...] = a
        u_scr[...] = a * d_h

        @pl.when(step == 0)
        def _():
            carry_scr[...] = jnp.zeros(carry_scr.shape, _F32)

        carry_in = carry_scr[...]
        carry_scr[...] = _tile_scan(a_scr, u_scr, p_scr, carry_in, tm, adj_desc)
        p = p_scr[...]
        row = lax.broadcasted_iota(jnp.int32, (tm, w), 0)
        h_t = ho_r[...]
        if adj_desc:
            p_nb = jnp.where(row == tm - 1, carry_in, pltpu.roll(p, tm - 1, axis=0))
            edge = jnp.where(first, 0.0, hh_r[_HALO - 1:_HALO, :])
            h_nb = jnp.where(row == 0, edge, pltpu.roll(h_t, 1, axis=0))
        else:
            p_nb = jnp.where(row == 0, carry_in, pltpu.roll(p, 1, axis=0))
            edge = jnp.where(last, 0.0, hh_r[0:1, :])
            h_nb = jnp.where(row == tm - 1, edge, pltpu.roll(h_t, tm - 1, axis=0))
        g = d_h + p_nb
        gi = g * i
        d_i = g * xc * m
        dxc = gi * m
        d_m = gi * xc
        d_l = g * h_nb * a - d_m * (1.0 - em) / m
        accs[4][...] += jnp.sum(d_l * r, axis=0, keepdims=True)
        dga = d_l * c * r * (1.0 - r)
        dgx = d_i * i * (1.0 - i)
        accs[2][...] += jnp.sum(dga, axis=0, keepdims=True)
        accs[3][...] += jnp.sum(dgx, axis=0, keepdims=True)
        parts = []
        for hh in range(_HEADS):
            sl = slice(hh * dh_, (hh + 1) * dh_)
            xs = xc[:, sl].astype(_MM)
            da_h = dga[:, sl].astype(_MM)
            dx_h = dgx[:, sl].astype(_MM)
            accs[0][hh] += _dot_tn(xs, da_h)
            accs[1][hh] += _dot_tn(xs, dx_h)
            parts.append(_dot_nt(da_h, wa_r[hh]) + _dot_nt(dx_h, wx_r[hh]))
        dxc = dxc + jnp.concatenate(parts, axis=1)
        if direction == 0:
            outs[0][...] = dxc
            outs[1][...] = (dpa_v * (h_t + hoth_r[...]) * _gelu_grad(zg32)).astype(_MM)
        else:
            outs[0][...] = dxc + ins[15][...]

        @pl.when(step == n - 1)
        def _():
            accs[4][...] = accs[4][...] * (_LRU_C * _sigmoid(-lam_v))

    halo_h = _halo_prev(h_own, tm) if adj_desc else _halo_next(h_own, tm)
    ins = [_rows(zx, tm), _halo_prev(zx, tm), _halo_next(zx, tm), _rows(zg, tm), _rows(dpa, tm),
           _rows(h_own, tm), halo_h, _rows(h_other, tm),
           _full(cw), _full(cb), _full(wa), _full(wx), _full(ba), _full(bx), _full(lam)]
    outs = [_out_rows(t, w, _F32, tm)]
    if direction == 0:
        outs += [_out_rows(t, w, _MM, tm)]
    else:
        ins += [_rows(dxc_in, tm)]
    accs = [(_HEADS, dh_, dh_), (_HEADS, dh_, dh_), (1, w), (1, w), (1, w)]
    scratch = [pltpu.VMEM((tm, w), _F32), pltpu.VMEM((tm, w), _F32), pltpu.VMEM((tm, w), _F32),
               pltpu.VMEM((1, w), _F32)]
    return _row_call(name, body, n, ins, outs, accs=accs, scratch=scratch, reverse=adj_desc)


def _lru_conv_bwd(name, dxc, zx, cw, tm):
    t, w = zx.shape
    n = t // tm

    def body(step, ti, ins, ws, outs, accs, scr):
        dc, dp, dn, zc, zp, zn, cw_r = ins
        first, last = ti == 0, ti == n - 1
        de = _with_halo(dc[...], dp[...], dn[...], first, last)
        ze = _with_halo(zc[...], zp[...], zn[...], first, last)
        cw_v = cw_r[...]
        d_cur = dc[...]
        dz = None
        for k in range(4):
            term = _shift_rows(de, 2 - k, tm) * cw_v[k:k + 1, :]
            dz = term if dz is None else dz + term
            accs[0][k:k + 1, :] += jnp.sum(d_cur * _shift_rows(ze, k - 2, tm), axis=0, keepdims=True)
        accs[1][...] += jnp.sum(d_cur, axis=0, keepdims=True)
        outs[0][...] = dz.astype(_MM)

    ins = [_rows(dxc, tm), _halo_prev(dxc, tm), _halo_next(dxc, tm),
           _rows(zx, tm), _halo_prev(zx, tm), _halo_next(zx, tm), _full(cw)]
    return _row_call(name, body, n, ins, [_out_rows(t, w, _MM, tm)], accs=[(4, w), (1, w)])


def _sgu_mix(v2, ws_ref, bias, mixed_scr, tm):
    gw = v2.shape[1]
    gh = gw // _HEADS
    for nn in range(tm // _CHUNK):
        rs = slice(nn * _CHUNK, (nn + 1) * _CHUNK)
        for g in range(_HEADS):
            cs = slice(g * gh, (g + 1) * gh)
            mixed_scr[rs, cs] = _dot(ws_ref[g], v2[rs, cs].astype(_MM)) + bias[:, cs]
    return mixed_scr[...]


def _ln_fwd(v1, lg, lb):
    mu = jnp.mean(v1, axis=-1, keepdims=True)
    vc = v1 - mu
    rs = lax.rsqrt(jnp.mean(vc * vc, axis=-1, keepdims=True) + _EPS)
    vn = vc * rs
    return vn * lg + lb, vn, rs


def _bc_fwd(name, zmid, scw, lg, lb, ws_mm, bias, sw, gw, tm):
    t = zmid.shape[0]
    n = t // tm

    def body(step, ti, ins, ws, outs, accs, scr):
        zc, zp, zn, scw_r, lg_r, lb_r, ws_r, bias_r = ins
        first, last = ti == 0, ti == n - 1
        z = zc[...].astype(_F32)
        zb, zcc, zxx = z[:, 0:sw], z[:, sw:2 * sw], z[:, 2 * sw:3 * sw]
        zu, zv = z[:, 3 * sw:3 * sw + gw], z[:, 3 * sw + gw:3 * sw + 2 * gw]
        zpv, znv = zp[...].astype(_F32), zn[...].astype(_F32)
        qe = _with_halo(zcc * zxx, zpv[:, sw:2 * sw] * zpv[:, 2 * sw:3 * sw],
                        znv[:, sw:2 * sw] * znv[:, 2 * sw:3 * sw], first, last)
        scw_v = scw_r[...]
        cq = None
        for k in range(3):
            term = _shift_rows(qe, k - 1, tm) * scw_v[k:k + 1, :]
            cq = term if cq is None else cq + term
        outs[0][...] = (zb * cq).astype(_MM)
        v2, _, _ = _ln_fwd(_gelu(zv), lg_r[...], lb_r[...])
        mixed = _sgu_mix(v2, ws_r, bias_r[...], scr[0], tm)
        outs[1][...] = (_gelu(zu) * mixed).astype(_MM)

    ins = [_rows(zmid, tm), _halo_prev(zmid, tm), _halo_next(zmid, tm), _full(scw), _full(lg), _full(lb),
           _full(ws_mm), _full(bias)]
    return _row_call(name, body, n, ins, [_out_rows(t, sw, _MM, tm), _out_rows(t, gw, _MM, tm)],
                     scratch=[pltpu.VMEM((tm, gw), _F32)])


def _bc_bwd(name, zmid, dpb, dpc, scw, lg, lb, ws_mm, wst_mm, bias, sw, gw, tm):
    t = zmid.shape[0]
    n = t // tm
    gh = gw // _HEADS

    def body(step, ti, ins, ws, outs, accs, scr):
        zc, zp, zn, db_c, db_p, db_n, dc_r, scw_r, lg_r, lb_r, ws_r, wst_r, bias_r = ins
        mixed_scr, dv2_scr = scr
        first, last = ti == 0, ti == n - 1
        z = zc[...].astype(_F32)
        zb, zcc, zxx = z[:, 0:sw], z[:, sw:2 * sw], z[:, 2 * sw:3 * sw]
        zu, zv = z[:, 3 * sw:3 * sw + gw], z[:, 3 * sw + gw:3 * sw + 2 * gw]
        zpv, znv = zp[...].astype(_F32), zn[...].astype(_F32)
        qe = _with_halo(zcc * zxx, zpv[:, sw:2 * sw] * zpv[:, 2 * sw:3 * sw],
                        znv[:, sw:2 * sw] * znv[:, 2 * sw:3 * sw], first, last)
        dpb_v = db_c[...]
        dcq = dpb_v * zb
        dcqe = _with_halo(dcq, db_p[...] * zpv[:, 0:sw], db_n[...] * znv[:, 0:sw], first, last)
        scw_v = scw_r[...]
        cq, dq = None, None
        for k in range(3):
            qk = _shift_rows(qe, k - 1, tm)
            term = qk * scw_v[k:k + 1, :]
            cq = term if cq is None else cq + term
            dterm = _shift_rows(dcqe, 1 - k, tm) * scw_v[k:k + 1, :]
            dq = dterm if dq is None else dq + dterm
            accs[0][k:k + 1, :] += jnp.sum(dcq * qk, axis=0, keepdims=True)
        outs[0][:, 0:sw] = (dpb_v * cq).astype(_MM)
        outs[0][:, sw:2 * sw] = (dq * zxx).astype(_MM)
        outs[0][:, 2 * sw:3 * sw] = (dq * zcc).astype(_MM)
        lg_v = lg_r[...]
        v2, vn, rs = _ln_fwd(_gelu(zv), lg_v, lb_r[...])
        mixed = _sgu_mix(v2, ws_r, bias_r[...], mixed_scr, tm)
        dpc_v = dc_r[...]
        outs[0][:, 3 * sw:3 * sw + gw] = (dpc_v * mixed * _gelu_grad(zu)).astype(_MM)
        dmix = dpc_v * _gelu(zu)
        for nn in range(tm // _CHUNK):
            rsl = slice(nn * _CHUNK, (nn + 1) * _CHUNK)
            accs[4][...] += dmix[rsl, :]
            for g in range(_HEADS):
                cs = slice(g * gh, (g + 1) * gh)
                dm_b = dmix[rsl, cs].astype(_MM)
                accs[3][g] += _dot_nt(dm_b, v2[rsl, cs].astype(_MM))
                dv2_scr[rsl, cs] = _dot(wst_r[g], dm_b)
        dv2 = dv2_scr[...]
        accs[1][...] += jnp.sum(dv2 * vn, axis=0, keepdims=True)
        accs[2][...] += jnp.sum(dv2, axis=0, keepdims=True)
        dvn = dv2 * lg_v
        dv1 = rs * (dvn - jnp.mean(dvn, axis=-1, keepdims=True)
                    - vn * jnp.mean(dvn * vn, axis=-1, keepdims=True))
        outs[0][:, 3 * sw + gw:3 * sw + 2 * gw] = (dv1 * _gelu_grad(zv)).astype(_MM)

    ins = [_rows(zmid, tm), _halo_prev(zmid, tm), _halo_next(zmid, tm),
           _rows(dpb, tm), _halo_prev(dpb, tm), _halo_next(dpb, tm), _rows(dpc, tm),
           _full(scw), _full(lg), _full(lb), _full(ws_mm), _full(wst_mm), _full(bias)]
    accs = [(3, sw), (1, gw), (1, gw), (_HEADS, _CHUNK, _CHUNK), (_CHUNK, gw)]
    return _row_call(name, body, n, ins, [_out_rows(t, 3 * sw + 2 * gw, _MM, tm)], accs=accs,
                     scratch=[pltpu.VMEM((tm, gw), _F32), pltpu.VMEM((tm, gw), _F32)])


def _mix_proj(name, pa, pb, pc, zm, wlo, wsc, wsg, tm):
    t = pa.shape[0]
    d = wlo.shape[1]

    def body(step, ti, ins, ws, outs, accs, scr):
        ys = [_dot(ins[k][...], ws[k][...]) for k in range(3)]
        gm = _sigmoid(ins[3][...].astype(_F32))
        m = None
        for k in range(3):
            outs[k][...] = ys[k].astype(_MM)
            term = gm[:, k * d:(k + 1) * d] * ys[k]
            m = term if m is None else m + term
        outs[3][...] = m.astype(_MM)

    return _row_call(name, body, t // tm, [_rows(pa, tm), _rows(pb, tm), _rows(pc, tm), _rows(zm, tm)],
                     [_out_rows(t, d, _MM, tm)] * 4, hbm=[wlo, wsc, wsg])


def _mix_bwd_out(name, dxo, mo, ya, yb, yc, zm, post_g, wo, tm):
    t, d = dxo.shape

    def body(step, ti, ins, ws, outs, accs, scr):
        dxo_ref, mo_ref, ya_r, yb_r, yc_r, zm_r, g_ref = ins
        dmo, dg = _rms_bwd(dxo_ref[...], mo_ref[...], g_ref[...])
        accs[0][...] += dg
        dmob = dmo.astype(_MM)
        outs[0][...] = dmob
        dm = _dot_nt(dmob, ws[0][...])
        gm = _sigmoid(zm_r[...].astype(_F32))
        for k, y_r in enumerate((ya_r, yb_r, yc_r)):
            gk = gm[:, k * d:(k + 1) * d]
            outs[1 + k][...] = (dm * gk).astype(_MM)
            outs[4][:, k * d:(k + 1) * d] = (dm * y_r[...].astype(_F32) * gk * (1.0 - gk)).astype(_MM)

    ins = [_rows(dxo, tm), _rows(mo, tm), _rows(ya, tm), _rows(yb, tm), _rows(yc, tm), _rows(zm, tm),
           _full(post_g)]
    return _row_call(name, body, t // tm, ins,
                     [_out_rows(t, d, _MM, tm)] * 4 + [_out_rows(t, 3 * d, _MM, tm)], accs=[(1, d)], hbm=[wo])


def _mix_bwd_proj(name, dya, dyb, dyc, wlo, wsc, wsg, tm):
    t = dya.shape[0]

    def body(step, ti, ins, ws, outs, accs, scr):
        for k in range(3):
            outs[k][...] = _dot_nt(ins[k][...], ws[k][...])

    return _row_call(name, body, t // tm, [_rows(dya, tm), _rows(dyb, tm), _rows(dyc, tm)],
                     [_out_rows(t, w.shape[0], _F32, tm) for w in (wlo, wsc, wsg)], hbm=[wlo, wsc, wsg])


def _loss_grad(name, y, target, tm):
    t, d = y.shape

    def body(step, ti, ins, ws, outs, accs, scr):
        err = ins[0][...] - ins[1][...]
        outs[0][...] = err * (1.0 / d)
        accs[0][...] += (0.5 / d) * jnp.sum(err * err)

    dy, acc = _row_call(name, body, t // tm, [_rows(y, tm), _rows(target, tm)], [_out_rows(t, d, _F32, tm)],
                        accs=[(1, 128)])
    return acc[0, 0], dy


def _adamw(w, g, m, v):
    m = _ADAM_B1 * m + (1.0 - _ADAM_B1) * g
    v = _ADAM_B2 * v + (1.0 - _ADAM_B2) * (g * g)
    m_hat = m / (1.0 - _ADAM_B1 ** _ADAM_STEP)
    v_hat = v / (1.0 - _ADAM_B2 ** _ADAM_STEP)
    delta = -_ADAM_LR * (m_hat / (jnp.sqrt(v_hat) + _ADAM_EPS) + _ADAM_WD * w)
    return delta, m, v


def _flat_tm(rows):
    return 512 if rows % 512 == 0 else rows


def _sum_adamw(name, parts, w, m, v):
    rows, c = w.shape
    tm = _flat_tm(rows)
    np_ = len(parts)

    def body(step, ti, ins, ws, outs, accs, scr):
        g = ins[0][...].astype(_F32)
        for k in range(1, np_):
            g = g + ins[k][...].astype(_F32)
        delta, nm, nv = _adamw(ins[np_][...], g, ins[np_ + 1][...], ins[np_ + 2][...])
        outs[0][...] = g
        outs[1][...] = delta
        outs[2][...] = nm
        outs[3][...] = nv

    return _row_call(name, body, rows // tm, list(parts) + [_rows(w, tm), _rows(m, tm), _rows(v, tm)],
                     [_out_rows(rows, c, _F32, tm)] * 4)


def _sum_parts(name, parts, rows, c, out_dtype):
    tm = _flat_tm(rows)

    def body(step, ti, ins, ws, outs, accs, scr):
        g = ins[0][...].astype(_F32)
        for k in range(1, len(ins)):
            g = g + ins[k][...].astype(_F32)
        outs[0][...] = g.astype(out_dtype)

    return _row_call(name, body, rows // tm, list(parts), [_out_rows(rows, c, out_dtype, tm)])[0]


def _pack(arrs, dtype):
    flat = jnp.concatenate([a.reshape(-1).astype(dtype) for a in arrs])
    return _to_rows(flat, _flat_rows(flat.shape[0]))


def _pack_rows(arrs, dtype):
    parts = [a.reshape(-1, _LANES).astype(dtype) for a in arrs]
    n = sum(p.shape[0] for p in parts)
    rows = _flat_rows(n * _LANES)
    if rows > n:
        parts.append(jnp.zeros((rows - n, _LANES), dtype))
    return jnp.concatenate(parts, axis=0)


def _unpack_rows(flat, shapes):
    out, off = [], 0
    for s in shapes:
        n = math.prod(s) // _LANES
        out.append(flat[off:off + n].reshape(tuple(s)))
        off += n
    return out


def _unpack(flat, shapes):
    v = flat.reshape(-1)
    out, off = [], 0
    for s in shapes:
        n = math.prod(s)
        out.append(v[off:off + n].reshape(tuple(s)))
        off += n
    return out


def _place():
    return lax.axis_index("x"), lax.axis_index("y"), lax.axis_index("c")


def _all_gather(name, block):
    r, c_ = block.shape

    def body(x_ref, out_ref, send_sems, recv_sems, local_sem):
        x, y, c = _place()
        me, sibling = (x, y, c), (x, y, 1 - c)
        chips = [(1 - x, y), (x, 1 - y), (1 - x, 1 - y)]

        def rows(px, py, pc):
            return out_ref.at[4 * px + 2 * py + pc]

        def copy(k, blk, to, src=None):
            return pltpu.make_async_remote_copy(
                src_ref=rows(*blk) if src is None else src, dst_ref=rows(*blk),
                send_sem=send_sems.at[k], recv_sem=recv_sems.at[k], device_id=to, device_id_type=_MESH)

        mine = pltpu.make_async_copy(x_ref, rows(*me), local_sem)
        mine.start()
        first = [copy(0, me, sibling, src=x_ref)]
        first += [copy(1 + j, me, (*chip, c), src=x_ref) for j, chip in enumerate(chips)]
        for cp in first:
            cp.start()
        passed = [copy(4 + j, (*chip, c), sibling) for j, chip in enumerate(chips)]
        for j, chip in enumerate(chips):
            copy(1 + j, (*chip, c), me).wait_recv()
            passed[j].start()
        copy(0, sibling, me).wait_recv()
        for j, chip in enumerate(chips):
            copy(4 + j, (*chip, 1 - c), me).wait_recv()
        for cp in first + passed:
            cp.wait_send()
        mine.wait()

    return pl.pallas_call(
        body, name=name, out_shape=jax.ShapeDtypeStruct((_NDEV, r, c_), block.dtype),
        in_specs=[pl.BlockSpec(memory_space=pl.ANY)], out_specs=pl.BlockSpec(memory_space=pl.ANY),
        scratch_shapes=[pltpu.SemaphoreType.DMA((7,)), pltpu.SemaphoreType.DMA((7,)), pltpu.SemaphoreType.DMA(())],
    )(block)


def _sibling_exchange(name, dm):
    _, _, r, c_ = dm.shape

    def body(d_ref, r_ref, send_sems, recv_sems):
        x, y, c = _place()
        cps = []
        for k in range(4):
            cp = pltpu.make_async_remote_copy(
                src_ref=d_ref.at[k, 1 - c], dst_ref=r_ref.at[k], send_sem=send_sems.at[k],
                recv_sem=recv_sems.at[k], device_id=(x, y, 1 - c), device_id_type=_MESH)
            cp.start()
            cps.append(cp)
        for cp in cps:
            cp.wait()

    return pl.pallas_call(
        body, name=name, out_shape=jax.ShapeDtypeStruct((4, r, c_), dm.dtype),
        in_specs=[pl.BlockSpec(memory_space=pl.ANY)], out_specs=pl.BlockSpec(memory_space=pl.ANY),
        scratch_shapes=[pltpu.SemaphoreType.DMA((4,)), pltpu.SemaphoreType.DMA((4,))],
    )(dm)


def _pair_sum(name, dm, got, core):
    _, _, r, c_ = dm.shape
    tm = _flat_tm(r)

    def kern(core_ref, a_ref, b_ref, o_ref):
        o_ref[...] = (a_ref[...] + b_ref[...]).astype(o_ref.dtype)

    grid_spec = pltpu.PrefetchScalarGridSpec(
        num_scalar_prefetch=1, grid=(4, r // tm),
        in_specs=[pl.BlockSpec((None, None, tm, c_), lambda k, i, cr: (k, cr[0], i, 0)),
                  pl.BlockSpec((None, tm, c_), lambda k, i, cr: (k, i, 0))],
        out_specs=pl.BlockSpec((None, tm, c_), lambda k, i, cr: (k, i, 0)))
    return pl.pallas_call(
        kern, name=name, grid_spec=grid_spec, out_shape=jax.ShapeDtypeStruct((4, r, c_), _MM),
        compiler_params=pltpu.CompilerParams(dimension_semantics=("arbitrary", "arbitrary"),
                                             vmem_limit_bytes=_VMEM_LIMIT),
    )(core, dm, got)


def _chip_sum_adamw(name, p, others, chip, w, m, v):
    r, c_ = w.shape
    tm = _flat_tm(r)

    def kern(chip_ref, p_ref, o0, o1, o2, w_ref, m_ref, v_ref, g_out, d_out, m_out, v_out):
        g = p_ref[...].astype(_F32) + o0[...].astype(_F32) + o1[...].astype(_F32) + o2[...].astype(_F32)
        delta, nm, nv = _adamw(w_ref[...], g, m_ref[...], v_ref[...])
        g_out[...] = g
        d_out[...] = delta
        m_out[...] = nm
        v_out[...] = nv

    flat = pl.BlockSpec((tm, c_), lambda i, cr: (i, 0))
    grid_spec = pltpu.PrefetchScalarGridSpec(
        num_scalar_prefetch=1, grid=(r // tm,),
        in_specs=[pl.BlockSpec((None, tm, c_), lambda i, cr: (cr[0], i, 0))]
        + [pl.BlockSpec((None, tm, c_), lambda i, cr, k=k: (k, i, 0)) for k in range(3)] + [flat] * 3,
        out_specs=[flat] * 4)
    return pl.pallas_call(
        kern, name=name, grid_spec=grid_spec, out_shape=[jax.ShapeDtypeStruct((r, c_), _F32)] * 4,
        compiler_params=pltpu.CompilerParams(dimension_semantics=("arbitrary",), vmem_limit_bytes=_VMEM_LIMIT),
    )(chip, p, others, others, others, w, m, v)


def _chip_exchange(name, p):
    _, r, c_ = p.shape

    def body(p_ref, r_ref, send_sems, recv_sems):
        x, y, c = _place()
        chips = [(1 - x, y), (x, 1 - y), (1 - x, 1 - y)]
        cps = []
        for j, (px, py) in enumerate(chips):
            cp = pltpu.make_async_remote_copy(
                src_ref=p_ref.at[2 * px + py], dst_ref=r_ref.at[j], send_sem=send_sems.at[j],
                recv_sem=recv_sems.at[j], device_id=(px, py, c), device_id_type=_MESH)
            cp.start()
            cps.append(cp)
        for cp in cps:
            cp.wait()

    return pl.pallas_call(
        body, name=name, out_shape=jax.ShapeDtypeStruct((3, r, c_), p.dtype),
        in_specs=[pl.BlockSpec(memory_space=pl.ANY)], out_specs=pl.BlockSpec(memory_space=pl.ANY),
        scratch_shapes=[pltpu.SemaphoreType.DMA((3,)), pltpu.SemaphoreType.DMA((3,))],
    )(p)


def _gather_full(gathered, names, shard_shapes, axes, unpack):
    per_dev = [unpack(gathered[d], shard_shapes) for d in range(_NDEV)]
    return {nme: jnp.concatenate([per_dev[d][i] for d in range(_NDEV)], axis=ax)
            for i, (nme, ax) in enumerate(zip(names, axes))}


def _block_rows(a, ax, d):
    s = a.shape[ax] // _NDEV
    return lax.slice_in_dim(a, d * s, (d + 1) * s, axis=ax).reshape(-1, _LANES)


def _to_rows(flat, rows):
    return jnp.pad(flat, (0, rows * _LANES - flat.shape[0])).reshape(rows, _LANES)


def _flat_rows(n):
    rows = -(-n // _LANES)
    return -(-rows // 512) * 512 if rows >= 512 else -(-rows // 16) * 16


def _tm(t, want):
    return min(t, want)


def _ffn_forward(tag, x, pre_g, wg, wu, wd, post_g):
    t = x.shape[0]
    h, a, b, s = _ffn_up(tag + "_up", x, pre_g, wg, wu, _tm(t, 256))
    f, x_out = _proj_norm_res(tag + "_down", s, x, post_g, wd, 0.5, _tm(t, 512))
    return x_out, dict(x=x, h=h, a=a, b=b, s=s, f=f)


def _ffn_backward(tag, dxo, sv, pre_g, wg, wu, wd, post_g):
    t = dxo.shape[0]
    da, db, df, dpost = _ffn_bwd_post(tag + "_bwd_post", dxo, sv['f'], sv['a'], sv['b'], post_g, wd, 0.5,
                                      _tm(t, 256))
    dx, dpre = _bwd_in_norm(tag + "_bwd_pre", [da, db], [wg, wu], sv['x'], dxo, pre_g, _tm(t, 512))
    tk = _tm(t, 512)
    grads = dict(pre_g=dpre[0], post_g=dpost[0],
                 w_gate=_xty(tag + "_dwg", sv['h'], da, tk), w_up=_xty(tag + "_dwu", sv['h'], db, tk),
                 w_down=_xty(tag + "_dwd", sv['s'], df, tk))
    return dx, grads


def _mixer_weights(w, l):
    lw = w['lru_conv_w'].shape[-1]
    sw = w['sc_conv_w'].shape[-1]
    gw = w['sgu_ln_g'].shape[-1]
    win = w['w_in'][l]
    cuts = [0, lw, 2 * lw, 2 * lw + 3 * sw + 2 * gw, win.shape[1]]
    p = dict(lw=lw, sw=sw, gw=gw,
             win=[win[:, cuts[k]:cuts[k + 1]] for k in range(4)],
             cw=w['lru_conv_w'][l], cb=w['lru_conv_b'][l][None, :],
             wa=w['lru_wa'][l], wx=w['lru_wx'][l],
             ba=w['lru_ba'][l], bx=w['lru_bx'][l], lam=w['lru_lambda'][l],
             wlo=w['lru_w_out'][l], scw=w['sc_conv_w'][l], wsc=w['sc_w_out'][l],
             lg=w['sgu_ln_g'][l][None, :], lb=w['sgu_ln_b'][l][None, :],
             ws=w['sgu_w_s'][l].astype(_MM), wst=jnp.swapaxes(w['sgu_w_s'][l], 1, 2).astype(_MM),
             bias=jnp.repeat(w['sgu_b'][l].T, gw // _HEADS, axis=1),
             wsg=w['sgu_w_out'][l], wo=w['w_o'][l],
             pre_g=w['mix_pre_g'][l][None, :], post_g=w['mix_post_g'][l][None, :])
    return p


def _mixer_forward(tag, x, p):
    t = x.shape[0]
    tl = _tm(t, 256)
    hm, zg, zx, zmid, zm = _mix_in(tag + "_in", x, p['pre_g'], p['win'], _tm(t, 512))
    lru = lambda d: (p['cw'], p['cb'], p['wa'][d], p['wx'][d], p['ba'][d:d + 1], p['bx'][d:d + 1],
                     p['lam'][d:d + 1])
    hf, = _lru_fwd(tag + "_lru_f", zx, *lru(0), tl, False)
    hb, pa = _lru_fwd(tag + "_lru_b", zx, *lru(1), tl, True, hf=hf, zg=zg)
    pb, pc = _bc_fwd(tag + "_bc", zmid, p['scw'], p['lg'], p['lb'], p['ws'], p['bias'], p['sw'], p['gw'], tl)
    ya, yb, yc, m = _mix_proj(tag + "_proj", pa, pb, pc, zm, p['wlo'], p['wsc'], p['wsg'], _tm(t, 512))
    mo, x_out = _proj_norm_res(tag + "_out", m, x, p['post_g'], p['wo'], 1.0, _tm(t, 512))
    sv = dict(x=x, hm=hm, zg=zg, zx=zx, zmid=zmid, zm=zm, hf=hf, hb=hb, pa=pa, pb=pb, pc=pc,
              ya=ya, yb=yb, yc=yc, m=m, mo=mo)
    return x_out, sv


def _mixer_backward(tag, dxo, sv, p):
    t = dxo.shape[0]
    tl = _tm(t, 256)
    tk = _tm(t, 512)
    dmo, dya, dyb, dyc, dzm, dpost = _mix_bwd_out(tag + "_bwd_out", dxo, sv['mo'], sv['ya'], sv['yb'], sv['yc'],
                                                  sv['zm'], p['post_g'], p['wo'], _tm(t, 512))
    dpa, dpb, dpc = _mix_bwd_proj(tag + "_bwd_proj", dya, dyb, dyc, p['wlo'], p['wsc'], p['wsg'], _tm(t, 512))
    dzmid, dscw, dlg, dlb, dws, dbias = _bc_bwd(tag + "_bc_bwd", sv['zmid'], dpb, dpc, p['scw'], p['lg'], p['lb'],
                                                p['ws'], p['wst'], p['bias'], p['sw'], p['gw'], tl)
    lru = lambda d: (p['cw'], p['cb'], p['wa'][d], p['wx'][d], p['ba'][d:d + 1], p['bx'][d:d + 1],
                     p['lam'][d:d + 1])
    dxc0, dzg, dwa0, dwx0, dba0, dbx0, dlam0 = _lru_bwd(tag + "_lru_bwd_f", sv['zx'], sv['zg'], dpa, sv['hf'],
                                                         sv['hb'], *lru(0), tl, 0)
    dxc, dwa1, dwx1, dba1, dbx1, dlam1 = _lru_bwd(tag + "_lru_bwd_b", sv['zx'], sv['zg'], dpa, sv['hb'],
                                                  sv['hf'], *lru(1), tl, 1, dxc_in=dxc0)
    dzx, dcw, dcb = _lru_conv_bwd(tag + "_conv_bwd", dxc, sv['zx'], p['cw'], tl)
    dzs = [dzg, dzx, dzmid, dzm]
    dx, dpre = _bwd_in_norm(tag + "_bwd_in", dzs, p['win'], sv['x'], dxo, p['pre_g'], _tm(t, 256))
    gh = p['gw'] // _HEADS
    grads = dict(
        mix_pre_g=dpre[0], mix_post_g=dpost[0],
        w_in=jnp.concatenate([_xty(tag + "_dwin%d" % k, sv['hm'], dz, tk) for k, dz in enumerate(dzs)], axis=1),
        lru_conv_w=dcw, lru_conv_b=dcb[0],
        lru_wa=jnp.stack([dwa0, dwa1]), lru_wx=jnp.stack([dwx0, dwx1]),
        lru_ba=jnp.concatenate([dba0, dba1]), lru_bx=jnp.concatenate([dbx0, dbx1]),
        lru_lambda=jnp.concatenate([dlam0, dlam1]),
        lru_w_out=_xty(tag + "_dwlo", sv['pa'], dya, tk),
        sc_conv_w=dscw, sc_w_out=_xty(tag + "_dwsc", sv['pb'], dyb, tk),
        sgu_ln_g=dlg[0], sgu_ln_b=dlb[0], sgu_w_s=dws,
        sgu_b=jnp.sum(dbias.reshape(_CHUNK, _HEADS, gh), axis=2).T,
        sgu_w_out=_xty(tag + "_dwsg", sv['pc'], dyc, tk),
        w_o=_xty(tag + "_dwo", sv['m'], dmo, tk))
    return dx, grads


def _local_step(x, target, w):
    depth = w['w_in'].shape[0]
    saved = []
    for l in range(depth):
        g = lambda nme: w[nme][l][None, :]
        x, s1 = _ffn_forward("l%d_ffn1" % l, x, g('ffn1_pre_g'), w['ffn1_w_gate'][l], w['ffn1_w_up'][l],
                             w['ffn1_w_down'][l], g('ffn1_post_g'))
        p = _mixer_weights(w, l)
        x, sm = _mixer_forward("l%d_mix" % l, x, p)
        x, s2 = _ffn_forward("l%d_ffn2" % l, x, g('ffn2_pre_g'), w['ffn2_w_gate'][l], w['ffn2_w_up'][l],
                             w['ffn2_w_down'][l], g('ffn2_post_g'))
        saved.append((s1, sm, s2, p))
    loss, dx = _loss_grad("loss", x, target, _tm(x.shape[0], 512))
    per_layer = []
    for l in reversed(range(depth)):
        s1, sm, s2, p = saved[l]
        g = lambda nme: w[nme][l][None, :]
        grads = {}
        dx, g2 = _ffn_backward("l%d_ffn2" % l, dx, s2, g('ffn2_pre_g'), w['ffn2_w_gate'][l], w['ffn2_w_up'][l],
                               w['ffn2_w_down'][l], g('ffn2_post_g'))
        grads.update({'ffn2_' + k: v for k, v in g2.items()})
        dx, gm = _mixer_backward("l%d_mix" % l, dx, sm, p)
        grads.update(gm)
        dx, g1 = _ffn_backward("l%d_ffn1" % l, dx, s1, g('ffn1_pre_g'), w['ffn1_w_gate'][l], w['ffn1_w_up'][l],
                               w['ffn1_w_down'][l], g('ffn1_post_g'))
        grads.update({'ffn1_' + k: v for k, v in g1.items()})
        per_layer.append(grads)
    per_layer.reverse()
    return loss, dx, per_layer


def kernel(x, ffn1_pre_g, ffn1_w_gate, ffn1_w_up, ffn1_w_down, ffn1_post_g, mix_pre_g, w_in, lru_conv_w, lru_conv_b, lru_wa, lru_ba, lru_wx, lru_bx, lru_lambda, lru_w_out, sc_conv_w, sc_w_out, sgu_ln_g, sgu_ln_b, sgu_w_s, sgu_b, sgu_w_out, w_o, mix_post_g, ffn2_pre_g, ffn2_w_gate, ffn2_w_up, ffn2_w_down, ffn2_post_g, loss_target, m_ffn1_pre_g, m_ffn1_w_gate, m_ffn1_w_up, m_ffn1_w_down, m_ffn1_post_g, m_mix_pre_g, m_w_in, m_lru_conv_w, m_lru_conv_b, m_lru_wa, m_lru_ba, m_lru_wx, m_lru_bx, m_lru_lambda, m_lru_w_out, m_sc_conv_w, m_sc_w_out, m_sgu_ln_g, m_sgu_ln_b, m_sgu_w_s, m_sgu_b, m_sgu_w_out, m_w_o, m_mix_post_g, m_ffn2_pre_g, m_ffn2_w_gate, m_ffn2_w_up, m_ffn2_w_down, m_ffn2_post_g, v_ffn1_pre_g, v_ffn1_w_gate, v_ffn1_w_up, v_ffn1_w_down, v_ffn1_post_g, v_mix_pre_g, v_w_in, v_lru_conv_w, v_lru_conv_b, v_lru_wa, v_lru_ba, v_lru_wx, v_lru_bx, v_lru_lambda, v_lru_w_out, v_sc_conv_w, v_sc_w_out, v_sgu_ln_g, v_sgu_ln_b, v_sgu_w_s, v_sgu_b, v_sgu_w_out, v_w_o, v_mix_post_g, v_ffn2_pre_g, v_ffn2_w_gate, v_ffn2_w_up, v_ffn2_w_down, v_ffn2_post_g):
    args = locals()
    wts = {n: args[n] for n in _WEIGHTS}
    mom = {n: args['m_' + n] for n in _WEIGHTS}
    var = {n: args['v_' + n] for n in _WEIGHTS}
    cx, cy, cc = _place()
    dev = 4 * cx + 2 * cy + cc
    big, small = list(_BIG), list(_SMALL_SHARDED)

    g_big = _all_gather("gather_matrices", _pack_rows([wts[n] for n in big], _MM))
    g_small = _all_gather("gather_vectors", _pack([wts[n] for n in small], _F32))
    full = dict(wts)
    full.update(_gather_full(g_big, big, [wts[n].shape for n in big], [_BIG[n] for n in big], _unpack_rows))
    full.update(_gather_full(g_small, small, [wts[n].shape for n in small], [_SMALL_SHARDED[n] for n in small],
                             _unpack))

    loss, grad_x, grads = _local_step(x[0], loss_target[0], full)
    loss = lax.psum(loss, ("x", "y", "c"))

    depth = len(grads)
    rows = _flat_rows(sum(math.prod(wts[n].shape) for n in big))
    pieces = []
    for d in range(_NDEV):
        blocks = [_block_rows(grads[l][n], _BIG[n] - 1, d) for n in big for l in range(depth)]
        fill = rows - sum(b.shape[0] for b in blocks)
        pieces += blocks + ([jnp.zeros((fill, _LANES), _F32)] if fill else [])
    dm = jnp.concatenate(pieces, axis=0).reshape(4, 2, rows, _LANES)
    got = _sibling_exchange("reduce_sibling", dm)
    p = _pair_sum("reduce_pair_sum", dm, got, jnp.reshape(cc, (1,)).astype(jnp.int32))
    others = _chip_exchange("reduce_chips", p)
    g_b, d_b, m_b, v_b = _chip_sum_adamw("update_matrices", p, others,
                                         jnp.reshape(2 * cx + cy, (1,)).astype(jnp.int32),
                                         _pack_rows([wts[n] for n in big], _F32),
                                         _pack_rows([mom[n] for n in big], _F32),
                                         _pack_rows([var[n] for n in big], _F32))
    shapes_b = [wts[n].shape for n in big]
    out = {}
    for key, flat in (('grad_', g_b), ('delta_', d_b), ('new_m_', m_b), ('new_v_', v_b)):
        for n, a in zip(big, _unpack_rows(flat, shapes_b)):
            out[key + n] = a

    vec = _REPLICATED + small
    gvec = {n: jnp.stack([g[n] for g in grads]) for n in vec}
    part = _pack([gvec[n] for n in vec], _F32)
    allp = _all_gather("gather_vector_grads", part)
    rv = part.shape[0]
    tmv = _flat_tm(rv)
    gsum = _sum_parts("reduce_vector_grads", [_rows3(allp, k, tmv) for k in range(_NDEV)], rv, _LANES, _F32)
    gfull = dict(zip(vec, _unpack(gsum, [gvec[n].shape for n in vec])))
    gloc = []
    for n in vec:
        if n in _SMALL_SHARDED:
            ax = _SMALL_SHARDED[n]
            sz = wts[n].shape[ax]
            gloc.append(lax.dynamic_slice_in_dim(gfull[n], dev * sz, sz, axis=ax))
        else:
            gloc.append(gfull[n])
    gl = _pack(gloc, _F32)
    g_s, d_s, m_s, v_s = _sum_adamw("update_vectors", [_rows(gl, _flat_tm(gl.shape[0]))],
                                    _pack([wts[n] for n in vec], _F32), _pack([mom[n] for n in vec], _F32),
                                    _pack([var[n] for n in vec], _F32))
    shapes_s = [wts[n].shape for n in vec]
    for key, flat in (('grad_', g_s), ('delta_', d_s), ('new_m_', m_s), ('new_v_', v_s)):
        for n, a in zip(vec, _unpack(flat, shapes_s)):
            out[key + n] = a

    res = [loss, grad_x[None]]
    for key in ('grad_', 'delta_', 'new_m_', 'new_v_'):
        res += [out[key + n] for n in _WEIGHTS]
    return tuple(res)
```

```python
import functools
import math

import jax
import jax.numpy as jnp
from jax import lax
from jax.experimental import pallas as pl
from jax.experimental.pallas import tpu as pltpu

_F32 = jnp.float32
_MM = jnp.bfloat16
_EPS = 1e-6
_HEADS = 4
_CHUNK = 128
_LRU_C = 8.0
_HALO = 16
_LANES = 1024
_NDEV = 8
_VMEM_LIMIT = 56 * 1024 * 1024
_GELU_K = math.sqrt(2.0 / math.pi)
_GELU_C = 0.044715
_MESH = pl.DeviceIdType.MESH

_ADAM_LR, _ADAM_B1, _ADAM_B2, _ADAM_EPS, _ADAM_WD, _ADAM_STEP = 1e-3, 0.9, 0.999, 1e-8, 0.01, 10

_WEIGHTS = ['ffn1_pre_g', 'ffn1_w_gate', 'ffn1_w_up', 'ffn1_w_down', 'ffn1_post_g', 'mix_pre_g', 'w_in',
            'lru_conv_w', 'lru_conv_b', 'lru_wa', 'lru_ba', 'lru_wx', 'lru_bx', 'lru_lambda', 'lru_w_out',
            'sc_conv_w', 'sc_w_out', 'sgu_ln_g', 'sgu_ln_b', 'sgu_w_s', 'sgu_b', 'sgu_w_out', 'w_o',
            'mix_post_g', 'ffn2_pre_g', 'ffn2_w_gate', 'ffn2_w_up', 'ffn2_w_down', 'ffn2_post_g']
_BIG = {'ffn1_w_gate': 2, 'ffn1_w_up': 2, 'ffn1_w_down': 1, 'w_in': 2, 'lru_wa': 3, 'lru_wx': 3,
        'lru_w_out': 1, 'sc_w_out': 2, 'sgu_w_out': 2, 'w_o': 1,
        'ffn2_w_gate': 2, 'ffn2_w_up': 2, 'ffn2_w_down': 1}
_SMALL_SHARDED = {'lru_conv_w': 2, 'lru_ba': 2, 'lru_bx': 2, 'lru_lambda': 2, 'sc_conv_w': 2}
_REPLICATED = ['ffn1_pre_g', 'ffn1_post_g', 'mix_pre_g', 'lru_conv_b', 'sgu_ln_g', 'sgu_ln_b', 'sgu_w_s',
               'sgu_b', 'mix_post_g', 'ffn2_pre_g', 'ffn2_post_g']


def _dot(a, b):
    return jnp.dot(a, b, preferred_element_type=_F32)


def _dot_nt(a, b):
    return lax.dot_general(a, b, (((1,), (1,)), ((), ())), preferred_element_type=_F32)


def _dot_tn(a, b):
    return lax.dot_general(a, b, (((0,), (0,)), ((), ())), preferred_element_type=_F32)


def _sigmoid(x):
    return 0.5 * jnp.tanh(0.5 * x) + 0.5


def _gelu(x):
    t = jnp.tanh(x * (_GELU_K + (_GELU_K * _GELU_C) * (x * x)))
    return (0.5 * x) * (1.0 + t)


def _gelu_and_grad(x):
    x2 = x * x
    t = jnp.tanh(x * (_GELU_K + (_GELU_K * _GELU_C) * x2))
    hx = 0.5 * x
    return hx * (1.0 + t), 0.5 * (1.0 + t) + hx * (1.0 - t * t) * (_GELU_K + (3.0 * _GELU_K * _GELU_C) * x2)


def _gelu_grad(x):
    return _gelu_and_grad(x)[1]


def _rms_fwd(x, g):
    r = lax.rsqrt(jnp.mean(x * x, axis=-1, keepdims=True) + _EPS)
    return x * r * g


def _rms_bwd(dy, x, g):
    r = lax.rsqrt(jnp.mean(x * x, axis=-1, keepdims=True) + _EPS)
    xh = x * r
    dxh = dy * g
    dx = r * (dxh - xh * jnp.mean(dxh * xh, axis=-1, keepdims=True))
    return dx, jnp.sum(dy * xh, axis=0, keepdims=True)


def _neg_softplus_neg(lam):
    e = jnp.exp(-jnp.abs(lam))
    l1p = jnp.where(e < 1e-2, e * (1.0 - e * (0.5 - e * (1.0 / 3.0 - 0.25 * e))), jnp.log(1.0 + e))
    return -_LRU_C * (jnp.maximum(-lam, 0.0) + l1p)


def _shift_rows(xe, d, tm):
    n = xe.shape[0]
    if d == 0:
        return xe[_HALO:_HALO + tm]
    return pltpu.roll(xe, (-d) % n, axis=0)[_HALO:_HALO + tm]


def _with_halo(cur, prev, nxt, first, last):
    p = jnp.where(first, 0.0, prev.astype(_F32))
    n = jnp.where(last, 0.0, nxt.astype(_F32))
    return jnp.concatenate([p, cur.astype(_F32), n], axis=0)


def _rows(arr, tm):
    c = arr.shape[1]
    return (arr, (tm, c), lambda ti: (ti, 0))


def _rows3(arr, k, tm):
    c = arr.shape[2]
    return (arr, (None, tm, c), lambda ti, k=k: (k, ti, 0))


def _halo_prev(arr, tm):
    c = arr.shape[1]
    return (arr, (_HALO, c), lambda ti: (jnp.maximum(ti * (tm // _HALO) - 1, 0), 0))


def _halo_next(arr, tm):
    c = arr.shape[1]
    nblk = arr.shape[0] // _HALO
    return (arr, (_HALO, c), lambda ti: (jnp.minimum((ti + 1) * (tm // _HALO), nblk - 1), 0))


def _full(arr):
    nd = arr.ndim
    return (arr, arr.shape, lambda ti, nd=nd: (0,) * nd)


def _out_rows(t, c, dtype, tm):
    return ((t, c), dtype, (tm, c), lambda ti: (ti, 0))


def _row_call(name, body, n_tiles, ins, outs, accs=(), hbm=(), scratch=(), reverse=False, big_accs=()):
    n_in, n_hbm, n_out, n_acc, n_big = len(ins), len(hbm), len(outs), len(accs), len(big_accs)

    def tile_of(step):
        return (n_tiles - 1 - step) if reverse else step

    def spec(block, index_fn):
        return pl.BlockSpec(block, lambda s, f=index_fn: f(tile_of(s)))

    def kern(*refs):
        in_refs = refs[:n_in]
        hbm_refs = refs[n_in:n_in + n_hbm]
        o0 = n_in + n_hbm
        out_refs = refs[o0:o0 + n_out]
        acc_refs = refs[o0 + n_out:o0 + n_out + n_acc]
        big_out = refs[o0 + n_out + n_acc:o0 + n_out + n_acc + n_big]
        rest = refs[o0 + n_out + n_acc + n_big:]
        w_refs, big_refs, scr = rest[:n_hbm], rest[n_hbm:n_hbm + n_big], rest[n_hbm + n_big:]
        step = pl.program_id(0)

        @pl.when(step == 0)
        def _():
            for src, dst in zip(hbm_refs, w_refs):
                pltpu.sync_copy(src, dst)
            for a in tuple(acc_refs) + tuple(big_refs):
                a[...] = jnp.zeros(a.shape, a.dtype)

        body(step, tile_of(step), in_refs, w_refs, out_refs, tuple(acc_refs) + tuple(big_refs), scr)

        if n_big:
            @pl.when(step == n_tiles - 1)
            def _():
                for src, dst in zip(big_refs, big_out):
                    pltpu.sync_copy(src, dst)

    in_specs = [spec(b, f) for (_, b, f) in ins] + [pl.BlockSpec(memory_space=pl.ANY)] * n_hbm
    out_specs = [spec(b, f) for (_, _, b, f) in outs]
    out_specs += [pl.BlockSpec(s, lambda st, nd=len(s): (0,) * nd) for s in accs]
    out_specs += [pl.BlockSpec(memory_space=pl.ANY)] * n_big
    out_shape = [jax.ShapeDtypeStruct(s, d) for (s, d, _, _) in outs]
    out_shape += [jax.ShapeDtypeStruct(s, _F32) for s in tuple(accs) + tuple(big_accs)]
    scratch_shapes = [pltpu.VMEM(w.shape, w.dtype) for w in hbm]
    scratch_shapes += [pltpu.VMEM(s, _F32) for s in big_accs] + list(scratch)
    res = pl.pallas_call(
        kern, name=name, grid=(n_tiles,), in_specs=in_specs, out_specs=out_specs, out_shape=out_shape,
        scratch_shapes=scratch_shapes,
        compiler_params=pltpu.CompilerParams(dimension_semantics=("arbitrary",), vmem_limit_bytes=_VMEM_LIMIT),
    )(*[a for (a, _, _) in ins], *hbm)
    return list(res)


def _xty(name, x, y, tk):
    t, k1 = x.shape
    k2 = y.shape[1]

    def kern(x_ref, y_ref, o_ref):
        @pl.when(pl.program_id(0) == 0)
        def _():
            o_ref[...] = jnp.zeros(o_ref.shape, o_ref.dtype)

        o_ref[...] += _dot_tn(x_ref[...], y_ref[...])

    return pl.pallas_call(
        kern, name=name, grid=(t // tk,),
        in_specs=[pl.BlockSpec((tk, k1), lambda k: (k, 0)), pl.BlockSpec((tk, k2), lambda k: (k, 0))],
        out_specs=pl.BlockSpec((k1, k2), lambda k: (0, 0)),
        out_shape=jax.ShapeDtypeStruct((k1, k2), _F32),
        compiler_params=pltpu.CompilerParams(dimension_semantics=("arbitrary",), vmem_limit_bytes=_VMEM_LIMIT),
    )(x, y)


def _ffn_up(name, x, pre_g, wg, wu, tm):
    t, d = x.shape
    f = wg.shape[1]

    def body(step, ti, ins, ws, outs, accs, scr):
        x_ref, g_ref = ins
        h = _rms_fwd(x_ref[...], g_ref[...]).astype(_MM)
        a = _dot(h, ws[0][...])
        b = _dot(h, ws[1][...])
        outs[0][...] = h
        outs[1][...] = a.astype(_MM)
        outs[2][...] = b.astype(_MM)
        outs[3][...] = (a * _sigmoid(a) * b).astype(_MM)

    return _row_call(name, body, t // tm, [_rows(x, tm), _full(pre_g)],
                     [_out_rows(t, d, _MM, tm), _out_rows(t, f, _MM, tm), _out_rows(t, f, _MM, tm),
                      _out_rows(t, f, _MM, tm)], hbm=[wg, wu])


def _proj_norm_res(name, lhs, x, post_g, w, scale, tm):
    t, d = x.shape

    def body(step, ti, ins, ws, outs, accs, scr):
        l_ref, x_ref, g_ref = ins
        f = _dot(l_ref[...], ws[0][...])
        outs[0][...] = f
        outs[1][...] = x_ref[...] + scale * _rms_fwd(f, g_ref[...])

    return _row_call(name, body, t // tm, [_rows(lhs, tm), _rows(x, tm), _full(post_g)],
                     [_out_rows(t, d, _F32, tm), _out_rows(t, d, _F32, tm)], hbm=[w])


def _ffn_bwd_post(name, dxo, f, a, b, post_g, wd, scale, tm):
    t, d = dxo.shape
    ff = a.shape[1]

    def body(step, ti, ins, ws, outs, accs, scr):
        dxo_ref, f_ref, a_ref, b_ref, g_ref = ins
        df, dg = _rms_bwd(scale * dxo_ref[...], f_ref[...], g_ref[...])
        accs[0][...] += dg
        dfb = df.astype(_MM)
        ds = _dot_nt(dfb, ws[0][...])
        a32 = a_ref[...].astype(_F32)
        b32 = b_ref[...].astype(_F32)
        sg = _sigmoid(a32)
        outs[0][...] = (ds * b32 * (sg * (1.0 + a32 * (1.0 - sg)))).astype(_MM)
        outs[1][...] = (ds * (a32 * sg)).astype(_MM)
        outs[2][...] = dfb

    return _row_call(name, body, t // tm,
                     [_rows(dxo, tm), _rows(f, tm), _rows(a, tm), _rows(b, tm), _full(post_g)],
                     [_out_rows(t, ff, _MM, tm), _out_rows(t, ff, _MM, tm), _out_rows(t, d, _MM, tm)],
                     accs=[(1, d)], hbm=[wd])


def _bwd_in_norm(name, dzs, ws_list, x, dxo, pre_g, tm):
    t, d = x.shape
    nz = len(dzs)

    def body(step, ti, ins, ws, outs, accs, scr):
        dh = _dot_nt(ins[0][...], ws[0][...])
        for k in range(1, nz):
            dh = dh + _dot_nt(ins[k][...], ws[k][...])
        x_ref, dxo_ref, g_ref = ins[nz:]
        dx, dg = _rms_bwd(dh, x_ref[...], g_ref[...])
        accs[0][...] += dg
        outs[0][...] = dxo_ref[...] + dx

    return _row_call(name, body, t // tm,
                     [_rows(z, tm) for z in dzs] + [_rows(x, tm), _rows(dxo, tm), _full(pre_g)],
                     [_out_rows(t, d, _F32, tm)], accs=[(1, d)], hbm=list(ws_list))


def _mix_in(name, x, pre_g, w_parts, tm):
    t, d = x.shape

    def body(step, ti, ins, ws, outs, accs, scr):
        x_ref, g_ref = ins
        h = _rms_fwd(x_ref[...], g_ref[...]).astype(_MM)
        outs[0][...] = h
        for k in range(len(ws)):
            outs[1 + k][...] = _dot(h, ws[k][...]).astype(_MM)

    return _row_call(name, body, t // tm, [_rows(x, tm), _full(pre_g)],
                     [_out_rows(t, d, _MM, tm)] + [_out_rows(t, w.shape[1], _MM, tm) for w in w_parts],
                     hbm=list(w_parts))


def _lru_conv(xe, cw, cb, tm):
    xc = cb
    for k in range(4):
        xc = xc + _shift_rows(xe, k - 2, tm) * cw[k:k + 1, :]
    return xc


def _lru_gates(xc, wa_ref, wx_ref, ba, bx, c):
    dh = xc.shape[1] // _HEADS
    gas, gxs = [], []
    for hh in range(_HEADS):
        xs = xc[:, hh * dh:(hh + 1) * dh].astype(_MM)
        gas.append(_dot(xs, wa_ref[hh]))
        gxs.append(_dot(xs, wx_ref[hh]))
    r = _sigmoid(jnp.concatenate(gas, axis=1) + ba)
    i = _sigmoid(jnp.concatenate(gxs, axis=1) + bx)
    la = c * r
    a = jnp.exp(la)
    em = -jnp.tanh(la) * (a * a + 1.0)
    return r, i, a, em


def _scan_scratch(tm, w):
    return [pltpu.VMEM((tm, w), _F32), pltpu.VMEM((tm, w), _F32)]


def _tile_scan(a, u, scan_scr, h_dst, carry, tm, descending):
    a_scr, u_scr = scan_scr
    a_scr[...] = a
    u_scr[...] = u
    ng = tm // 8
    w = a.shape[1]
    row = lax.broadcasted_iota(jnp.int32, (8, w), 0)

    def grp(j, carry):
        g = (ng - 1 - j) if descending else j
        r0 = pl.multiple_of(g * 8, 8)
        a8 = a_scr[pl.ds(r0, 8), :]
        u8 = u_scr[pl.ds(r0, 8), :]
        for dd in (1, 2, 4):
            if descending:
                ok = row < 8 - dd
                sh = 8 - dd
            else:
                ok = row >= dd
                sh = dd
            a_s = jnp.where(ok, pltpu.roll(a8, sh, axis=0), 1.0)
            u_s = jnp.where(ok, pltpu.roll(u8, sh, axis=0), 0.0)
            u8 = a8 * u_s + u8
            a8 = a8 * a_s
        h8 = u8 + a8 * carry
        h_dst[pl.ds(r0, 8), :] = h8
        return h8[0:1, :] if descending else h8[7:8, :]

    return lax.fori_loop(0, ng, grp, carry, unroll=2)


def _lru_fwd(name, zx, cw, cb, wa, wx, ba, bx, lam, tm, descending, hf=None, zg=None):
    t, w = zx.shape
    n = t // tm

    def body(step, ti, ins, ws, outs, accs, scr):
        zc, zp, zn, cw_r, cb_r, wa_r, wx_r, ba_r, bx_r, lam_r = ins[:10]
        carry_scr = scr[2]
        xe = _with_halo(zc[...], zp[...], zn[...], ti == 0, ti == n - 1)
        xc = _lru_conv(xe, cw_r[...], cb_r[...], tm)
        c = _neg_softplus_neg(lam_r[...])
        r, i, a, em = _lru_gates(xc, wa_r, wx_r, ba_r[...], bx_r[...], c)

        @pl.when(step == 0)
        def _():
            carry_scr[...] = jnp.zeros(carry_scr.shape, _F32)

        carry_scr[...] = _tile_scan(a, i * xc * jnp.sqrt(em), scr[:2], outs[0], carry_scr[...], tm, descending)
        if descending:
            hf_r, zg_r = ins[10:]
            outs[1][...] = ((hf_r[...] + outs[0][...]) * _gelu(zg_r[...].astype(_F32))).astype(_MM)

    ins = [_rows(zx, tm), _halo_prev(zx, tm), _halo_next(zx, tm), _full(cw), _full(cb), _full(wa), _full(wx),
           _full(ba), _full(bx), _full(lam)]
    outs = [_out_rows(t, w, _F32, tm)]
    if descending:
        ins += [_rows(hf, tm), _rows(zg, tm)]
        outs += [_out_rows(t, w, _MM, tm)]
    scratch = _scan_scratch(tm, w) + [pltpu.VMEM((1, w), _F32)]
    return _row_call(name, body, n, ins, outs, scratch=scratch, reverse=descending)


def _lru_bwd(name, zx, d_in, h_own, cw, cb, wa, wx, ba, bx, lam, tm, direction, zg=None, h_other=None,
             dxc_in=None):
    t, w = zx.shape
    n = t // tm
    dh_ = w // _HEADS
    adj_desc = direction == 0

    def body(step, ti, ins, ws, outs, accs, scr):
        zc, zp, zn, din_r, ho_r, hh_r, cw_r, cb_r, wa_r, wx_r, ba_r, bx_r, lam_r = ins[:13]
        p_scr, carry_scr = scr[2:]
        first, last = ti == 0, ti == n - 1
        xe = _with_halo(zc[...], zp[...], zn[...], first, last)
        xc = _lru_conv(xe, cw_r[...], cb_r[...], tm)
        lam_v = lam_r[...]
        c = _neg_softplus_neg(lam_v)
        r, i, a, em = _lru_gates(xc, wa_r, wx_r, ba_r[...], bx_r[...], c)
        m = jnp.sqrt(em)
        if direction == 0:
            gel, gel_grad = _gelu_and_grad(ins[13][...].astype(_F32))
            dpa_v = din_r[...]
            d_h = dpa_v * gel
        else:
            d_h = din_r[...]
        @pl.when(step == 0)
        def _():
            carry_scr[...] = jnp.zeros(carry_scr.shape, _F32)

        carry_in = carry_scr[...]
        carry_scr[...] = _tile_scan(a, a * d_h, scr[:2], p_scr, carry_in, tm, adj_desc)
        p = p_scr[...]
        row = lax.broadcasted_iota(jnp.int32, (tm, w), 0)
        h_t = ho_r[...]
        if adj_desc:
            p_nb = jnp.where(row == tm - 1, carry_in, pltpu.roll(p, tm - 1, axis=0))
            edge = jnp.where(first, 0.0, hh_r[_HALO - 1:_HALO, :])
            h_nb = jnp.where(row == 0, edge, pltpu.roll(h_t, 1, axis=0))
        else:
            p_nb = jnp.where(row == 0, carry_in, pltpu.roll(p, 1, axis=0))
            edge = jnp.where(last, 0.0, hh_r[0:1, :])
            h_nb = jnp.where(row == tm - 1, edge, pltpu.roll(h_t, tm - 1, axis=0))
        g = d_h + p_nb
        gi = g * i
        d_i = g * xc * m
        dxc = gi * m
        d_m = gi * xc
        d_l = g * h_nb * a - d_m * (1.0 - em) / m
        accs[4][...] += jnp.sum(d_l * r, axis=0, keepdims=True)
        dga = d_l * c * r * (1.0 - r)
        dgx = d_i * i * (1.0 - i)
        accs[2][...] += jnp.sum(dga, axis=0, keepdims=True)
        accs[3][...] += jnp.sum(dgx, axis=0, keepdims=True)
        parts = []
        for hh in range(_HEADS):
            sl = slice(hh * dh_, (hh + 1) * dh_)
            xs = xc[:, sl].astype(_MM)
            da_h = dga[:, sl].astype(_MM)
            dx_h = dgx[:, sl].astype(_MM)
            accs[0][hh] += _dot_tn(xs, da_h)
            accs[1][hh] += _dot_tn(xs, dx_h)
            parts.append(_dot_nt(da_h, wa_r[hh]) + _dot_nt(dx_h, wx_r[hh]))
        dxc = dxc + jnp.concatenate(parts, axis=1)
        if direction == 0:
            outs[0][...] = dxc
            outs[1][...] = (dpa_v * (h_t + ins[14][...]) * gel_grad).astype(_MM)
            outs[2][...] = d_h
        else:
            outs[0][...] = dxc + ins[13][...]

        @pl.when(step == n - 1)
        def _():
            accs[4][...] = accs[4][...] * (_LRU_C * _sigmoid(-lam_v))

    halo_h = _halo_prev(h_own, tm) if adj_desc else _halo_next(h_own, tm)
    ins = [_rows(zx, tm), _halo_prev(zx, tm), _halo_next(zx, tm), _rows(d_in, tm), _rows(h_own, tm), halo_h,
           _full(cw), _full(cb), _full(wa), _full(wx), _full(ba), _full(bx), _full(lam)]
    outs = [_out_rows(t, w, _F32, tm)]
    if direction == 0:
        ins += [_rows(zg, tm), _rows(h_other, tm)]
        outs += [_out_rows(t, w, _MM, tm), _out_rows(t, w, _F32, tm)]
    else:
        ins += [_rows(dxc_in, tm)]
    accs = [(_HEADS, dh_, dh_), (_HEADS, dh_, dh_), (1, w), (1, w), (1, w)]
    scratch = _scan_scratch(tm, w) + [pltpu.VMEM((tm, w), _F32), pltpu.VMEM((1, w), _F32)]
    return _row_call(name, body, n, ins, outs, accs=accs, scratch=scratch, reverse=adj_desc)


def _lru_conv_bwd(name, dxc, zx, cw, tm):
    t, w = zx.shape
    n = t // tm

    def body(step, ti, ins, ws, outs, accs, scr):
        dc, dp, dn, zc, zp, zn, cw_r = ins
        first, last = ti == 0, ti == n - 1
        de = _with_halo(dc[...], dp[...], dn[...], first, last)
        ze = _with_halo(zc[...], zp[...], zn[...], first, last)
        cw_v = cw_r[...]
        d_cur = dc[...]
        dz = None
        for k in range(4):
            term = _shift_rows(de, 2 - k, tm) * cw_v[k:k + 1, :]
            dz = term if dz is None else dz + term
            accs[0][k:k + 1, :] += jnp.sum(d_cur * _shift_rows(ze, k - 2, tm), axis=0, keepdims=True)
        accs[1][...] += jnp.sum(d_cur, axis=0, keepdims=True)
        outs[0][...] = dz.astype(_MM)

    ins = [_rows(dxc, tm), _halo_prev(dxc, tm), _halo_next(dxc, tm),
           _rows(zx, tm), _halo_prev(zx, tm), _halo_next(zx, tm), _full(cw)]
    return _row_call(name, body, n, ins, [_out_rows(t, w, _MM, tm)], accs=[(4, w), (1, w)])


def _sgu_mix(v2, ws_ref, bias, mixed_scr, tm):
    gw = v2.shape[1]
    gh = gw // _HEADS
    for nn in range(tm // _CHUNK):
        rs = slice(nn * _CHUNK, (nn + 1) * _CHUNK)
        for g in range(_HEADS):
            cs = slice(g * gh, (g + 1) * gh)
            mixed_scr[rs, cs] = _dot(ws_ref[g], v2[rs, cs].astype(_MM)) + bias[:, cs]
    return mixed_scr[...]


def _ln_fwd(v1, lg, lb):
    mu = jnp.mean(v1, axis=-1, keepdims=True)
    vc = v1 - mu
    rs = lax.rsqrt(jnp.mean(vc * vc, axis=-1, keepdims=True) + _EPS)
    vn = vc * rs
    return vn * lg + lb, vn, rs


def _bc_fwd(name, zmid, scw, lg, lb, ws_mm, bias, sw, gw, tm):
    t = zmid.shape[0]
    n = t // tm

    def body(step, ti, ins, ws, outs, accs, scr):
        zc, zp, zn, scw_r, lg_r, lb_r, ws_r, bias_r = ins
        first, last = ti == 0, ti == n - 1
        z = zc[...].astype(_F32)
        zb, zcc, zxx = z[:, 0:sw], z[:, sw:2 * sw], z[:, 2 * sw:3 * sw]
        zu, zv = z[:, 3 * sw:3 * sw + gw], z[:, 3 * sw + gw:3 * sw + 2 * gw]
        zpv, znv = zp[...].astype(_F32), zn[...].astype(_F32)
        qe = _with_halo(zcc * zxx, zpv[:, sw:2 * sw] * zpv[:, 2 * sw:3 * sw],
                        znv[:, sw:2 * sw] * znv[:, 2 * sw:3 * sw], first, last)
        scw_v = scw_r[...]
        cq = None
        for k in range(3):
            term = _shift_rows(qe, k - 1, tm) * scw_v[k:k + 1, :]
            cq = term if cq is None else cq + term
        outs[0][...] = (zb * cq).astype(_MM)
        v2, _, _ = _ln_fwd(_gelu(zv), lg_r[...], lb_r[...])
        mixed = _sgu_mix(v2, ws_r, bias_r[...], scr[0], tm)
        outs[1][...] = (_gelu(zu) * mixed).astype(_MM)

    ins = [_rows(zmid, tm), _halo_prev(zmid, tm), _halo_next(zmid, tm), _full(scw), _full(lg), _full(lb),
           _full(ws_mm), _full(bias)]
    return _row_call(name, body, n, ins, [_out_rows(t, sw, _MM, tm), _out_rows(t, gw, _MM, tm)],
                     scratch=[pltpu.VMEM((tm, gw), _F32)])


def _bc_bwd(name, zmid, dpb, dpc, scw, lg, lb, ws_mm, wst_mm, bias, sw, gw, tm):
    t = zmid.shape[0]
    n = t // tm
    gh = gw // _HEADS

    def body(step, ti, ins, ws, outs, accs, scr):
        zc, zp, zn, db_c, db_p, db_n, dc_r, scw_r, lg_r, lb_r, ws_r, wst_r, bias_r = ins
        mixed_scr, dv2_scr = scr
        first, last = ti == 0, ti == n - 1
        z = zc[...].astype(_F32)
        zb, zcc, zxx = z[:, 0:sw], z[:, sw:2 * sw], z[:, 2 * sw:3 * sw]
        zu, zv = z[:, 3 * sw:3 * sw + gw], z[:, 3 * sw + gw:3 * sw + 2 * gw]
        zpv, znv = zp[...].astype(_F32), zn[...].astype(_F32)
        qe = _with_halo(zcc * zxx, zpv[:, sw:2 * sw] * zpv[:, 2 * sw:3 * sw],
                        znv[:, sw:2 * sw] * znv[:, 2 * sw:3 * sw], first, last)
        dpb_v = db_c[...]
        dcq = dpb_v * zb
        dcqe = _with_halo(dcq, db_p[...] * zpv[:, 0:sw], db_n[...] * znv[:, 0:sw], first, last)
        scw_v = scw_r[...]
        cq, dq = None, None
        for k in range(3):
            qk = _shift_rows(qe, k - 1, tm)
            term = qk * scw_v[k:k + 1, :]
            cq = term if cq is None else cq + term
            dterm = _shift_rows(dcqe, 1 - k, tm) * scw_v[k:k + 1, :]
            dq = dterm if dq is None else dq + dterm
            accs[0][k:k + 1, :] += jnp.sum(dcq * qk, axis=0, keepdims=True)
        outs[0][:, 0:sw] = (dpb_v * cq).astype(_MM)
        outs[0][:, sw:2 * sw] = (dq * zxx).astype(_MM)
        outs[0][:, 2 * sw:3 * sw] = (dq * zcc).astype(_MM)
        lg_v = lg_r[...]
        v2, vn, rs = _ln_fwd(_gelu(zv), lg_v, lb_r[...])
        mixed = _sgu_mix(v2, ws_r, bias_r[...], mixed_scr, tm)
        dpc_v = dc_r[...]
        outs[0][:, 3 * sw:3 * sw + gw] = (dpc_v * mixed * _gelu_grad(zu)).astype(_MM)
        dmix = dpc_v * _gelu(zu)
        for nn in range(tm // _CHUNK):
            rsl = slice(nn * _CHUNK, (nn + 1) * _CHUNK)
            accs[4][...] += dmix[rsl, :]
            for g in range(_HEADS):
                cs = slice(g * gh, (g + 1) * gh)
                dm_b = dmix[rsl, cs].astype(_MM)
                accs[3][g] += _dot_nt(dm_b, v2[rsl, cs].astype(_MM))
                dv2_scr[rsl, cs] = _dot(wst_r[g], dm_b)
        dv2 = dv2_scr[...]
        accs[1][...] += jnp.sum(dv2 * vn, axis=0, keepdims=True)
        accs[2][...] += jnp.sum(dv2, axis=0, keepdims=True)
        dvn = dv2 * lg_v
        dv1 = rs * (dvn - jnp.mean(dvn, axis=-1, keepdims=True)
                    - vn * jnp.mean(dvn * vn, axis=-1, keepdims=True))
        outs[0][:, 3 * sw + gw:3 * sw + 2 * gw] = (dv1 * _gelu_grad(zv)).astype(_MM)

    ins = [_rows(zmid, tm), _halo_prev(zmid, tm), _halo_next(zmid, tm),
           _rows(dpb, tm), _halo_prev(dpb, tm), _halo_next(dpb, tm), _rows(dpc, tm),
           _full(scw), _full(lg), _full(lb), _full(ws_mm), _full(wst_mm), _full(bias)]
    accs = [(3, sw), (1, gw), (1, gw), (_HEADS, _CHUNK, _CHUNK), (_CHUNK, gw)]
    return _row_call(name, body, n, ins, [_out_rows(t, 3 * sw + 2 * gw, _MM, tm)], accs=accs,
                     scratch=[pltpu.VMEM((tm, gw), _F32), pltpu.VMEM((tm, gw), _F32)])


def _mix_proj(name, pa, pb, pc, zm, wlo, wsc, wsg, tm):
    t = pa.shape[0]
    d = wlo.shape[1]

    def body(step, ti, ins, ws, outs, accs, scr):
        ys = [_dot(ins[k][...], ws[k][...]) for k in range(3)]
        gm = _sigmoid(ins[3][...].astype(_F32))
        m = None
        for k in range(3):
            outs[k][...] = ys[k].astype(_MM)
            term = gm[:, k * d:(k + 1) * d] * ys[k]
            m = term if m is None else m + term
        outs[3][...] = m.astype(_MM)

    return _row_call(name, body, t // tm, [_rows(pa, tm), _rows(pb, tm), _rows(pc, tm), _rows(zm, tm)],
                     [_out_rows(t, d, _MM, tm)] * 4, hbm=[wlo, wsc, wsg])


def _mix_bwd_out(name, dxo, mo, ya, yb, yc, zm, post_g, wo, tm):
    t, d = dxo.shape

    def body(step, ti, ins, ws, outs, accs, scr):
        dxo_ref, mo_ref, ya_r, yb_r, yc_r, zm_r, g_ref = ins
        dmo, dg = _rms_bwd(dxo_ref[...], mo_ref[...], g_ref[...])
        accs[0][...] += dg
        dmob = dmo.astype(_MM)
        outs[0][...] = dmob
        dm = _dot_nt(dmob, ws[0][...])
        gm = _sigmoid(zm_r[...].astype(_F32))
        for k, y_r in enumerate((ya_r, yb_r, yc_r)):
            gk = gm[:, k * d:(k + 1) * d]
            outs[1 + k][...] = (dm * gk).astype(_MM)
            outs[4][:, k * d:(k + 1) * d] = (dm * y_r[...].astype(_F32) * gk * (1.0 - gk)).astype(_MM)

    ins = [_rows(dxo, tm), _rows(mo, tm), _rows(ya, tm), _rows(yb, tm), _rows(yc, tm), _rows(zm, tm),
           _full(post_g)]
    return _row_call(name, body, t // tm, ins,
                     [_out_rows(t, d, _MM, tm)] * 4 + [_out_rows(t, 3 * d, _MM, tm)], accs=[(1, d)], hbm=[wo])


def _mix_bwd_proj(name, dya, dyb, dyc, wlo, wsc, wsg, tm):
    t = dya.shape[0]

    def body(step, ti, ins, ws, outs, accs, scr):
        for k in range(3):
            outs[k][...] = _dot_nt(ins[k][...], ws[k][...])

    return _row_call(name, body, t // tm, [_rows(dya, tm), _rows(dyb, tm), _rows(dyc, tm)],
                     [_out_rows(t, w.shape[0], _F32, tm) for w in (wlo, wsc, wsg)], hbm=[wlo, wsc, wsg])


def _loss_grad(name, y, target, tm):
    t, d = y.shape

    def body(step, ti, ins, ws, outs, accs, scr):
        err = ins[0][...] - ins[1][...]
        outs[0][...] = err * (1.0 / d)
        accs[0][...] += (0.5 / d) * jnp.sum(err * err)

    dy, acc = _row_call(name, body, t // tm, [_rows(y, tm), _rows(target, tm)], [_out_rows(t, d, _F32, tm)],
                        accs=[(1, 128)])
    return acc[0, 0], dy


def _adamw(w, g, m, v):
    m = _ADAM_B1 * m + (1.0 - _ADAM_B1) * g
    v = _ADAM_B2 * v + (1.0 - _ADAM_B2) * (g * g)
    m_hat = m / (1.0 - _ADAM_B1 ** _ADAM_STEP)
    v_hat = v / (1.0 - _ADAM_B2 ** _ADAM_STEP)
    delta = -_ADAM_LR * (m_hat / (jnp.sqrt(v_hat) + _ADAM_EPS) + _ADAM_WD * w)
    return delta, m, v


def _flat_tm(rows):
    return 512 if rows % 512 == 0 else rows


def _sum_adamw(name, parts, w, m, v):
    rows, c = w.shape
    tm = _flat_tm(rows)
    np_ = len(parts)

    def body(step, ti, ins, ws, outs, accs, scr):
        g = ins[0][...].astype(_F32)
        for k in range(1, np_):
            g = g + ins[k][...].astype(_F32)
        delta, nm, nv = _adamw(ins[np_][...], g, ins[np_ + 1][...], ins[np_ + 2][...])
        outs[0][...] = g
        outs[1][...] = delta
        outs[2][...] = nm
        outs[3][...] = nv

    return _row_call(name, body, rows // tm, list(parts) + [_rows(w, tm), _rows(m, tm), _rows(v, tm)],
                     [_out_rows(rows, c, _F32, tm)] * 4)


def _sum_parts(name, parts, rows, c, out_dtype):
    tm = _flat_tm(rows)

    def body(step, ti, ins, ws, outs, accs, scr):
        g = ins[0][...].astype(_F32)
        for k in range(1, len(ins)):
            g = g + ins[k][...].astype(_F32)
        outs[0][...] = g.astype(out_dtype)

    return _row_call(name, body, rows // tm, list(parts), [_out_rows(rows, c, out_dtype, tm)])[0]


def _pack(arrs, dtype):
    flat = jnp.concatenate([a.reshape(-1).astype(dtype) for a in arrs])
    return _to_rows(flat, _flat_rows(flat.shape[0]))


def _pack_rows(arrs, dtype):
    parts = [a.reshape(-1, _LANES).astype(dtype) for a in arrs]
    n = sum(p.shape[0] for p in parts)
    rows = _flat_rows(n * _LANES)
    if rows > n:
        parts.append(jnp.zeros((rows - n, _LANES), dtype))
    return jnp.concatenate(parts, axis=0)


def _unpack_rows(flat, shapes):
    out, off = [], 0
    for s in shapes:
        n = math.prod(s) // _LANES
        out.append(flat[off:off + n].reshape(tuple(s)))
        off += n
    return out


def _unpack(flat, shapes):
    v = flat.reshape(-1)
    out, off = [], 0
    for s in shapes:
        n = math.prod(s)
        out.append(v[off:off + n].reshape(tuple(s)))
        off += n
    return out


def _place():
    return lax.axis_index("x"), lax.axis_index("y"), lax.axis_index("c")


def _all_gather(name, block):
    r, c_ = block.shape

    def body(x_ref, out_ref, send_sems, recv_sems, local_sem):
        x, y, c = _place()
        me, sibling = (x, y, c), (x, y, 1 - c)
        chips = [(1 - x, y), (x, 1 - y), (1 - x, 1 - y)]

        def rows(px, py, pc):
            return out_ref.at[4 * px + 2 * py + pc]

        def copy(k, blk, to, src=None):
            return pltpu.make_async_remote_copy(
                src_ref=rows(*blk) if src is None else src, dst_ref=rows(*blk),
                send_sem=send_sems.at[k], recv_sem=recv_sems.at[k], device_id=to, device_id_type=_MESH)

        mine = pltpu.make_async_copy(x_ref, rows(*me), local_sem)
        mine.start()
        first = [copy(0, me, sibling, src=x_ref)]
        first += [copy(1 + j, me, (*chip, c), src=x_ref) for j, chip in enumerate(chips)]
        for cp in first:
            cp.start()
        passed = [copy(4 + j, (*chip, c), sibling) for j, chip in enumerate(chips)]
        for j, chip in enumerate(chips):
            copy(1 + j, (*chip, c), me).wait_recv()
            passed[j].start()
        copy(0, sibling, me).wait_recv()
        for j, chip in enumerate(chips):
            copy(4 + j, (*chip, 1 - c), me).wait_recv()
        for cp in first + passed:
            cp.wait_send()
        mine.wait()

    return pl.pallas_call(
        body, name=name, out_shape=jax.ShapeDtypeStruct((_NDEV, r, c_), block.dtype),
        in_specs=[pl.BlockSpec(memory_space=pl.ANY)], out_specs=pl.BlockSpec(memory_space=pl.ANY),
        scratch_shapes=[pltpu.SemaphoreType.DMA((7,)), pltpu.SemaphoreType.DMA((7,)), pltpu.SemaphoreType.DMA(())],
    )(block)


def _sibling_exchange(name, dm):
    _, _, r, c_ = dm.shape

    def body(d_ref, r_ref, send_sems, recv_sems):
        x, y, c = _place()
        cps = []
        for k in range(4):
            cp = pltpu.make_async_remote_copy(
                src_ref=d_ref.at[k, 1 - c], dst_ref=r_ref.at[k], send_sem=send_sems.at[k],
                recv_sem=recv_sems.at[k], device_id=(x, y, 1 - c), device_id_type=_MESH)
            cp.start()
            cps.append(cp)
        for cp in cps:
            cp.wait()

    return pl.pallas_call(
        body, name=name, out_shape=jax.ShapeDtypeStruct((4, r, c_), dm.dtype),
        in_specs=[pl.BlockSpec(memory_space=pl.ANY)], out_specs=pl.BlockSpec(memory_space=pl.ANY),
        scratch_shapes=[pltpu.SemaphoreType.DMA((4,)), pltpu.SemaphoreType.DMA((4,))],
    )(dm)


def _pair_sum(name, dm, got, core):
    _, _, r, c_ = dm.shape
    tm = _flat_tm(r)

    def kern(core_ref, a_ref, b_ref, o_ref):
        o_ref[...] = (a_ref[...] + b_ref[...]).astype(o_ref.dtype)

    grid_spec = pltpu.PrefetchScalarGridSpec(
        num_scalar_prefetch=1, grid=(4, r // tm),
        in_specs=[pl.BlockSpec((None, None, tm, c_), lambda k, i, cr: (k, cr[0], i, 0)),
                  pl.BlockSpec((None, tm, c_), lambda k, i, cr: (k, i, 0))],
        out_specs=pl.BlockSpec((None, tm, c_), lambda k, i, cr: (k, i, 0)))
    return pl.pallas_call(
        kern, name=name, grid_spec=grid_spec, out_shape=jax.ShapeDtypeStruct((4, r, c_), _MM),
        compiler_params=pltpu.CompilerParams(dimension_semantics=("arbitrary", "arbitrary"),
                                             vmem_limit_bytes=_VMEM_LIMIT),
    )(core, dm, got)


def _chip_sum(name, p, others, chip):
    _, r, c_ = p.shape
    tm = _flat_tm(r)

    def kern(chip_ref, p_ref, o0, o1, o2, g_out):
        g_out[...] = (p_ref[...].astype(_F32) + o0[...].astype(_F32) + o1[...].astype(_F32)
                      + o2[...].astype(_F32))

    grid_spec = pltpu.PrefetchScalarGridSpec(
        num_scalar_prefetch=1, grid=(r // tm,),
        in_specs=[pl.BlockSpec((None, tm, c_), lambda i, cr: (cr[0], i, 0))]
        + [pl.BlockSpec((None, tm, c_), lambda i, cr, k=k: (k, i, 0)) for k in range(3)],
        out_specs=pl.BlockSpec((tm, c_), lambda i, cr: (i, 0)))
    return pl.pallas_call(
        kern, name=name, grid_spec=grid_spec, out_shape=jax.ShapeDtypeStruct((r, c_), _F32),
        compiler_params=pltpu.CompilerParams(dimension_semantics=("arbitrary",), vmem_limit_bytes=_VMEM_LIMIT),
    )(chip, p, others, others, others)


def _row_tile(rows, lanes):
    for d in range(min(rows, max(8, (1 << 18) // lanes)) // 8 * 8, 7, -8):
        if rows % d == 0:
            return d
    return rows


def _adamw_update(name, w, g, m, v):
    shape = w.shape
    as2d = lambda a: a.reshape(-1, shape[-1])
    rows = math.prod(shape[:-1])
    tm = _row_tile(rows, shape[-1])

    def body(step, ti, ins, ws, outs, accs, scr):
        delta, nm, nv = _adamw(ins[0][...], ins[1][...], ins[2][...], ins[3][...])
        outs[0][...] = delta
        outs[1][...] = nm
        outs[2][...] = nv

    res = _row_call(name, body, rows // tm, [_rows(as2d(a), tm) for a in (w, g, m, v)],
                    [_out_rows(rows, shape[-1], _F32, tm)] * 3)
    return [r.reshape(shape) for r in res]


def _chip_exchange(name, p):
    _, r, c_ = p.shape

    def body(p_ref, r_ref, send_sems, recv_sems):
        x, y, c = _place()
        chips = [(1 - x, y), (x, 1 - y), (1 - x, 1 - y)]
        cps = []
        for j, (px, py) in enumerate(chips):
            cp = pltpu.make_async_remote_copy(
                src_ref=p_ref.at[2 * px + py], dst_ref=r_ref.at[j], send_sem=send_sems.at[j],
                recv_sem=recv_sems.at[j], device_id=(px, py, c), device_id_type=_MESH)
            cp.start()
            cps.append(cp)
        for cp in cps:
            cp.wait()

    return pl.pallas_call(
        body, name=name, out_shape=jax.ShapeDtypeStruct((3, r, c_), p.dtype),
        in_specs=[pl.BlockSpec(memory_space=pl.ANY)], out_specs=pl.BlockSpec(memory_space=pl.ANY),
        scratch_shapes=[pltpu.SemaphoreType.DMA((3,)), pltpu.SemaphoreType.DMA((3,))],
    )(p)


def _gather_full(gathered, names, shard_shapes, axes, unpack):
    per_dev = [unpack(gathered[d], shard_shapes) for d in range(_NDEV)]
    return {nme: jnp.concatenate([per_dev[d][i] for d in range(_NDEV)], axis=ax)
            for i, (nme, ax) in enumerate(zip(names, axes))}


def _block_rows(a, ax, d):
    s = a.shape[ax] // _NDEV
    return lax.slice_in_dim(a, d * s, (d + 1) * s, axis=ax).reshape(-1, _LANES)


def _to_rows(flat, rows):
    return jnp.pad(flat, (0, rows * _LANES - flat.shape[0])).reshape(rows, _LANES)


def _flat_rows(n):
    rows = -(-n // _LANES)
    return -(-rows // 512) * 512 if rows >= 512 else -(-rows // 16) * 16


def _tm(t, want):
    return min(t, want)


def _ffn_forward(tag, x, pre_g, wg, wu, wd, post_g):
    t = x.shape[0]
    h, a, b, s = _ffn_up(tag + "_up", x, pre_g, wg, wu, _tm(t, 256))
    f, x_out = _proj_norm_res(tag + "_down", s, x, post_g, wd, 0.5, _tm(t, 512))
    return x_out, dict(x=x, h=h, a=a, b=b, s=s, f=f)


def _ffn_backward(tag, dxo, sv, pre_g, wg, wu, wd, post_g):
    t = dxo.shape[0]
    da, db, df, dpost = _ffn_bwd_post(tag + "_bwd_post", dxo, sv['f'], sv['a'], sv['b'], post_g, wd, 0.5,
                                      _tm(t, 256))
    dx, dpre = _bwd_in_norm(tag + "_bwd_pre", [da, db], [wg, wu], sv['x'], dxo, pre_g, _tm(t, 512))
    tk = _tm(t, 512)
    grads = dict(pre_g=dpre[0], post_g=dpost[0],
                 w_gate=_xty(tag + "_dwg", sv['h'], da, tk), w_up=_xty(tag + "_dwu", sv['h'], db, tk),
                 w_down=_xty(tag + "_dwd", sv['s'], df, tk))
    return dx, grads


def _mixer_weights(w, l):
    lw = w['lru_conv_w'].shape[-1]
    sw = w['sc_conv_w'].shape[-1]
    gw = w['sgu_ln_g'].shape[-1]
    win = w['w_in'][l]
    cuts = [0, lw, 2 * lw, 2 * lw + 3 * sw + 2 * gw, win.shape[1]]
    p = dict(lw=lw, sw=sw, gw=gw,
             win=[win[:, cuts[k]:cuts[k + 1]] for k in range(4)],
             cw=w['lru_conv_w'][l], cb=w['lru_conv_b'][l][None, :],
             wa=w['lru_wa'][l], wx=w['lru_wx'][l],
             ba=w['lru_ba'][l], bx=w['lru_bx'][l], lam=w['lru_lambda'][l],
             wlo=w['lru_w_out'][l], scw=w['sc_conv_w'][l], wsc=w['sc_w_out'][l],
             lg=w['sgu_ln_g'][l][None, :], lb=w['sgu_ln_b'][l][None, :],
             ws=w['sgu_w_s'][l].astype(_MM), wst=jnp.swapaxes(w['sgu_w_s'][l], 1, 2).astype(_MM),
             bias=jnp.repeat(w['sgu_b'][l].T, gw // _HEADS, axis=1),
             wsg=w['sgu_w_out'][l], wo=w['w_o'][l],
             pre_g=w['mix_pre_g'][l][None, :], post_g=w['mix_post_g'][l][None, :])
    return p


def _mixer_forward(tag, x, p):
    t = x.shape[0]
    tl = _tm(t, 256)
    hm, zg, zx, zmid, zm = _mix_in(tag + "_in", x, p['pre_g'], p['win'], _tm(t, 512))
    lru = lambda d: (p['cw'], p['cb'], p['wa'][d], p['wx'][d], p['ba'][d:d + 1], p['bx'][d:d + 1],
                     p['lam'][d:d + 1])
    hf, = _lru_fwd(tag + "_lru_f", zx, *lru(0), tl, False)
    hb, pa = _lru_fwd(tag + "_lru_b", zx, *lru(1), tl, True, hf=hf, zg=zg)
    pb, pc = _bc_fwd(tag + "_bc", zmid, p['scw'], p['lg'], p['lb'], p['ws'], p['bias'], p['sw'], p['gw'], tl)
    ya, yb, yc, m = _mix_proj(tag + "_proj", pa, pb, pc, zm, p['wlo'], p['wsc'], p['wsg'], _tm(t, 512))
    mo, x_out = _proj_norm_res(tag + "_out", m, x, p['post_g'], p['wo'], 1.0, _tm(t, 512))
    sv = dict(x=x, hm=hm, zg=zg, zx=zx, zmid=zmid, zm=zm, hf=hf, hb=hb, pa=pa, pb=pb, pc=pc,
              ya=ya, yb=yb, yc=yc, m=m, mo=mo)
    return x_out, sv


def _mixer_backward(tag, dxo, sv, p):
    t = dxo.shape[0]
    tl = _tm(t, 256)
    tk = _tm(t, 512)
    dmo, dya, dyb, dyc, dzm, dpost = _mix_bwd_out(tag + "_bwd_out", dxo, sv['mo'], sv['ya'], sv['yb'], sv['yc'],
                                                  sv['zm'], p['post_g'], p['wo'], _tm(t, 512))
    dpa, dpb, dpc = _mix_bwd_proj(tag + "_bwd_proj", dya, dyb, dyc, p['wlo'], p['wsc'], p['wsg'], _tm(t, 512))
    dzmid, dscw, dlg, dlb, dws, dbias = _bc_bwd(tag + "_bc_bwd", sv['zmid'], dpb, dpc, p['scw'], p['lg'], p['lb'],
                                                p['ws'], p['wst'], p['bias'], p['sw'], p['gw'], tl)
    lru = lambda d: (p['cw'], p['cb'], p['wa'][d], p['wx'][d], p['ba'][d:d + 1], p['bx'][d:d + 1],
                     p['lam'][d:d + 1])
    dxc0, dzg, d_h, dwa0, dwx0, dba0, dbx0, dlam0 = _lru_bwd(tag + "_lru_bwd_f", sv['zx'], dpa, sv['hf'], *lru(0),
                                                              tl, 0, zg=sv['zg'], h_other=sv['hb'])
    dxc, dwa1, dwx1, dba1, dbx1, dlam1 = _lru_bwd(tag + "_lru_bwd_b", sv['zx'], d_h, sv['hb'], *lru(1), tl, 1,
                                                  dxc_in=dxc0)
    dzx, dcw, dcb = _lru_conv_bwd(tag + "_conv_bwd", dxc, sv['zx'], p['cw'], tl)
    dzs = [dzg, dzx, dzmid, dzm]
    dx, dpre = _bwd_in_norm(tag + "_bwd_in", dzs, p['win'], sv['x'], dxo, p['pre_g'], _tm(t, 256))
    gh = p['gw'] // _HEADS
    grads = dict(
        mix_pre_g=dpre[0], mix_post_g=dpost[0],
        w_in=jnp.concatenate([_xty(tag + "_dwin%d" % k, sv['hm'], dz, tk) for k, dz in enumerate(dzs)], axis=1),
        lru_conv_w=dcw, lru_conv_b=dcb[0],
        lru_wa=jnp.stack([dwa0, dwa1]), lru_wx=jnp.stack([dwx0, dwx1]),
        lru_ba=jnp.concatenate([dba0, dba1]), lru_bx=jnp.concatenate([dbx0, dbx1]),
        lru_lambda=jnp.concatenate([dlam0, dlam1]),
        lru_w_out=_xty(tag + "_dwlo", sv['pa'], dya, tk),
        sc_conv_w=dscw, sc_w_out=_xty(tag + "_dwsc", sv['pb'], dyb, tk),
        sgu_ln_g=dlg[0], sgu_ln_b=dlb[0], sgu_w_s=dws,
        sgu_b=jnp.sum(dbias.reshape(_CHUNK, _HEADS, gh), axis=2).T,
        sgu_w_out=_xty(tag + "_dwsg", sv['pc'], dyc, tk),
        w_o=_xty(tag + "_dwo", sv['m'], dmo, tk))
    return dx, grads


def _local_step(x, target, w):
    depth = w['w_in'].shape[0]
    saved = []
    for l in range(depth):
        g = lambda nme: w[nme][l][None, :]
        x, s1 = _ffn_forward("l%d_ffn1" % l, x, g('ffn1_pre_g'), w['ffn1_w_gate'][l], w['ffn1_w_up'][l],
                             w['ffn1_w_down'][l], g('ffn1_post_g'))
        p = _mixer_weights(w, l)
        x, sm = _mixer_forward("l%d_mix" % l, x, p)
        x, s2 = _ffn_forward("l%d_ffn2" % l, x, g('ffn2_pre_g'), w['ffn2_w_gate'][l], w['ffn2_w_up'][l],
                             w['ffn2_w_down'][l], g('ffn2_post_g'))
        saved.append((s1, sm, s2, p))
    loss, dx = _loss_grad("loss", x, target, _tm(x.shape[0], 512))
    per_layer = []
    for l in reversed(range(depth)):
        s1, sm, s2, p = saved[l]
        g = lambda nme: w[nme][l][None, :]
        grads = {}
        dx, g2 = _ffn_backward("l%d_ffn2" % l, dx, s2, g('ffn2_pre_g'), w['ffn2_w_gate'][l], w['ffn2_w_up'][l],
                               w['ffn2_w_down'][l], g('ffn2_post_g'))
        grads.update({'ffn2_' + k: v for k, v in g2.items()})
        dx, gm = _mixer_backward("l%d_mix" % l, dx, sm, p)
        grads.update(gm)
        dx, g1 = _ffn_backward("l%d_ffn1" % l, dx, s1, g('ffn1_pre_g'), w['ffn1_w_gate'][l], w['ffn1_w_up'][l],
                               w['ffn1_w_down'][l], g('ffn1_post_g'))
        grads.update({'ffn1_' + k: v for k, v in g1.items()})
        per_layer.append(grads)
    per_layer.reverse()
    return loss, dx, per_layer


def kernel(x, ffn1_pre_g, ffn1_w_gate, ffn1_w_up, ffn1_w_down, ffn1_post_g, mix_pre_g, w_in, lru_conv_w, lru_conv_b, lru_wa, lru_ba, lru_wx, lru_bx, lru_lambda, lru_w_out, sc_conv_w, sc_w_out, sgu_ln_g, sgu_ln_b, sgu_w_s, sgu_b, sgu_w_out, w_o, mix_post_g, ffn2_pre_g, ffn2_w_gate, ffn2_w_up, ffn2_w_down, ffn2_post_g, loss_target, m_ffn1_pre_g, m_ffn1_w_gate, m_ffn1_w_up, m_ffn1_w_down, m_ffn1_post_g, m_mix_pre_g, m_w_in, m_lru_conv_w, m_lru_conv_b, m_lru_wa, m_lru_ba, m_lru_wx, m_lru_bx, m_lru_lambda, m_lru_w_out, m_sc_conv_w, m_sc_w_out, m_sgu_ln_g, m_sgu_ln_b, m_sgu_w_s, m_sgu_b, m_sgu_w_out, m_w_o, m_mix_post_g, m_ffn2_pre_g, m_ffn2_w_gate, m_ffn2_w_up, m_ffn2_w_down, m_ffn2_post_g, v_ffn1_pre_g, v_ffn1_w_gate, v_ffn1_w_up, v_ffn1_w_down, v_ffn1_post_g, v_mix_pre_g, v_w_in, v_lru_conv_w, v_lru_conv_b, v_lru_wa, v_lru_ba, v_lru_wx, v_lru_bx, v_lru_lambda, v_lru_w_out, v_sc_conv_w, v_sc_w_out, v_sgu_ln_g, v_sgu_ln_b, v_sgu_w_s, v_sgu_b, v_sgu_w_out, v_w_o, v_mix_post_g, v_ffn2_pre_g, v_ffn2_w_gate, v_ffn2_w_up, v_ffn2_w_down, v_ffn2_post_g):
    args = locals()
    wts = {n: args[n] for n in _WEIGHTS}
    mom = {n: args['m_' + n] for n in _WEIGHTS}
    var = {n: args['v_' + n] for n in _WEIGHTS}
    cx, cy, cc = _place()
    dev = 4 * cx + 2 * cy + cc
    big, small = list(_BIG), list(_SMALL_SHARDED)

    g_big = _all_gather("gather_matrices", _pack_rows([wts[n] for n in big], _MM))
    g_small = _all_gather("gather_vectors", _pack([wts[n] for n in small], _F32))
    full = dict(wts)
    full.update(_gather_full(g_big, big, [wts[n].shape for n in big], [_BIG[n] for n in big], _unpack_rows))
    full.update(_gather_full(g_small, small, [wts[n].shape for n in small], [_SMALL_SHARDED[n] for n in small],
                             _unpack))

    loss, grad_x, grads = _local_step(x[0], loss_target[0], full)
    loss = lax.psum(loss, ("x", "y", "c"))

    depth = len(grads)
    rows = _flat_rows(sum(math.prod(wts[n].shape) for n in big))
    pieces = []
    for d in range(_NDEV):
        blocks = [_block_rows(grads[l][n], _BIG[n] - 1, d) for n in big for l in range(depth)]
        fill = rows - sum(b.shape[0] for b in blocks)
        pieces += blocks + ([jnp.zeros((fill, _LANES), _F32)] if fill else [])
    dm = jnp.concatenate(pieces, axis=0).reshape(4, 2, rows, _LANES)
    got = _sibling_exchange("reduce_sibling", dm)
    p = _pair_sum("reduce_pair_sum", dm, got, jnp.reshape(cc, (1,)).astype(jnp.int32))
    others = _chip_exchange("reduce_chips", p)
    g_b = _chip_sum("reduce_final_sum", p, others, jnp.reshape(2 * cx + cy, (1,)).astype(jnp.int32))
    out = {}
    for n, g in zip(big, _unpack_rows(g_b, [wts[n].shape for n in big])):
        out['grad_' + n] = g
        out['delta_' + n], out['new_m_' + n], out['new_v_' + n] = _adamw_update("update_" + n, wts[n], g, mom[n],
                                                                                 var[n])

    vec = _REPLICATED + small
    gvec = {n: jnp.stack([g[n] for g in grads]) for n in vec}
    part = _pack([gvec[n] for n in vec], _F32)
    allp = _all_gather("gather_vector_grads", part)
    rv = part.shape[0]
    tmv = _flat_tm(rv)
    gsum = _sum_parts("reduce_vector_grads", [_rows3(allp, k, tmv) for k in range(_NDEV)], rv, _LANES, _F32)
    gfull = dict(zip(vec, _unpack(gsum, [gvec[n].shape for n in vec])))
    gloc = []
    for n in vec:
        if n in _SMALL_SHARDED:
            ax = _SMALL_SHARDED[n]
            sz = wts[n].shape[ax]
            gloc.append(lax.dynamic_slice_in_dim(gfull[n], dev * sz, sz, axis=ax))
        else:
            gloc.append(gfull[n])
    gl = _pack(gloc, _F32)
    g_s, d_s, m_s, v_s = _sum_adamw("update_vectors", [_rows(gl, _flat_tm(gl.shape[0]))],
                                    _pack([wts[n] for n in vec], _F32), _pack([mom[n] for n in vec], _F32),
                                    _pack([var[n] for n in vec], _F32))
    shapes_s = [wts[n].shape for n in vec]
    for key, flat in (('grad_', g_s), ('delta_', d_s), ('new_m_', m_s), ('new_v_', v_s)):
        for n, a in zip(vec, _unpack(flat, shapes_s)):
            out[key + n] = a

    res = [loss, grad_x[None]]
    for key in ('grad_', 'delta_', 'new_m_', 'new_v_'):
        res += [out[key + n] for n in _WEIGHTS]
    return tuple(res)
```

```python
import functools
import math

import jax
import jax.numpy as jnp
from jax import lax
from jax.experimental import pallas as pl
from jax.experimental.pallas import tpu as pltpu

_F32 = jnp.float32
_MM = jnp.bfloat16
_EPS = 1e-6
_HEADS = 4
_CHUNK = 128
_LRU_C = 8.0
_HALO = 16
_LANES = 1024
_NDEV = 8
_VMEM_LIMIT = 56 * 1024 * 1024
_GELU_K = math.sqrt(2.0 / math.pi)
_GELU_C = 0.044715
_MESH = pl.DeviceIdType.MESH

_ADAM_LR, _ADAM_B1, _ADAM_B2, _ADAM_EPS, _ADAM_WD, _ADAM_STEP = 1e-3, 0.9, 0.999, 1e-8, 0.01, 10

_WEIGHTS = ['ffn1_pre_g', 'ffn1_w_gate', 'ffn1_w_up', 'ffn1_w_down', 'ffn1_post_g', 'mix_pre_g', 'w_in',
            'lru_conv_w', 'lru_conv_b', 'lru_wa', 'lru_ba', 'lru_wx', 'lru_bx', 'lru_lambda', 'lru_w_out',
            'sc_conv_w', 'sc_w_out', 'sgu_ln_g', 'sgu_ln_b', 'sgu_w_s', 'sgu_b', 'sgu_w_out', 'w_o',
            'mix_post_g', 'ffn2_pre_g', 'ffn2_w_gate', 'ffn2_w_up', 'ffn2_w_down', 'ffn2_post_g']
_BIG = {'ffn1_w_gate': 2, 'ffn1_w_up': 2, 'ffn1_w_down': 1, 'w_in': 2, 'lru_wa': 3, 'lru_wx': 3,
        'lru_w_out': 1, 'sc_w_out': 2, 'sgu_w_out': 2, 'w_o': 1,
        'ffn2_w_gate': 2, 'ffn2_w_up': 2, 'ffn2_w_down': 1}
_SMALL_SHARDED = {'lru_conv_w': 2, 'lru_ba': 2, 'lru_bx': 2, 'lru_lambda': 2, 'sc_conv_w': 2}
_REPLICATED = ['ffn1_pre_g', 'ffn1_post_g', 'mix_pre_g', 'lru_conv_b', 'sgu_ln_g', 'sgu_ln_b', 'sgu_w_s',
               'sgu_b', 'mix_post_g', 'ffn2_pre_g', 'ffn2_post_g']


def _dot(a, b):
    return jnp.dot(a, b, preferred_element_type=_F32)


def _dot_nt(a, b):
    return lax.dot_general(a, b, (((1,), (1,)), ((), ())), preferred_element_type=_F32)


def _dot_tn(a, b):
    return lax.dot_general(a, b, (((0,), (0,)), ((), ())), preferred_element_type=_F32)


def _sigmoid(x):
    return 0.5 * jnp.tanh(0.5 * x) + 0.5


def _gelu(x):
    t = jnp.tanh(x * (_GELU_K + (_GELU_K * _GELU_C) * (x * x)))
    return (0.5 * x) * (1.0 + t)


def _gelu_and_grad(x):
    x2 = x * x
    t = jnp.tanh(x * (_GELU_K + (_GELU_K * _GELU_C) * x2))
    hx = 0.5 * x
    return hx * (1.0 + t), 0.5 * (1.0 + t) + hx * (1.0 - t * t) * (_GELU_K + (3.0 * _GELU_K * _GELU_C) * x2)


def _gelu_grad(x):
    return _gelu_and_grad(x)[1]


def _rms_fwd(x, g):
    r = lax.rsqrt(jnp.mean(x * x, axis=-1, keepdims=True) + _EPS)
    return x * r * g


def _rms_bwd(dy, x, g):
    r = lax.rsqrt(jnp.mean(x * x, axis=-1, keepdims=True) + _EPS)
    xh = x * r
    dxh = dy * g
    dx = r * (dxh - xh * jnp.mean(dxh * xh, axis=-1, keepdims=True))
    return dx, jnp.sum(dy * xh, axis=0, keepdims=True)


def _neg_softplus_neg(lam):
    e = jnp.exp(-jnp.abs(lam))
    l1p = jnp.where(e < 1e-2, e * (1.0 - e * (0.5 - e * (1.0 / 3.0 - 0.25 * e))), jnp.log(1.0 + e))
    return -_LRU_C * (jnp.maximum(-lam, 0.0) + l1p)


def _shift_rows(xe, d, tm):
    n = xe.shape[0]
    if d == 0:
        return xe[_HALO:_HALO + tm]
    return pltpu.roll(xe, (-d) % n, axis=0)[_HALO:_HALO + tm]


def _with_halo(cur, prev, nxt, first, last):
    p = jnp.where(first, 0.0, prev.astype(_F32))
    n = jnp.where(last, 0.0, nxt.astype(_F32))
    return jnp.concatenate([p, cur.astype(_F32), n], axis=0)


def _rows(arr, tm):
    c = arr.shape[1]
    return (arr, (tm, c), lambda ti: (ti, 0))


def _rows3(arr, k, tm):
    c = arr.shape[2]
    return (arr, (None, tm, c), lambda ti, k=k: (k, ti, 0))


def _halo_prev(arr, tm):
    c = arr.shape[1]
    return (arr, (_HALO, c), lambda ti: (jnp.maximum(ti * (tm // _HALO) - 1, 0), 0))


def _halo_next(arr, tm):
    c = arr.shape[1]
    nblk = arr.shape[0] // _HALO
    return (arr, (_HALO, c), lambda ti: (jnp.minimum((ti + 1) * (tm // _HALO), nblk - 1), 0))


def _full(arr):
    nd = arr.ndim
    return (arr, arr.shape, lambda ti, nd=nd: (0,) * nd)


def _out_rows(t, c, dtype, tm):
    return ((t, c), dtype, (tm, c), lambda ti: (ti, 0))


def _row_call(name, body, n_tiles, ins, outs, accs=(), hbm=(), scratch=(), reverse=False, comm=None):
    n_in, n_hbm, n_out, n_acc = len(ins), len(hbm), len(outs), len(accs)
    c_ins = list(comm.ins) if comm else []
    c_outs = list(comm.outs) if comm else []
    c_sems = list(comm.sems) if comm else []

    def tile_of(step):
        return (n_tiles - 1 - step) if reverse else step

    def spec(block, index_fn):
        return pl.BlockSpec(block, lambda s, f=index_fn: f(tile_of(s)))

    def kern(*refs):
        in_refs = refs[:n_in]
        hbm_refs = refs[n_in:n_in + n_hbm]
        o0 = n_in + n_hbm + len(c_ins)
        cin_refs = refs[n_in + n_hbm:o0]
        out_refs = refs[o0:o0 + n_out]
        acc_refs = refs[o0 + n_out:o0 + n_out + n_acc]
        s0 = o0 + n_out + n_acc + len(c_outs)
        cout_refs = refs[o0 + n_out + n_acc:s0]
        w_refs = refs[s0:s0 + n_hbm]
        csem_refs = refs[s0 + n_hbm:s0 + n_hbm + len(c_sems)]
        scr = refs[s0 + n_hbm + len(c_sems):]
        step = pl.program_id(0)

        @pl.when(step == 0)
        def _():
            if comm:
                comm.start(cin_refs, cout_refs, csem_refs)
            for src, dst in zip(hbm_refs, w_refs):
                pltpu.sync_copy(src, dst)
            for a in acc_refs:
                a[...] = jnp.zeros(a.shape, a.dtype)

        body(step, tile_of(step), in_refs, w_refs, out_refs, acc_refs, scr)

        if comm:
            @pl.when(step == n_tiles - 1)
            def _():
                comm.finish(cin_refs, cout_refs, csem_refs)

    any_spec = pl.BlockSpec(memory_space=pl.ANY)
    in_specs = [spec(b, f) for (_, b, f) in ins] + [any_spec] * (n_hbm + len(c_ins))
    out_specs = [spec(b, f) for (_, _, b, f) in outs]
    out_specs += [pl.BlockSpec(s, lambda st, nd=len(s): (0,) * nd) for s in accs] + [any_spec] * len(c_outs)
    out_shape = [jax.ShapeDtypeStruct(s, d) for (s, d, _, _) in outs]
    out_shape += [jax.ShapeDtypeStruct(s, _F32) for s in accs] + c_outs
    scratch_shapes = [pltpu.VMEM(w.shape, w.dtype) for w in hbm] + c_sems + list(scratch)
    res = pl.pallas_call(
        kern, name=name, grid=(n_tiles,), in_specs=in_specs, out_specs=out_specs, out_shape=out_shape,
        scratch_shapes=scratch_shapes,
        compiler_params=pltpu.CompilerParams(dimension_semantics=("arbitrary",), vmem_limit_bytes=_VMEM_LIMIT),
    )(*[a for (a, _, _) in ins], *hbm, *c_ins)
    return list(res)


def _xty(name, x, y, tk):
    t, k1 = x.shape
    k2 = y.shape[1]

    def kern(x_ref, y_ref, o_ref):
        @pl.when(pl.program_id(0) == 0)
        def _():
            o_ref[...] = jnp.zeros(o_ref.shape, o_ref.dtype)

        o_ref[...] += _dot_tn(x_ref[...], y_ref[...])

    return pl.pallas_call(
        kern, name=name, grid=(t // tk,),
        in_specs=[pl.BlockSpec((tk, k1), lambda k: (k, 0)), pl.BlockSpec((tk, k2), lambda k: (k, 0))],
        out_specs=pl.BlockSpec((k1, k2), lambda k: (0, 0)),
        out_shape=jax.ShapeDtypeStruct((k1, k2), _F32),
        compiler_params=pltpu.CompilerParams(dimension_semantics=("arbitrary",), vmem_limit_bytes=_VMEM_LIMIT),
    )(x, y)


def _ffn_up(name, x, pre_g, wg, wu, tm):
    t, d = x.shape
    f = wg.shape[1]

    def body(step, ti, ins, ws, outs, accs, scr):
        x_ref, g_ref = ins
        h = _rms_fwd(x_ref[...], g_ref[...]).astype(_MM)
        a = _dot(h, ws[0][...])
        b = _dot(h, ws[1][...])
        outs[0][...] = h
        outs[1][...] = a.astype(_MM)
        outs[2][...] = b.astype(_MM)
        outs[3][...] = (a * _sigmoid(a) * b).astype(_MM)

    return _row_call(name, body, t // tm, [_rows(x, tm), _full(pre_g)],
                     [_out_rows(t, d, _MM, tm), _out_rows(t, f, _MM, tm), _out_rows(t, f, _MM, tm),
                      _out_rows(t, f, _MM, tm)], hbm=[wg, wu])


def _proj_norm_res(name, lhs, x, post_g, w, scale, tm):
    t, d = x.shape

    def body(step, ti, ins, ws, outs, accs, scr):
        l_ref, x_ref, g_ref = ins
        f = _dot(l_ref[...], ws[0][...])
        outs[0][...] = f
        outs[1][...] = x_ref[...] + scale * _rms_fwd(f, g_ref[...])

    return _row_call(name, body, t // tm, [_rows(lhs, tm), _rows(x, tm), _full(post_g)],
                     [_out_rows(t, d, _F32, tm), _out_rows(t, d, _F32, tm)], hbm=[w])


def _ffn_bwd_post(name, dxo, f, a, b, post_g, wd, scale, tm, comm=None):
    t, d = dxo.shape
    ff = a.shape[1]

    def body(step, ti, ins, ws, outs, accs, scr):
        dxo_ref, f_ref, a_ref, b_ref, g_ref = ins
        df, dg = _rms_bwd(scale * dxo_ref[...], f_ref[...], g_ref[...])
        accs[0][...] += dg
        dfb = df.astype(_MM)
        ds = _dot_nt(dfb, ws[0][...])
        a32 = a_ref[...].astype(_F32)
        b32 = b_ref[...].astype(_F32)
        sg = _sigmoid(a32)
        outs[0][...] = (ds * b32 * (sg * (1.0 + a32 * (1.0 - sg)))).astype(_MM)
        outs[1][...] = (ds * (a32 * sg)).astype(_MM)
        outs[2][...] = dfb

    return _row_call(name, body, t // tm,
                     [_rows(dxo, tm), _rows(f, tm), _rows(a, tm), _rows(b, tm), _full(post_g)],
                     [_out_rows(t, ff, _MM, tm), _out_rows(t, ff, _MM, tm), _out_rows(t, d, _MM, tm)],
                     accs=[(1, d)], hbm=[wd], comm=comm)


def _bwd_in_norm(name, dzs, ws_list, x, dxo, pre_g, tm):
    t, d = x.shape
    nz = len(dzs)

    def body(step, ti, ins, ws, outs, accs, scr):
        dh = _dot_nt(ins[0][...], ws[0][...])
        for k in range(1, nz):
            dh = dh + _dot_nt(ins[k][...], ws[k][...])
        x_ref, dxo_ref, g_ref = ins[nz:]
        dx, dg = _rms_bwd(dh, x_ref[...], g_ref[...])
        accs[0][...] += dg
        outs[0][...] = dxo_ref[...] + dx

    return _row_call(name, body, t // tm,
                     [_rows(z, tm) for z in dzs] + [_rows(x, tm), _rows(dxo, tm), _full(pre_g)],
                     [_out_rows(t, d, _F32, tm)], accs=[(1, d)], hbm=list(ws_list))


def _mix_in(name, x, pre_g, w_parts, tm, comm=None):
    t, d = x.shape

    def body(step, ti, ins, ws, outs, accs, scr):
        x_ref, g_ref = ins
        h = _rms_fwd(x_ref[...], g_ref[...]).astype(_MM)
        outs[0][...] = h
        for k in range(len(ws)):
            outs[1 + k][...] = _dot(h, ws[k][...]).astype(_MM)

    return _row_call(name, body, t // tm, [_rows(x, tm), _full(pre_g)],
                     [_out_rows(t, d, _MM, tm)] + [_out_rows(t, w.shape[1], _MM, tm) for w in w_parts],
                     hbm=list(w_parts), comm=comm)


def _lru_conv(xe, cw, cb, tm):
    xc = cb
    for k in range(4):
        xc = xc + _shift_rows(xe, k - 2, tm) * cw[k:k + 1, :]
    return xc


def _lru_gates(xc, wa_ref, wx_ref, ba, bx, c):
    dh = xc.shape[1] // _HEADS
    gas, gxs = [], []
    for hh in range(_HEADS):
        xs = xc[:, hh * dh:(hh + 1) * dh].astype(_MM)
        gas.append(_dot(xs, wa_ref[hh]))
        gxs.append(_dot(xs, wx_ref[hh]))
    r = _sigmoid(jnp.concatenate(gas, axis=1) + ba)
    i = _sigmoid(jnp.concatenate(gxs, axis=1) + bx)
    la = c * r
    a = jnp.exp(la)
    em = -jnp.tanh(la) * (a * a + 1.0)
    return r, i, a, em


def _scan_scratch(tm, w):
    return [pltpu.VMEM((tm, w), _F32), pltpu.VMEM((tm, w), _F32)]


def _tile_scan(a, u, scan_scr, h_dst, carry, tm, descending):
    a_scr, u_scr = scan_scr
    a_scr[...] = a
    u_scr[...] = u
    ng = tm // 8
    w = a.shape[1]
    row = lax.broadcasted_iota(jnp.int32, (8, w), 0)

    def grp(j, carry):
        g = (ng - 1 - j) if descending else j
        r0 = pl.multiple_of(g * 8, 8)
        a8 = a_scr[pl.ds(r0, 8), :]
        u8 = u_scr[pl.ds(r0, 8), :]
        for dd in (1, 2, 4):
            if descending:
                ok = row < 8 - dd
                sh = 8 - dd
            else:
                ok = row >= dd
                sh = dd
            a_s = jnp.where(ok, pltpu.roll(a8, sh, axis=0), 1.0)
            u_s = jnp.where(ok, pltpu.roll(u8, sh, axis=0), 0.0)
            u8 = a8 * u_s + u8
            a8 = a8 * a_s
        h8 = u8 + a8 * carry
        h_dst[pl.ds(r0, 8), :] = h8
        return h8[0:1, :] if descending else h8[7:8, :]

    return lax.fori_loop(0, ng, grp, carry, unroll=2)


def _lru_fwd(name, zx, cw, cb, wa, wx, ba, bx, lam, tm, descending, hf=None, zg=None):
    t, w = zx.shape
    n = t // tm

    def body(step, ti, ins, ws, outs, accs, scr):
        zc, zp, zn, cw_r, cb_r, wa_r, wx_r, ba_r, bx_r, lam_r = ins[:10]
        carry_scr = scr[2]
        xe = _with_halo(zc[...], zp[...], zn[...], ti == 0, ti == n - 1)
        xc = _lru_conv(xe, cw_r[...], cb_r[...], tm)
        c = _neg_softplus_neg(lam_r[...])
        r, i, a, em = _lru_gates(xc, wa_r, wx_r, ba_r[...], bx_r[...], c)

        @pl.when(step == 0)
        def _():
            carry_scr[...] = jnp.zeros(carry_scr.shape, _F32)

        carry_scr[...] = _tile_scan(a, i * xc * jnp.sqrt(em), scr[:2], outs[0], carry_scr[...], tm, descending)
        if descending:
            hf_r, zg_r = ins[10:]
            outs[1][...] = ((hf_r[...] + outs[0][...]) * _gelu(zg_r[...].astype(_F32))).astype(_MM)

    ins = [_rows(zx, tm), _halo_prev(zx, tm), _halo_next(zx, tm), _full(cw), _full(cb), _full(wa), _full(wx),
           _full(ba), _full(bx), _full(lam)]
    outs = [_out_rows(t, w, _F32, tm)]
    if descending:
        ins += [_rows(hf, tm), _rows(zg, tm)]
        outs += [_out_rows(t, w, _MM, tm)]
    scratch = _scan_scratch(tm, w) + [pltpu.VMEM((1, w), _F32)]
    return _row_call(name, body, n, ins, outs, scratch=scratch, reverse=descending)


def _lru_bwd(name, zx, d_in, h_own, cw, cb, wa, wx, ba, bx, lam, tm, direction, zg=None, h_other=None,
             dxc_in=None, comm=None):
    t, w = zx.shape
    n = t // tm
    dh_ = w // _HEADS
    adj_desc = direction == 0

    def body(step, ti, ins, ws, outs, accs, scr):
        zc, zp, zn, din_r, ho_r, hh_r, cw_r, cb_r, wa_r, wx_r, ba_r, bx_r, lam_r = ins[:13]
        p_scr, carry_scr = scr[2:]
        first, last = ti == 0, ti == n - 1
        xe = _with_halo(zc[...], zp[...], zn[...], first, last)
        xc = _lru_conv(xe, cw_r[...], cb_r[...], tm)
        lam_v = lam_r[...]
        c = _neg_softplus_neg(lam_v)
        r, i, a, em = _lru_gates(xc, wa_r, wx_r, ba_r[...], bx_r[...], c)
        m = jnp.sqrt(em)
        if direction == 0:
            gel, gel_grad = _gelu_and_grad(ins[13][...].astype(_F32))
            dpa_v = din_r[...]
            d_h = dpa_v * gel
        else:
            d_h = din_r[...]
        @pl.when(step == 0)
        def _():
            carry_scr[...] = jnp.zeros(carry_scr.shape, _F32)

        carry_in = carry_scr[...]
        carry_scr[...] = _tile_scan(a, a * d_h, scr[:2], p_scr, carry_in, tm, adj_desc)
        p = p_scr[...]
        row = lax.broadcasted_iota(jnp.int32, (tm, w), 0)
        h_t = ho_r[...]
        if adj_desc:
            p_nb = jnp.where(row == tm - 1, carry_in, pltpu.roll(p, tm - 1, axis=0))
            edge = jnp.where(first, 0.0, hh_r[_HALO - 1:_HALO, :])
            h_nb = jnp.where(row == 0, edge, pltpu.roll(h_t, 1, axis=0))
        else:
            p_nb = jnp.where(row == 0, carry_in, pltpu.roll(p, 1, axis=0))
            edge = jnp.where(last, 0.0, hh_r[0:1, :])
            h_nb = jnp.where(row == tm - 1, edge, pltpu.roll(h_t, tm - 1, axis=0))
        g = d_h + p_nb
        gi = g * i
        d_i = g * xc * m
        dxc = gi * m
        d_m = gi * xc
        d_l = g * h_nb * a - d_m * (1.0 - em) / m
        accs[4][...] += jnp.sum(d_l * r, axis=0, keepdims=True)
        dga = d_l * c * r * (1.0 - r)
        dgx = d_i * i * (1.0 - i)
        accs[2][...] += jnp.sum(dga, axis=0, keepdims=True)
        accs[3][...] += jnp.sum(dgx, axis=0, keepdims=True)
        parts = []
        for hh in range(_HEADS):
            sl = slice(hh * dh_, (hh + 1) * dh_)
            xs = xc[:, sl].astype(_MM)
            da_h = dga[:, sl].astype(_MM)
            dx_h = dgx[:, sl].astype(_MM)
            accs[0][hh] += _dot_tn(xs, da_h)
            accs[1][hh] += _dot_tn(xs, dx_h)
            parts.append(_dot_nt(da_h, wa_r[hh]) + _dot_nt(dx_h, wx_r[hh]))
        dxc = dxc + jnp.concatenate(parts, axis=1)
        if direction == 0:
            outs[0][...] = dxc
            outs[1][...] = (dpa_v * (h_t + ins[14][...]) * gel_grad).astype(_MM)
            outs[2][...] = d_h
        else:
            outs[0][...] = dxc + ins[13][...]

        @pl.when(step == n - 1)
        def _():
            accs[4][...] = accs[4][...] * (_LRU_C * _sigmoid(-lam_v))

    halo_h = _halo_prev(h_own, tm) if adj_desc else _halo_next(h_own, tm)
    ins = [_rows(zx, tm), _halo_prev(zx, tm), _halo_next(zx, tm), _rows(d_in, tm), _rows(h_own, tm), halo_h,
           _full(cw), _full(cb), _full(wa), _full(wx), _full(ba), _full(bx), _full(lam)]
    outs = [_out_rows(t, w, _F32, tm)]
    if direction == 0:
        ins += [_rows(zg, tm), _rows(h_other, tm)]
        outs += [_out_rows(t, w, _MM, tm), _out_rows(t, w, _F32, tm)]
    else:
        ins += [_rows(dxc_in, tm)]
    accs = [(_HEADS, dh_, dh_), (_HEADS, dh_, dh_), (1, w), (1, w), (1, w)]
    scratch = _scan_scratch(tm, w) + [pltpu.VMEM((tm, w), _F32), pltpu.VMEM((1, w), _F32)]
    return _row_call(name, body, n, ins, outs, accs=accs, scratch=scratch, reverse=adj_desc, comm=comm)


def _lru_conv_bwd(name, dxc, zx, cw, tm):
    t, w = zx.shape
    n = t // tm

    def body(step, ti, ins, ws, outs, accs, scr):
        dc, dp, dn, zc, zp, zn, cw_r = ins
        first, last = ti == 0, ti == n - 1
        de = _with_halo(dc[...], dp[...], dn[...], first, last)
        ze = _with_halo(zc[...], zp[...], zn[...], first, last)
        cw_v = cw_r[...]
        d_cur = dc[...]
        dz = None
        for k in range(4):
            term = _shift_rows(de, 2 - k, tm) * cw_v[k:k + 1, :]
            dz = term if dz is None else dz + term
            accs[0][k:k + 1, :] += jnp.sum(d_cur * _shift_rows(ze, k - 2, tm), axis=0, keepdims=True)
        accs[1][...] += jnp.sum(d_cur, axis=0, keepdims=True)
        outs[0][...] = dz.astype(_MM)

    ins = [_rows(dxc, tm), _halo_prev(dxc, tm), _halo_next(dxc, tm),
           _rows(zx, tm), _halo_prev(zx, tm), _halo_next(zx, tm), _full(cw)]
    return _row_call(name, body, n, ins, [_out_rows(t, w, _MM, tm)], accs=[(4, w), (1, w)])


def _sgu_mix(v2, ws_ref, bias, mixed_scr, tm):
    gw = v2.shape[1]
    gh = gw // _HEADS
    for nn in range(tm // _CHUNK):
        rs = slice(nn * _CHUNK, (nn + 1) * _CHUNK)
        for g in range(_HEADS):
            cs = slice(g * gh, (g + 1) * gh)
            mixed_scr[rs, cs] = _dot(ws_ref[g], v2[rs, cs].astype(_MM)) + bias[:, cs]
    return mixed_scr[...]


def _ln_fwd(v1, lg, lb):
    mu = jnp.mean(v1, axis=-1, keepdims=True)
    vc = v1 - mu
    rs = lax.rsqrt(jnp.mean(vc * vc, axis=-1, keepdims=True) + _EPS)
    vn = vc * rs
    return vn * lg + lb, vn, rs


def _bc_fwd(name, zmid, scw, lg, lb, ws_mm, bias, sw, gw, tm):
    t = zmid.shape[0]
    n = t // tm

    def body(step, ti, ins, ws, outs, accs, scr):
        zc, zp, zn, scw_r, lg_r, lb_r, ws_r, bias_r = ins
        first, last = ti == 0, ti == n - 1
        z = zc[...].astype(_F32)
        zb, zcc, zxx = z[:, 0:sw], z[:, sw:2 * sw], z[:, 2 * sw:3 * sw]
        zu, zv = z[:, 3 * sw:3 * sw + gw], z[:, 3 * sw + gw:3 * sw + 2 * gw]
        zpv, znv = zp[...].astype(_F32), zn[...].astype(_F32)
        qe = _with_halo(zcc * zxx, zpv[:, sw:2 * sw] * zpv[:, 2 * sw:3 * sw],
                        znv[:, sw:2 * sw] * znv[:, 2 * sw:3 * sw], first, last)
        scw_v = scw_r[...]
        cq = None
        for k in range(3):
            term = _shift_rows(qe, k - 1, tm) * scw_v[k:k + 1, :]
            cq = term if cq is None else cq + term
        outs[0][...] = (zb * cq).astype(_MM)
        v2, _, _ = _ln_fwd(_gelu(zv), lg_r[...], lb_r[...])
        mixed = _sgu_mix(v2, ws_r, bias_r[...], scr[0], tm)
        outs[1][...] = (_gelu(zu) * mixed).astype(_MM)

    ins = [_rows(zmid, tm), _halo_prev(zmid, tm), _halo_next(zmid, tm), _full(scw), _full(lg), _full(lb),
           _full(ws_mm), _full(bias)]
    return _row_call(name, body, n, ins, [_out_rows(t, sw, _MM, tm), _out_rows(t, gw, _MM, tm)],
                     scratch=[pltpu.VMEM((tm, gw), _F32)])


def _bc_bwd(name, zmid, dpb, dpc, scw, lg, lb, ws_mm, wst_mm, bias, sw, gw, tm):
    t = zmid.shape[0]
    n = t // tm
    gh = gw // _HEADS

    def body(step, ti, ins, ws, outs, accs, scr):
        zc, zp, zn, db_c, db_p, db_n, dc_r, scw_r, lg_r, lb_r, ws_r, wst_r, bias_r = ins
        mixed_scr, dv2_scr = scr
        first, last = ti == 0, ti == n - 1
        z = zc[...].astype(_F32)
        zb, zcc, zxx = z[:, 0:sw], z[:, sw:2 * sw], z[:, 2 * sw:3 * sw]
        zu, zv = z[:, 3 * sw:3 * sw + gw], z[:, 3 * sw + gw:3 * sw + 2 * gw]
        zpv, znv = zp[...].astype(_F32), zn[...].astype(_F32)
        qe = _with_halo(zcc * zxx, zpv[:, sw:2 * sw] * zpv[:, 2 * sw:3 * sw],
                        znv[:, sw:2 * sw] * znv[:, 2 * sw:3 * sw], first, last)
        dpb_v = db_c[...]
        dcq = dpb_v * zb
        dcqe = _with_halo(dcq, db_p[...] * zpv[:, 0:sw], db_n[...] * znv[:, 0:sw], first, last)
        scw_v = scw_r[...]
        cq, dq = None, None
        for k in range(3):
            qk = _shift_rows(qe, k - 1, tm)
            term = qk * scw_v[k:k + 1, :]
            cq = term if cq is None else cq + term
            dterm = _shift_rows(dcqe, 1 - k, tm) * scw_v[k:k + 1, :]
            dq = dterm if dq is None else dq + dterm
            accs[0][k:k + 1, :] += jnp.sum(dcq * qk, axis=0, keepdims=True)
        outs[0][:, 0:sw] = (dpb_v * cq).astype(_MM)
        outs[0][:, sw:2 * sw] = (dq * zxx).astype(_MM)
        outs[0][:, 2 * sw:3 * sw] = (dq * zcc).astype(_MM)
        lg_v = lg_r[...]
        v2, vn, rs = _ln_fwd(_gelu(zv), lg_v, lb_r[...])
        mixed = _sgu_mix(v2, ws_r, bias_r[...], mixed_scr, tm)
        dpc_v = dc_r[...]
        outs[0][:, 3 * sw:3 * sw + gw] = (dpc_v * mixed * _gelu_grad(zu)).astype(_MM)
        dmix = dpc_v * _gelu(zu)
        for nn in range(tm // _CHUNK):
            rsl = slice(nn * _CHUNK, (nn + 1) * _CHUNK)
            accs[4][...] += dmix[rsl, :]
            for g in range(_HEADS):
                cs = slice(g * gh, (g + 1) * gh)
                dm_b = dmix[rsl, cs].astype(_MM)
                accs[3][g] += _dot_nt(dm_b, v2[rsl, cs].astype(_MM))
                dv2_scr[rsl, cs] = _dot(wst_r[g], dm_b)
        dv2 = dv2_scr[...]
        accs[1][...] += jnp.sum(dv2 * vn, axis=0, keepdims=True)
        accs[2][...] += jnp.sum(dv2, axis=0, keepdims=True)
        dvn = dv2 * lg_v
        dv1 = rs * (dvn - jnp.mean(dvn, axis=-1, keepdims=True)
                    - vn * jnp.mean(dvn * vn, axis=-1, keepdims=True))
        outs[0][:, 3 * sw + gw:3 * sw + 2 * gw] = (dv1 * _gelu_grad(zv)).astype(_MM)

    ins = [_rows(zmid, tm), _halo_prev(zmid, tm), _halo_next(zmid, tm),
           _rows(dpb, tm), _halo_prev(dpb, tm), _halo_next(dpb, tm), _rows(dpc, tm),
           _full(scw), _full(lg), _full(lb), _full(ws_mm), _full(wst_mm), _full(bias)]
    accs = [(3, sw), (1, gw), (1, gw), (_HEADS, _CHUNK, _CHUNK), (_CHUNK, gw)]
    return _row_call(name, body, n, ins, [_out_rows(t, 3 * sw + 2 * gw, _MM, tm)], accs=accs,
                     scratch=[pltpu.VMEM((tm, gw), _F32), pltpu.VMEM((tm, gw), _F32)])


def _mix_proj(name, pa, pb, pc, zm, wlo, wsc, wsg, tm):
    t = pa.shape[0]
    d = wlo.shape[1]

    def body(step, ti, ins, ws, outs, accs, scr):
        ys = [_dot(ins[k][...], ws[k][...]) for k in range(3)]
        gm = _sigmoid(ins[3][...].astype(_F32))
        m = None
        for k in range(3):
            outs[k][...] = ys[k].astype(_MM)
            term = gm[:, k * d:(k + 1) * d] * ys[k]
            m = term if m is None else m + term
        outs[3][...] = m.astype(_MM)

    return _row_call(name, body, t // tm, [_rows(pa, tm), _rows(pb, tm), _rows(pc, tm), _rows(zm, tm)],
                     [_out_rows(t, d, _MM, tm)] * 4, hbm=[wlo, wsc, wsg])


def _mix_bwd_out(name, dxo, mo, ya, yb, yc, zm, post_g, wo, tm):
    t, d = dxo.shape

    def body(step, ti, ins, ws, outs, accs, scr):
        dxo_ref, mo_ref, ya_r, yb_r, yc_r, zm_r, g_ref = ins
        dmo, dg = _rms_bwd(dxo_ref[...], mo_ref[...], g_ref[...])
        accs[0][...] += dg
        dmob = dmo.astype(_MM)
        outs[0][...] = dmob
        dm = _dot_nt(dmob, ws[0][...])
        gm = _sigmoid(zm_r[...].astype(_F32))
        for k, y_r in enumerate((ya_r, yb_r, yc_r)):
            gk = gm[:, k * d:(k + 1) * d]
            outs[1 + k][...] = (dm * gk).astype(_MM)
            outs[4][:, k * d:(k + 1) * d] = (dm * y_r[...].astype(_F32) * gk * (1.0 - gk)).astype(_MM)

    ins = [_rows(dxo, tm), _rows(mo, tm), _rows(ya, tm), _rows(yb, tm), _rows(yc, tm), _rows(zm, tm),
           _full(post_g)]
    return _row_call(name, body, t // tm, ins,
                     [_out_rows(t, d, _MM, tm)] * 4 + [_out_rows(t, 3 * d, _MM, tm)], accs=[(1, d)], hbm=[wo])


def _mix_bwd_proj(name, dya, dyb, dyc, wlo, wsc, wsg, tm):
    t = dya.shape[0]

    def body(step, ti, ins, ws, outs, accs, scr):
        for k in range(3):
            outs[k][...] = _dot_nt(ins[k][...], ws[k][...])

    return _row_call(name, body, t // tm, [_rows(dya, tm), _rows(dyb, tm), _rows(dyc, tm)],
                     [_out_rows(t, w.shape[0], _F32, tm) for w in (wlo, wsc, wsg)], hbm=[wlo, wsc, wsg])


def _loss_grad(name, y, target, tm):
    t, d = y.shape

    def body(step, ti, ins, ws, outs, accs, scr):
        err = ins[0][...] - ins[1][...]
        outs[0][...] = err * (1.0 / d)
        accs[0][...] += (0.5 / d) * jnp.sum(err * err)

    dy, acc = _row_call(name, body, t // tm, [_rows(y, tm), _rows(target, tm)], [_out_rows(t, d, _F32, tm)],
                        accs=[(1, 128)])
    return acc[0, 0], dy


def _adamw(w, g, m, v):
    m = _ADAM_B1 * m + (1.0 - _ADAM_B1) * g
    v = _ADAM_B2 * v + (1.0 - _ADAM_B2) * (g * g)
    m_hat = m / (1.0 - _ADAM_B1 ** _ADAM_STEP)
    v_hat = v / (1.0 - _ADAM_B2 ** _ADAM_STEP)
    delta = -_ADAM_LR * (m_hat / (jnp.sqrt(v_hat) + _ADAM_EPS) + _ADAM_WD * w)
    return delta, m, v


def _flat_tm(rows):
    return 512 if rows % 512 == 0 else rows


def _sum_adamw(name, parts, w, m, v):
    rows, c = w.shape
    tm = _flat_tm(rows)
    np_ = len(parts)

    def body(step, ti, ins, ws, outs, accs, scr):
        g = ins[0][...].astype(_F32)
        for k in range(1, np_):
            g = g + ins[k][...].astype(_F32)
        delta, nm, nv = _adamw(ins[np_][...], g, ins[np_ + 1][...], ins[np_ + 2][...])
        outs[0][...] = g
        outs[1][...] = delta
        outs[2][...] = nm
        outs[3][...] = nv

    return _row_call(name, body, rows // tm, list(parts) + [_rows(w, tm), _rows(m, tm), _rows(v, tm)],
                     [_out_rows(rows, c, _F32, tm)] * 4)


def _sum_parts(name, parts, rows, c, out_dtype):
    tm = _flat_tm(rows)

    def body(step, ti, ins, ws, outs, accs, scr):
        g = ins[0][...].astype(_F32)
        for k in range(1, len(ins)):
            g = g + ins[k][...].astype(_F32)
        outs[0][...] = g.astype(out_dtype)

    return _row_call(name, body, rows // tm, list(parts), [_out_rows(rows, c, out_dtype, tm)])[0]


def _pack(arrs, dtype):
    flat = jnp.concatenate([a.reshape(-1).astype(dtype) for a in arrs])
    return _to_rows(flat, _flat_rows(flat.shape[0]))


def _pack_rows(arrs, dtype):
    parts = [a.reshape(-1, _LANES).astype(dtype) for a in arrs]
    n = sum(p.shape[0] for p in parts)
    rows = _flat_rows(n * _LANES)
    if rows > n:
        parts.append(jnp.zeros((rows - n, _LANES), dtype))
    return jnp.concatenate(parts, axis=0)


def _unpack_rows(flat, shapes):
    out, off = [], 0
    for s in shapes:
        n = math.prod(s) // _LANES
        out.append(flat[off:off + n].reshape(tuple(s)))
        off += n
    return out


def _unpack(flat, shapes):
    v = flat.reshape(-1)
    out, off = [], 0
    for s in shapes:
        n = math.prod(s)
        out.append(v[off:off + n].reshape(tuple(s)))
        off += n
    return out


def _place():
    return lax.axis_index("x"), lax.axis_index("y"), lax.axis_index("c")


class _Exchange:
    def __init__(self, ins, outs, sems, start, finish):
        self.ins, self.outs, self.sems, self.start, self.finish = ins, outs, sems, start, finish


def _run_exchange(name, ex):
    n_in, n_out = len(ex.ins), len(ex.outs)

    def body(*refs):
        ins, outs, sems = refs[:n_in], refs[n_in:n_in + n_out], refs[n_in + n_out:]
        ex.start(ins, outs, sems)
        ex.finish(ins, outs, sems)

    any_spec = pl.BlockSpec(memory_space=pl.ANY)
    return pl.pallas_call(body, name=name, out_shape=list(ex.outs), in_specs=[any_spec] * n_in,
                          out_specs=[any_spec] * n_out, scratch_shapes=list(ex.sems))(*ex.ins)


def _gather_exchange(block):
    r, c_ = block.shape

    def copies(ins, outs, sems):
        x_ref, out_ref = ins[0], outs[0]
        send_sems, recv_sems, local_sem = sems
        x, y, c = _place()
        me, sibling = (x, y, c), (x, y, 1 - c)
        chips = [(1 - x, y), (x, 1 - y), (1 - x, 1 - y)]

        def rows(px, py, pc):
            return out_ref.at[4 * px + 2 * py + pc]

        def copy(k, blk, to, src=None):
            return pltpu.make_async_remote_copy(
                src_ref=rows(*blk) if src is None else src, dst_ref=rows(*blk),
                send_sem=send_sems.at[k], recv_sem=recv_sems.at[k], device_id=to, device_id_type=_MESH)

        mine = pltpu.make_async_copy(x_ref, rows(*me), local_sem)
        first = [copy(0, me, sibling, src=x_ref)]
        first += [copy(1 + j, me, (*chip, c), src=x_ref) for j, chip in enumerate(chips)]
        passed = [copy(4 + j, (*chip, c), sibling) for j, chip in enumerate(chips)]
        landed = [copy(1 + j, (*chip, c), me) for j, chip in enumerate(chips)]
        landed_later = [copy(0, sibling, me)] + [copy(4 + j, (*chip, 1 - c), me) for j, chip in enumerate(chips)]
        return mine, first, passed, landed, landed_later

    def start(ins, outs, sems):
        mine, first, _, _, _ = copies(ins, outs, sems)
        mine.start()
        for cp in first:
            cp.start()

    def finish(ins, outs, sems):
        mine, first, passed, landed, landed_later = copies(ins, outs, sems)
        for j in range(3):
            landed[j].wait_recv()
            passed[j].start()
        for cp in landed_later:
            cp.wait_recv()
        for cp in first + passed:
            cp.wait_send()
        mine.wait()

    return _Exchange([block], [jax.ShapeDtypeStruct((_NDEV, r, c_), block.dtype)],
                     [pltpu.SemaphoreType.DMA((7,)), pltpu.SemaphoreType.DMA((7,)), pltpu.SemaphoreType.DMA(())],
                     start, finish)


def _all_gather(name, block):
    return _run_exchange(name, _gather_exchange(block))[0]


def _sibling_exchange_of(dm):
    _, _, r, c_ = dm.shape

    def copies(ins, outs, sems):
        x, y, c = _place()
        return [pltpu.make_async_remote_copy(
            src_ref=ins[0].at[k, 1 - c], dst_ref=outs[0].at[k], send_sem=sems[0].at[k],
            recv_sem=sems[1].at[k], device_id=(x, y, 1 - c), device_id_type=_MESH) for k in range(4)]

    def start(ins, outs, sems):
        for cp in copies(ins, outs, sems):
            cp.start()

    def finish(ins, outs, sems):
        for cp in copies(ins, outs, sems):
            cp.wait()

    return _Exchange([dm], [jax.ShapeDtypeStruct((4, r, c_), dm.dtype)],
                     [pltpu.SemaphoreType.DMA((4,)), pltpu.SemaphoreType.DMA((4,))], start, finish)


def _pair_sum(name, dm, got, core):
    _, _, r, c_ = dm.shape
    tm = _flat_tm(r)

    def kern(core_ref, a_ref, b_ref, o_ref):
        o_ref[...] = (a_ref[...] + b_ref[...]).astype(o_ref.dtype)

    grid_spec = pltpu.PrefetchScalarGridSpec(
        num_scalar_prefetch=1, grid=(4, r // tm),
        in_specs=[pl.BlockSpec((None, None, tm, c_), lambda k, i, cr: (k, cr[0], i, 0)),
                  pl.BlockSpec((None, tm, c_), lambda k, i, cr: (k, i, 0))],
        out_specs=pl.BlockSpec((None, tm, c_), lambda k, i, cr: (k, i, 0)))
    return pl.pallas_call(
        kern, name=name, grid_spec=grid_spec, out_shape=jax.ShapeDtypeStruct((4, r, c_), _MM),
        compiler_params=pltpu.CompilerParams(dimension_semantics=("arbitrary", "arbitrary"),
                                             vmem_limit_bytes=_VMEM_LIMIT),
    )(core, dm, got)


def _chip_sum(name, p, others, chip):
    _, r, c_ = p.shape
    tm = _flat_tm(r)

    def kern(chip_ref, p_ref, o0, o1, o2, g_out):
        g_out[...] = (p_ref[...].astype(_F32) + o0[...].astype(_F32) + o1[...].astype(_F32)
                      + o2[...].astype(_F32))

    grid_spec = pltpu.PrefetchScalarGridSpec(
        num_scalar_prefetch=1, grid=(r // tm,),
        in_specs=[pl.BlockSpec((None, tm, c_), lambda i, cr: (cr[0], i, 0))]
        + [pl.BlockSpec((None, tm, c_), lambda i, cr, k=k: (k, i, 0)) for k in range(3)],
        out_specs=pl.BlockSpec((tm, c_), lambda i, cr: (i, 0)))
    return pl.pallas_call(
        kern, name=name, grid_spec=grid_spec, out_shape=jax.ShapeDtypeStruct((r, c_), _F32),
        compiler_params=pltpu.CompilerParams(dimension_semantics=("arbitrary",), vmem_limit_bytes=_VMEM_LIMIT),
    )(chip, p, others, others, others)


def _row_tile(rows, lanes):
    for d in range(min(rows, max(8, (1 << 18) // lanes)) // 8 * 8, 7, -8):
        if rows % d == 0:
            return d
    return rows


def _adamw_update(name, w, g, m, v):
    shape = w.shape
    as2d = lambda a: a.reshape(-1, shape[-1])
    rows = math.prod(shape[:-1])
    tm = _row_tile(rows, shape[-1])

    def body(step, ti, ins, ws, outs, accs, scr):
        delta, nm, nv = _adamw(ins[0][...], ins[1][...], ins[2][...], ins[3][...])
        outs[0][...] = delta
        outs[1][...] = nm
        outs[2][...] = nv

    res = _row_call(name, body, rows // tm, [_rows(as2d(a), tm) for a in (w, g, m, v)],
                    [_out_rows(rows, shape[-1], _F32, tm)] * 3)
    return [r.reshape(shape) for r in res]


def _chip_exchange_of(p):
    _, r, c_ = p.shape

    def copies(ins, outs, sems):
        x, y, c = _place()
        chips = [(1 - x, y), (x, 1 - y), (1 - x, 1 - y)]
        return [pltpu.make_async_remote_copy(
            src_ref=ins[0].at[2 * px + py], dst_ref=outs[0].at[j], send_sem=sems[0].at[j],
            recv_sem=sems[1].at[j], device_id=(px, py, c), device_id_type=_MESH)
            for j, (px, py) in enumerate(chips)]

    def start(ins, outs, sems):
        for cp in copies(ins, outs, sems):
            cp.start()

    def finish(ins, outs, sems):
        for cp in copies(ins, outs, sems):
            cp.wait()

    return _Exchange([p], [jax.ShapeDtypeStruct((3, r, c_), p.dtype)],
                     [pltpu.SemaphoreType.DMA((3,)), pltpu.SemaphoreType.DMA((3,))], start, finish)


def _gather_full(gathered, names, shard_shapes, axes, unpack):
    per_dev = [unpack(gathered[d], shard_shapes) for d in range(_NDEV)]
    return {nme: jnp.concatenate([per_dev[d][i] for d in range(_NDEV)], axis=ax)
            for i, (nme, ax) in enumerate(zip(names, axes))}


def _block_rows(a, ax, d):
    s = a.shape[ax] // _NDEV
    return lax.slice_in_dim(a, d * s, (d + 1) * s, axis=ax).reshape(-1, _LANES)


def _to_rows(flat, rows):
    return jnp.pad(flat, (0, rows * _LANES - flat.shape[0])).reshape(rows, _LANES)


def _flat_rows(n):
    rows = -(-n // _LANES)
    return -(-rows // 512) * 512 if rows >= 512 else -(-rows // 16) * 16


def _tm(t, want):
    return min(t, want)


def _ffn_forward(tag, x, pre_g, wg, wu, wd, post_g):
    t = x.shape[0]
    h, a, b, s = _ffn_up(tag + "_up", x, pre_g, wg, wu, _tm(t, 256))
    f, x_out = _proj_norm_res(tag + "_down", s, x, post_g, wd, 0.5, _tm(t, 512))
    return x_out, dict(x=x, h=h, a=a, b=b, s=s, f=f)


def _ffn_backward(tag, dxo, sv, pre_g, wg, wu, wd, post_g, comm=None):
    t = dxo.shape[0]
    da, db, df, dpost, *carried = _ffn_bwd_post(tag + "_bwd_post", dxo, sv['f'], sv['a'], sv['b'], post_g, wd,
                                                0.5, _tm(t, 256), comm=comm)
    dx, dpre = _bwd_in_norm(tag + "_bwd_pre", [da, db], [wg, wu], sv['x'], dxo, pre_g, _tm(t, 512))
    tk = _tm(t, 512)
    grads = dict(pre_g=dpre[0], post_g=dpost[0],
                 w_gate=_xty(tag + "_dwg", sv['h'], da, tk), w_up=_xty(tag + "_dwu", sv['h'], db, tk),
                 w_down=_xty(tag + "_dwd", sv['s'], df, tk))
    return dx, grads, carried


def _mixer_weights(w):
    lw = w['lru_conv_w'].shape[-1]
    sw = w['sc_conv_w'].shape[-1]
    gw = w['sgu_ln_g'].shape[-1]
    win = w['w_in']
    cuts = [0, lw, 2 * lw, 2 * lw + 3 * sw + 2 * gw, win.shape[1]]
    p = dict(lw=lw, sw=sw, gw=gw,
             win=[win[:, cuts[k]:cuts[k + 1]] for k in range(4)],
             cw=w['lru_conv_w'], cb=w['lru_conv_b'][None, :],
             wa=w['lru_wa'], wx=w['lru_wx'],
             ba=w['lru_ba'], bx=w['lru_bx'], lam=w['lru_lambda'],
             wlo=w['lru_w_out'], scw=w['sc_conv_w'], wsc=w['sc_w_out'],
             lg=w['sgu_ln_g'][None, :], lb=w['sgu_ln_b'][None, :],
             ws=w['sgu_w_s'].astype(_MM), wst=jnp.swapaxes(w['sgu_w_s'], 1, 2).astype(_MM),
             bias=jnp.repeat(w['sgu_b'].T, gw // _HEADS, axis=1),
             wsg=w['sgu_w_out'], wo=w['w_o'],
             pre_g=w['mix_pre_g'][None, :], post_g=w['mix_post_g'][None, :])
    return p


def _mixer_forward(tag, x, p, comm=None):
    t = x.shape[0]
    tl = _tm(t, 256)
    hm, zg, zx, zmid, zm, *carried = _mix_in(tag + "_in", x, p['pre_g'], p['win'], _tm(t, 512), comm=comm)
    lru = lambda d: (p['cw'], p['cb'], p['wa'][d], p['wx'][d], p['ba'][d:d + 1], p['bx'][d:d + 1],
                     p['lam'][d:d + 1])
    hf, = _lru_fwd(tag + "_lru_f", zx, *lru(0), tl, False)
    hb, pa = _lru_fwd(tag + "_lru_b", zx, *lru(1), tl, True, hf=hf, zg=zg)
    pb, pc = _bc_fwd(tag + "_bc", zmid, p['scw'], p['lg'], p['lb'], p['ws'], p['bias'], p['sw'], p['gw'], tl)
    ya, yb, yc, m = _mix_proj(tag + "_proj", pa, pb, pc, zm, p['wlo'], p['wsc'], p['wsg'], _tm(t, 512))
    mo, x_out = _proj_norm_res(tag + "_out", m, x, p['post_g'], p['wo'], 1.0, _tm(t, 512))
    sv = dict(x=x, hm=hm, zg=zg, zx=zx, zmid=zmid, zm=zm, hf=hf, hb=hb, pa=pa, pb=pb, pc=pc,
              ya=ya, yb=yb, yc=yc, m=m, mo=mo)
    return x_out, sv, carried


def _mixer_backward(tag, dxo, sv, p, comm=None):
    t = dxo.shape[0]
    tl = _tm(t, 256)
    tk = _tm(t, 512)
    dmo, dya, dyb, dyc, dzm, dpost = _mix_bwd_out(tag + "_bwd_out", dxo, sv['mo'], sv['ya'], sv['yb'], sv['yc'],
                                                  sv['zm'], p['post_g'], p['wo'], _tm(t, 512))
    dpa, dpb, dpc = _mix_bwd_proj(tag + "_bwd_proj", dya, dyb, dyc, p['wlo'], p['wsc'], p['wsg'], _tm(t, 512))
    dzmid, dscw, dlg, dlb, dws, dbias = _bc_bwd(tag + "_bc_bwd", sv['zmid'], dpb, dpc, p['scw'], p['lg'], p['lb'],
                                                p['ws'], p['wst'], p['bias'], p['sw'], p['gw'], tl)
    lru = lambda d: (p['cw'], p['cb'], p['wa'][d], p['wx'][d], p['ba'][d:d + 1], p['bx'][d:d + 1],
                     p['lam'][d:d + 1])
    dxc0, dzg, d_h, dwa0, dwx0, dba0, dbx0, dlam0, *carried = _lru_bwd(
        tag + "_lru_bwd_f", sv['zx'], dpa, sv['hf'], *lru(0), tl, 0, zg=sv['zg'], h_other=sv['hb'], comm=comm)
    dxc, dwa1, dwx1, dba1, dbx1, dlam1 = _lru_bwd(tag + "_lru_bwd_b", sv['zx'], d_h, sv['hb'], *lru(1), tl, 1,
                                                  dxc_in=dxc0)
    dzx, dcw, dcb = _lru_conv_bwd(tag + "_conv_bwd", dxc, sv['zx'], p['cw'], tl)
    dzs = [dzg, dzx, dzmid, dzm]
    dx, dpre = _bwd_in_norm(tag + "_bwd_in", dzs, p['win'], sv['x'], dxo, p['pre_g'], _tm(t, 256))
    gh = p['gw'] // _HEADS
    grads = dict(
        mix_pre_g=dpre[0], mix_post_g=dpost[0],
        w_in=jnp.concatenate([_xty(tag + "_dwin%d" % k, sv['hm'], dz, tk) for k, dz in enumerate(dzs)], axis=1),
        lru_conv_w=dcw, lru_conv_b=dcb[0],
        lru_wa=jnp.stack([dwa0, dwa1]), lru_wx=jnp.stack([dwx0, dwx1]),
        lru_ba=jnp.concatenate([dba0, dba1]), lru_bx=jnp.concatenate([dbx0, dbx1]),
        lru_lambda=jnp.concatenate([dlam0, dlam1]),
        lru_w_out=_xty(tag + "_dwlo", sv['pa'], dya, tk),
        sc_conv_w=dscw, sc_w_out=_xty(tag + "_dwsc", sv['pb'], dyb, tk),
        sgu_ln_g=dlg[0], sgu_ln_b=dlb[0], sgu_w_s=dws,
        sgu_b=jnp.sum(dbias.reshape(_CHUNK, _HEADS, gh), axis=2).T,
        sgu_w_out=_xty(tag + "_dwsg", sv['pc'], dyc, tk),
        w_o=_xty(tag + "_dwo", sv['m'], dmo, tk))
    return dx, grads, carried


def _forward_backward(x, target, depth, gathered, weights_of, gather_of, blocks_of, core):
    t = x.shape[0]
    saved = []
    for l in range(depth):
        w = weights_of(l, gathered)
        g = lambda nme: w[nme][None, :]
        tag = "l%d_" % l
        x, s1 = _ffn_forward(tag + "ffn1", x, g('ffn1_pre_g'), w['ffn1_w_gate'], w['ffn1_w_up'], w['ffn1_w_down'],
                             g('ffn1_post_g'))
        p = _mixer_weights(w)
        x, sm, carried = _mixer_forward(tag + "mix", x, p, comm=gather_of(l + 1) if l + 1 < depth else None)
        gathered = carried[0] if carried else None
        x, s2 = _ffn_forward(tag + "ffn2", x, g('ffn2_pre_g'), w['ffn2_w_gate'], w['ffn2_w_up'], w['ffn2_w_down'],
                             g('ffn2_post_g'))
        saved.append((s1, sm, s2, p, w))
    loss, dx = _loss_grad("loss", x, target, _tm(t, 512))
    per_layer, reduced, above = [], [], None
    for l in reversed(range(depth)):
        s1, sm, s2, p, w = saved[l]
        g = lambda nme: w[nme][None, :]
        tag = "l%d_" % l
        grads = {}
        dx, g2, got = _ffn_backward(tag + "ffn2", dx, s2, g('ffn2_pre_g'), w['ffn2_w_gate'], w['ffn2_w_up'],
                                    w['ffn2_w_down'], g('ffn2_post_g'),
                                    comm=_sibling_exchange_of(above) if above is not None else None)
        grads.update({'ffn2_' + k: v for k, v in g2.items()})
        pair = _pair_sum(tag + "reduce_pair_sum", above, got[0], core) if above is not None else None
        dx, gm, others = _mixer_backward(tag + "mix", dx, sm, p,
                                         comm=_chip_exchange_of(pair) if pair is not None else None)
        grads.update(gm)
        if pair is not None:
            reduced.append((pair, others[0]))
        dx, g1, _ = _ffn_backward(tag + "ffn1", dx, s1, g('ffn1_pre_g'), w['ffn1_w_gate'], w['ffn1_w_up'],
                                  w['ffn1_w_down'], g('ffn1_post_g'))
        grads.update({'ffn1_' + k: v for k, v in g1.items()})
        per_layer.append(grads)
        above = blocks_of(grads)
    per_layer.reverse()
    reduced.reverse()
    return loss, dx, per_layer, reduced, above


def kernel(x, ffn1_pre_g, ffn1_w_gate, ffn1_w_up, ffn1_w_down, ffn1_post_g, mix_pre_g, w_in, lru_conv_w, lru_conv_b, lru_wa, lru_ba, lru_wx, lru_bx, lru_lambda, lru_w_out, sc_conv_w, sc_w_out, sgu_ln_g, sgu_ln_b, sgu_w_s, sgu_b, sgu_w_out, w_o, mix_post_g, ffn2_pre_g, ffn2_w_gate, ffn2_w_up, ffn2_w_down, ffn2_post_g, loss_target, m_ffn1_pre_g, m_ffn1_w_gate, m_ffn1_w_up, m_ffn1_w_down, m_ffn1_post_g, m_mix_pre_g, m_w_in, m_lru_conv_w, m_lru_conv_b, m_lru_wa, m_lru_ba, m_lru_wx, m_lru_bx, m_lru_lambda, m_lru_w_out, m_sc_conv_w, m_sc_w_out, m_sgu_ln_g, m_sgu_ln_b, m_sgu_w_s, m_sgu_b, m_sgu_w_out, m_w_o, m_mix_post_g, m_ffn2_pre_g, m_ffn2_w_gate, m_ffn2_w_up, m_ffn2_w_down, m_ffn2_post_g, v_ffn1_pre_g, v_ffn1_w_gate, v_ffn1_w_up, v_ffn1_w_down, v_ffn1_post_g, v_mix_pre_g, v_w_in, v_lru_conv_w, v_lru_conv_b, v_lru_wa, v_lru_ba, v_lru_wx, v_lru_bx, v_lru_lambda, v_lru_w_out, v_sc_conv_w, v_sc_w_out, v_sgu_ln_g, v_sgu_ln_b, v_sgu_w_s, v_sgu_b, v_sgu_w_out, v_w_o, v_mix_post_g, v_ffn2_pre_g, v_ffn2_w_gate, v_ffn2_w_up, v_ffn2_w_down, v_ffn2_post_g):
    args = locals()
    wts = {n: args[n] for n in _WEIGHTS}
    mom = {n: args['m_' + n] for n in _WEIGHTS}
    var = {n: args['v_' + n] for n in _WEIGHTS}
    cx, cy, cc = _place()
    dev = 4 * cx + 2 * cy + cc
    big, small = list(_BIG), list(_SMALL_SHARDED)

    depth = w_in.shape[0]
    core = jnp.reshape(cc, (1,)).astype(jnp.int32)
    chip = jnp.reshape(2 * cx + cy, (1,)).astype(jnp.int32)
    layer_shapes = [wts[n].shape[1:] for n in big]
    rows = _flat_rows(sum(math.prod(s) for s in layer_shapes))

    g_small = _all_gather("gather_vectors", _pack([wts[n] for n in small], _F32))
    vecs = dict(wts)
    vecs.update(_gather_full(g_small, small, [wts[n].shape for n in small], [_SMALL_SHARDED[n] for n in small],
                             _unpack))

    def layer_block(l):
        return _pack_rows([wts[n][l] for n in big], _MM)

    def weights_of(l, gathered):
        w = {n: vecs[n][l] for n in _WEIGHTS if n not in _BIG}
        w.update(_gather_full(gathered, big, layer_shapes, [_BIG[n] - 1 for n in big], _unpack_rows))
        return w

    def blocks_of(grads):
        pieces = []
        for d in range(_NDEV):
            blocks = [_block_rows(grads[n], _BIG[n] - 1, d) for n in big]
            fill = rows - sum(b.shape[0] for b in blocks)
            pieces += blocks + ([jnp.zeros((fill, _LANES), _F32)] if fill else [])
        return jnp.concatenate(pieces, axis=0).reshape(4, 2, rows, _LANES)

    loss, grad_x, grads, reduced, blocks0 = _forward_backward(
        x[0], loss_target[0], depth, _all_gather("l0_gather_matrices", layer_block(0)), weights_of,
        lambda l: _gather_exchange(layer_block(l)), blocks_of, core)
    loss = lax.psum(loss, ("x", "y", "c"))

    got0, = _run_exchange("l0_reduce_sibling", _sibling_exchange_of(blocks0))
    pair0 = _pair_sum("l0_reduce_pair_sum", blocks0, got0, core)
    others0, = _run_exchange("l0_reduce_chips", _chip_exchange_of(pair0))
    per_layer = [_unpack_rows(_chip_sum("l%d_reduce_final_sum" % l, pair, others, chip), layer_shapes)
                 for l, (pair, others) in enumerate([(pair0, others0)] + reduced)]
    out = {}
    for i, n in enumerate(big):
        g = jnp.stack([per_layer[l][i] for l in range(depth)])
        out['grad_' + n] = g
        out['delta_' + n], out['new_m_' + n], out['new_v_' + n] = _adamw_update("update_" + n, wts[n], g, mom[n],
                                                                                 var[n])

    vec = _REPLICATED + small
    gvec = {n: jnp.stack([g[n] for g in grads]) for n in vec}
    part = _pack([gvec[n] for n in vec], _F32)
    allp = _all_gather("gather_vector_grads", part)
    rv = part.shape[0]
    tmv = _flat_tm(rv)
    gsum = _sum_parts("reduce_vector_grads", [_rows3(allp, k, tmv) for k in range(_NDEV)], rv, _LANES, _F32)
    gfull = dict(zip(vec, _unpack(gsum, [gvec[n].shape for n in vec])))
    gloc = []
    for n in vec:
        if n in _SMALL_SHARDED:
            ax = _SMALL_SHARDED[n]
            sz = wts[n].shape[ax]
            gloc.append(lax.dynamic_slice_in_dim(gfull[n], dev * sz, sz, axis=ax))
        else:
            gloc.append(gfull[n])
    gl = _pack(gloc, _F32)
    g_s, d_s, m_s, v_s = _sum_adamw("update_vectors", [_rows(gl, _flat_tm(gl.shape[0]))],
                                    _pack([wts[n] for n in vec], _F32), _pack([mom[n] for n in vec], _F32),
                                    _pack([var[n] for n in vec], _F32))
    shapes_s = [wts[n].shape for n in vec]
    for key, flat in (('grad_', g_s), ('delta_', d_s), ('new_m_', m_s), ('new_v_', v_s)):
        for n, a in zip(vec, _unpack(flat, shapes_s)):
            out[key + n] = a

    res = [loss, grad_x[None]]
    for key in ('grad_', 'delta_', 'new_m_', 'new_v_'):
        res += [out[key + n] for n in _WEIGHTS]
    return tuple(res)
```

```python
import functools
import math

import jax
import jax.numpy as jnp
from jax import lax
from jax.experimental import pallas as pl
from jax.experimental.pallas import tpu as pltpu

_F32 = jnp.float32
_MM = jnp.bfloat16
_EPS = 1e-6
_HEADS = 4
_CHUNK = 128
_LRU_C = 8.0
_HALO = 16
_LANES = 1024
_NDEV = 8
_VMEM_LIMIT = 56 * 1024 * 1024
_GELU_K = math.sqrt(2.0 / math.pi)
_GELU_C = 0.044715
_MESH = pl.DeviceIdType.MESH

_ADAM_LR, _ADAM_B1, _ADAM_B2, _ADAM_EPS, _ADAM_WD, _ADAM_STEP = 1e-3, 0.9, 0.999, 1e-8, 0.01, 10

_WEIGHTS = ['ffn1_pre_g', 'ffn1_w_gate', 'ffn1_w_up', 'ffn1_w_down', 'ffn1_post_g', 'mix_pre_g', 'w_in',
            'lru_conv_w', 'lru_conv_b', 'lru_wa', 'lru_ba', 'lru_wx', 'lru_bx', 'lru_lambda', 'lru_w_out',
            'sc_conv_w', 'sc_w_out', 'sgu_ln_g', 'sgu_ln_b', 'sgu_w_s', 'sgu_b', 'sgu_w_out', 'w_o',
            'mix_post_g', 'ffn2_pre_g', 'ffn2_w_gate', 'ffn2_w_up', 'ffn2_w_down', 'ffn2_post_g']
_BIG = {'ffn1_w_gate': 2, 'ffn1_w_up': 2, 'ffn1_w_down': 1, 'w_in': 2, 'lru_wa': 3, 'lru_wx': 3,
        'lru_w_out': 1, 'sc_w_out': 2, 'sgu_w_out': 2, 'w_o': 1,
        'ffn2_w_gate': 2, 'ffn2_w_up': 2, 'ffn2_w_down': 1}
_SMALL_SHARDED = {'lru_conv_w': 2, 'lru_ba': 2, 'lru_bx': 2, 'lru_lambda': 2, 'sc_conv_w': 2}
_REPLICATED = ['ffn1_pre_g', 'ffn1_post_g', 'mix_pre_g', 'lru_conv_b', 'sgu_ln_g', 'sgu_ln_b', 'sgu_w_s',
               'sgu_b', 'mix_post_g', 'ffn2_pre_g', 'ffn2_post_g']


def _dot(a, b):
    return jnp.dot(a, b, preferred_element_type=_F32)


def _dot_nt(a, b):
    return lax.dot_general(a, b, (((1,), (1,)), ((), ())), preferred_element_type=_F32)


def _dot_tn(a, b):
    return lax.dot_general(a, b, (((0,), (0,)), ((), ())), preferred_element_type=_F32)


def _sigmoid(x):
    return 0.5 * jnp.tanh(0.5 * x) + 0.5


def _gelu(x):
    t = jnp.tanh(x * (_GELU_K + (_GELU_K * _GELU_C) * (x * x)))
    return (0.5 * x) * (1.0 + t)


def _gelu_and_grad(x):
    x2 = x * x
    t = jnp.tanh(x * (_GELU_K + (_GELU_K * _GELU_C) * x2))
    hx = 0.5 * x
    return hx * (1.0 + t), 0.5 * (1.0 + t) + hx * (1.0 - t * t) * (_GELU_K + (3.0 * _GELU_K * _GELU_C) * x2)


def _gelu_grad(x):
    return _gelu_and_grad(x)[1]


def _rms_fwd(x, g):
    r = lax.rsqrt(jnp.mean(x * x, axis=-1, keepdims=True) + _EPS)
    return x * r * g


def _rms_bwd(dy, x, g):
    r = lax.rsqrt(jnp.mean(x * x, axis=-1, keepdims=True) + _EPS)
    xh = x * r
    dxh = dy * g
    dx = r * (dxh - xh * jnp.mean(dxh * xh, axis=-1, keepdims=True))
    return dx, jnp.sum(dy * xh, axis=0, keepdims=True)


def _neg_softplus_neg(lam):
    e = jnp.exp(-jnp.abs(lam))
    l1p = jnp.where(e < 1e-2, e * (1.0 - e * (0.5 - e * (1.0 / 3.0 - 0.25 * e))), jnp.log(1.0 + e))
    return -_LRU_C * (jnp.maximum(-lam, 0.0) + l1p)


def _shift_rows(xe, d, tm):
    n = xe.shape[0]
    if d == 0:
        return xe[_HALO:_HALO + tm]
    return pltpu.roll(xe, (-d) % n, axis=0)[_HALO:_HALO + tm]


def _with_halo(cur, prev, nxt, first, last):
    p = jnp.where(first, 0.0, prev.astype(_F32))
    n = jnp.where(last, 0.0, nxt.astype(_F32))
    return jnp.concatenate([p, cur.astype(_F32), n], axis=0)


def _rows(arr, tm):
    c = arr.shape[1]
    return (arr, (tm, c), lambda ti: (ti, 0))


def _rows3(arr, k, tm):
    c = arr.shape[2]
    return (arr, (None, tm, c), lambda ti, k=k: (k, ti, 0))


def _halo_prev(arr, tm):
    c = arr.shape[1]
    return (arr, (_HALO, c), lambda ti: (jnp.maximum(ti * (tm // _HALO) - 1, 0), 0))


def _halo_next(arr, tm):
    c = arr.shape[1]
    nblk = arr.shape[0] // _HALO
    return (arr, (_HALO, c), lambda ti: (jnp.minimum((ti + 1) * (tm // _HALO), nblk - 1), 0))


def _full(arr):
    nd = arr.ndim
    return (arr, arr.shape, lambda ti, nd=nd: (0,) * nd)


def _out_rows(t, c, dtype, tm):
    return ((t, c), dtype, (tm, c), lambda ti: (ti, 0))


def _row_call(name, body, n_tiles, ins, outs, accs=(), hbm=(), scratch=(), reverse=False, comm=None):
    n_in, n_hbm, n_out, n_acc = len(ins), len(hbm), len(outs), len(accs)
    c_ins = list(comm.ins) if comm else []
    c_outs = list(comm.outs) if comm else []
    c_sems = list(comm.sems) if comm else []

    def tile_of(step):
        return (n_tiles - 1 - step) if reverse else step

    def spec(block, index_fn):
        return pl.BlockSpec(block, lambda s, f=index_fn: f(tile_of(s)))

    def kern(*refs):
        in_refs = refs[:n_in]
        hbm_refs = refs[n_in:n_in + n_hbm]
        o0 = n_in + n_hbm + len(c_ins)
        cin_refs = refs[n_in + n_hbm:o0]
        out_refs = refs[o0:o0 + n_out]
        acc_refs = refs[o0 + n_out:o0 + n_out + n_acc]
        s0 = o0 + n_out + n_acc + len(c_outs)
        cout_refs = refs[o0 + n_out + n_acc:s0]
        w_refs = refs[s0:s0 + n_hbm]
        csem_refs = refs[s0 + n_hbm:s0 + n_hbm + len(c_sems)]
        scr = refs[s0 + n_hbm + len(c_sems):]
        step = pl.program_id(0)

        @pl.when(step == 0)
        def _():
            if comm:
                comm.start(cin_refs, cout_refs, csem_refs)
            for src, dst in zip(hbm_refs, w_refs):
                pltpu.sync_copy(src, dst)
            for a in acc_refs:
                a[...] = jnp.zeros(a.shape, a.dtype)

        body(step, tile_of(step), in_refs, w_refs, out_refs, acc_refs, scr)

        if comm:
            @pl.when(step == n_tiles - 1)
            def _():
                comm.finish(cin_refs, cout_refs, csem_refs)

    any_spec = pl.BlockSpec(memory_space=pl.ANY)
    in_specs = [spec(b, f) for (_, b, f) in ins] + [any_spec] * (n_hbm + len(c_ins))
    out_specs = [spec(b, f) for (_, _, b, f) in outs]
    out_specs += [pl.BlockSpec(s, lambda st, nd=len(s): (0,) * nd) for s in accs] + [any_spec] * len(c_outs)
    out_shape = [jax.ShapeDtypeStruct(s, d) for (s, d, _, _) in outs]
    out_shape += [jax.ShapeDtypeStruct(s, _F32) for s in accs] + c_outs
    scratch_shapes = [pltpu.VMEM(w.shape, w.dtype) for w in hbm] + c_sems + list(scratch)
    res = pl.pallas_call(
        kern, name=name, grid=(n_tiles,), in_specs=in_specs, out_specs=out_specs, out_shape=out_shape,
        scratch_shapes=scratch_shapes,
        compiler_params=pltpu.CompilerParams(dimension_semantics=("arbitrary",), vmem_limit_bytes=_VMEM_LIMIT),
    )(*[a for (a, _, _) in ins], *hbm, *c_ins)
    return list(res)


def _xty(name, x, y, tk):
    t, k1 = x.shape
    k2 = y.shape[1]

    def kern(x_ref, y_ref, o_ref):
        @pl.when(pl.program_id(0) == 0)
        def _():
            o_ref[...] = jnp.zeros(o_ref.shape, o_ref.dtype)

        o_ref[...] += _dot_tn(x_ref[...], y_ref[...])

    return pl.pallas_call(
        kern, name=name, grid=(t // tk,),
        in_specs=[pl.BlockSpec((tk, k1), lambda k: (k, 0)), pl.BlockSpec((tk, k2), lambda k: (k, 0))],
        out_specs=pl.BlockSpec((k1, k2), lambda k: (0, 0)),
        out_shape=jax.ShapeDtypeStruct((k1, k2), _F32),
        compiler_params=pltpu.CompilerParams(dimension_semantics=("arbitrary",), vmem_limit_bytes=_VMEM_LIMIT),
    )(x, y)


def _ffn_up(name, x, pre_g, wg, wu, tm):
    t, d = x.shape
    f = wg.shape[1]

    def body(step, ti, ins, ws, outs, accs, scr):
        x_ref, g_ref = ins
        h = _rms_fwd(x_ref[...], g_ref[...]).astype(_MM)
        a = _dot(h, ws[0][...])
        b = _dot(h, ws[1][...])
        outs[0][...] = h
        outs[1][...] = a.astype(_MM)
        outs[2][...] = b.astype(_MM)
        outs[3][...] = (a * _sigmoid(a) * b).astype(_MM)

    return _row_call(name, body, t // tm, [_rows(x, tm), _full(pre_g)],
                     [_out_rows(t, d, _MM, tm), _out_rows(t, f, _MM, tm), _out_rows(t, f, _MM, tm),
                      _out_rows(t, f, _MM, tm)], hbm=[wg, wu])


def _proj_norm_res(name, lhs, x, post_g, w, scale, tm):
    t, d = x.shape

    def body(step, ti, ins, ws, outs, accs, scr):
        l_ref, x_ref, g_ref = ins
        f = _dot(l_ref[...], ws[0][...])
        outs[0][...] = f
        outs[1][...] = x_ref[...] + scale * _rms_fwd(f, g_ref[...])

    return _row_call(name, body, t // tm, [_rows(lhs, tm), _rows(x, tm), _full(post_g)],
                     [_out_rows(t, d, _F32, tm), _out_rows(t, d, _F32, tm)], hbm=[w])


def _ffn_bwd_post(name, dxo, f, a, b, post_g, wd, scale, tm, comm=None):
    t, d = dxo.shape
    ff = a.shape[1]

    def body(step, ti, ins, ws, outs, accs, scr):
        dxo_ref, f_ref, a_ref, b_ref, g_ref = ins
        df, dg = _rms_bwd(scale * dxo_ref[...], f_ref[...], g_ref[...])
        accs[0][...] += dg
        dfb = df.astype(_MM)
        ds = _dot_nt(dfb, ws[0][...])
        a32 = a_ref[...].astype(_F32)
        b32 = b_ref[...].astype(_F32)
        sg = _sigmoid(a32)
        outs[0][...] = (ds * b32 * (sg * (1.0 + a32 * (1.0 - sg)))).astype(_MM)
        outs[1][...] = (ds * (a32 * sg)).astype(_MM)
        outs[2][...] = dfb

    return _row_call(name, body, t // tm,
                     [_rows(dxo, tm), _rows(f, tm), _rows(a, tm), _rows(b, tm), _full(post_g)],
                     [_out_rows(t, ff, _MM, tm), _out_rows(t, ff, _MM, tm), _out_rows(t, d, _MM, tm)],
                     accs=[(1, d)], hbm=[wd], comm=comm)


def _bwd_in_norm(name, dzs, ws_list, x, dxo, pre_g, tm):
    t, d = x.shape
    nz = len(dzs)

    def body(step, ti, ins, ws, outs, accs, scr):
        dh = _dot_nt(ins[0][...], ws[0][...])
        for k in range(1, nz):
            dh = dh + _dot_nt(ins[k][...], ws[k][...])
        x_ref, dxo_ref, g_ref = ins[nz:]
        dx, dg = _rms_bwd(dh, x_ref[...], g_ref[...])
        accs[0][...] += dg
        outs[0][...] = dxo_ref[...] + dx

    return _row_call(name, body, t // tm,
                     [_rows(z, tm) for z in dzs] + [_rows(x, tm), _rows(dxo, tm), _full(pre_g)],
                     [_out_rows(t, d, _F32, tm)], accs=[(1, d)], hbm=list(ws_list))


def _mix_in(name, x, pre_g, w_parts, tm, comm=None):
    t, d = x.shape

    def body(step, ti, ins, ws, outs, accs, scr):
        x_ref, g_ref = ins
        h = _rms_fwd(x_ref[...], g_ref[...]).astype(_MM)
        outs[0][...] = h
        for k in range(len(ws)):
            outs[1 + k][...] = _dot(h, ws[k][...]).astype(_MM)

    return _row_call(name, body, t // tm, [_rows(x, tm), _full(pre_g)],
                     [_out_rows(t, d, _MM, tm)] + [_out_rows(t, w.shape[1], _MM, tm) for w in w_parts],
                     hbm=list(w_parts), comm=comm)


def _lru_conv(xe, cw, cb, tm):
    xc = cb
    for k in range(4):
        xc = xc + _shift_rows(xe, k - 2, tm) * cw[k:k + 1, :]
    return xc


def _lru_gates(xc, wa_ref, wx_ref, ba, bx, c):
    dh = xc.shape[1] // _HEADS
    gas, gxs = [], []
    for hh in range(_HEADS):
        xs = xc[:, hh * dh:(hh + 1) * dh].astype(_MM)
        gas.append(_dot(xs, wa_ref[hh]))
        gxs.append(_dot(xs, wx_ref[hh]))
    r = _sigmoid(jnp.concatenate(gas, axis=1) + ba)
    i = _sigmoid(jnp.concatenate(gxs, axis=1) + bx)
    la = c * r
    a = jnp.exp(la)
    em = -jnp.tanh(la) * (a * a + 1.0)
    return r, i, a, em


def _scan_scratch(tm, w):
    return [pltpu.VMEM((tm, w), _F32), pltpu.VMEM((tm, w), _F32)]


def _tile_scan(a, u, scan_scr, h_dst, carry, tm, descending):
    a_scr, u_scr = scan_scr
    a_scr[...] = a
    u_scr[...] = u
    ng = tm // 8
    w = a.shape[1]
    row = lax.broadcasted_iota(jnp.int32, (8, w), 0)

    def grp(j, carry):
        g = (ng - 1 - j) if descending else j
        r0 = pl.multiple_of(g * 8, 8)
        a8 = a_scr[pl.ds(r0, 8), :]
        u8 = u_scr[pl.ds(r0, 8), :]
        for dd in (1, 2, 4):
            if descending:
                ok = row < 8 - dd
                sh = 8 - dd
            else:
                ok = row >= dd
                sh = dd
            a_s = jnp.where(ok, pltpu.roll(a8, sh, axis=0), 1.0)
            u_s = jnp.where(ok, pltpu.roll(u8, sh, axis=0), 0.0)
            u8 = a8 * u_s + u8
            a8 = a8 * a_s
        h8 = u8 + a8 * carry
        h_dst[pl.ds(r0, 8), :] = h8
        return h8[0:1, :] if descending else h8[7:8, :]

    return lax.fori_loop(0, ng, grp, carry, unroll=2)


def _lru_fwd(name, zx, cw, cb, wa, wx, ba, bx, lam, tm, descending, hf=None, zg=None):
    t, w = zx.shape
    n = t // tm

    def body(step, ti, ins, ws, outs, accs, scr):
        zc, zp, zn, cw_r, cb_r, wa_r, wx_r, ba_r, bx_r, lam_r = ins[:10]
        carry_scr = scr[2]
        xe = _with_halo(zc[...], zp[...], zn[...], ti == 0, ti == n - 1)
        xc = _lru_conv(xe, cw_r[...], cb_r[...], tm)
        c = _neg_softplus_neg(lam_r[...])
        r, i, a, em = _lru_gates(xc, wa_r, wx_r, ba_r[...], bx_r[...], c)

        @pl.when(step == 0)
        def _():
            carry_scr[...] = jnp.zeros(carry_scr.shape, _F32)

        carry_scr[...] = _tile_scan(a, i * xc * jnp.sqrt(em), scr[:2], outs[0], carry_scr[...], tm, descending)
        if descending:
            hf_r, zg_r = ins[10:]
            outs[1][...] = ((hf_r[...] + outs[0][...]) * _gelu(zg_r[...].astype(_F32))).astype(_MM)

    ins = [_rows(zx, tm), _halo_prev(zx, tm), _halo_next(zx, tm), _full(cw), _full(cb), _full(wa), _full(wx),
           _full(ba), _full(bx), _full(lam)]
    outs = [_out_rows(t, w, _F32, tm)]
    if descending:
        ins += [_rows(hf, tm), _rows(zg, tm)]
        outs += [_out_rows(t, w, _MM, tm)]
    scratch = _scan_scratch(tm, w) + [pltpu.VMEM((1, w), _F32)]
    return _row_call(name, body, n, ins, outs, scratch=scratch, reverse=descending)


def _lru_bwd(name, zx, d_in, h_own, cw, cb, wa, wx, ba, bx, lam, tm, direction, zg=None, h_other=None,
             dxc_in=None, comm=None):
    t, w = zx.shape
    n = t // tm
    dh_ = w // _HEADS
    adj_desc = direction == 0

    def body(step, ti, ins, ws, outs, accs, scr):
        zc, zp, zn, din_r, ho_r, hh_r, cw_r, cb_r, wa_r, wx_r, ba_r, bx_r, lam_r = ins[:13]
        p_scr, carry_scr = scr[2:]
        first, last = ti == 0, ti == n - 1
        xe = _with_halo(zc[...], zp[...], zn[...], first, last)
        xc = _lru_conv(xe, cw_r[...], cb_r[...], tm)
        lam_v = lam_r[...]
        c = _neg_softplus_neg(lam_v)
        r, i, a, em = _lru_gates(xc, wa_r, wx_r, ba_r[...], bx_r[...], c)
        m = jnp.sqrt(em)
        if direction == 0:
            gel, gel_grad = _gelu_and_grad(ins[13][...].astype(_F32))
            dpa_v = din_r[...]
            d_h = dpa_v * gel
        else:
            d_h = din_r[...]
        @pl.when(step == 0)
        def _():
            carry_scr[...] = jnp.zeros(carry_scr.shape, _F32)

        carry_in = carry_scr[...]
        carry_scr[...] = _tile_scan(a, a * d_h, scr[:2], p_scr, carry_in, tm, adj_desc)
        p = p_scr[...]
        row = lax.broadcasted_iota(jnp.int32, (tm, w), 0)
        h_t = ho_r[...]
        if adj_desc:
            p_nb = jnp.where(row == tm - 1, carry_in, pltpu.roll(p, tm - 1, axis=0))
            edge = jnp.where(first, 0.0, hh_r[_HALO - 1:_HALO, :])
            h_nb = jnp.where(row == 0, edge, pltpu.roll(h_t, 1, axis=0))
        else:
            p_nb = jnp.where(row == 0, carry_in, pltpu.roll(p, 1, axis=0))
            edge = jnp.where(last, 0.0, hh_r[0:1, :])
            h_nb = jnp.where(row == tm - 1, edge, pltpu.roll(h_t, tm - 1, axis=0))
        g = d_h + p_nb
        gi = g * i
        d_i = g * xc * m
        dxc = gi * m
        d_m = gi * xc
        d_l = g * h_nb * a - d_m * (1.0 - em) / m
        accs[4][...] += jnp.sum(d_l * r, axis=0, keepdims=True)
        dga = d_l * c * r * (1.0 - r)
        dgx = d_i * i * (1.0 - i)
        accs[2][...] += jnp.sum(dga, axis=0, keepdims=True)
        accs[3][...] += jnp.sum(dgx, axis=0, keepdims=True)
        parts = []
        for hh in range(_HEADS):
            sl = slice(hh * dh_, (hh + 1) * dh_)
            xs = xc[:, sl].astype(_MM)
            da_h = dga[:, sl].astype(_MM)
            dx_h = dgx[:, sl].astype(_MM)
            accs[0][hh] += _dot_tn(xs, da_h)
            accs[1][hh] += _dot_tn(xs, dx_h)
            parts.append(_dot_nt(da_h, wa_r[hh]) + _dot_nt(dx_h, wx_r[hh]))
        dxc = dxc + jnp.concatenate(parts, axis=1)
        if direction == 0:
            outs[0][...] = dxc
            outs[1][...] = (dpa_v * (h_t + ins[14][...]) * gel_grad).astype(_MM)
            outs[2][...] = d_h
        else:
            outs[0][...] = dxc + ins[13][...]

        @pl.when(step == n - 1)
        def _():
            accs[4][...] = accs[4][...] * (_LRU_C * _sigmoid(-lam_v))

    halo_h = _halo_prev(h_own, tm) if adj_desc else _halo_next(h_own, tm)
    ins = [_rows(zx, tm), _halo_prev(zx, tm), _halo_next(zx, tm), _rows(d_in, tm), _rows(h_own, tm), halo_h,
           _full(cw), _full(cb), _full(wa), _full(wx), _full(ba), _full(bx), _full(lam)]
    outs = [_out_rows(t, w, _F32, tm)]
    if direction == 0:
        ins += [_rows(zg, tm), _rows(h_other, tm)]
        outs += [_out_rows(t, w, _MM, tm), _out_rows(t, w, _F32, tm)]
    else:
        ins += [_rows(dxc_in, tm)]
    accs = [(_HEADS, dh_, dh_), (_HEADS, dh_, dh_), (1, w), (1, w), (1, w)]
    scratch = _scan_scratch(tm, w) + [pltpu.VMEM((tm, w), _F32), pltpu.VMEM((1, w), _F32)]
    return _row_call(name, body, n, ins, outs, accs=accs, scratch=scratch, reverse=adj_desc, comm=comm)


def _lru_conv_bwd(name, dxc, zx, cw, tm):
    t, w = zx.shape
    n = t // tm

    def body(step, ti, ins, ws, outs, accs, scr):
        dc, dp, dn, zc, zp, zn, cw_r = ins
        first, last = ti == 0, ti == n - 1
        de = _with_halo(dc[...], dp[...], dn[...], first, last)
        ze = _with_halo(zc[...], zp[...], zn[...], first, last)
        cw_v = cw_r[...]
        d_cur = dc[...]
        dz = None
        for k in range(4):
            term = _shift_rows(de, 2 - k, tm) * cw_v[k:k + 1, :]
            dz = term if dz is None else dz + term
            accs[0][k:k + 1, :] += jnp.sum(d_cur * _shift_rows(ze, k - 2, tm), axis=0, keepdims=True)
        accs[1][...] += jnp.sum(d_cur, axis=0, keepdims=True)
        outs[0][...] = dz.astype(_MM)

    ins = [_rows(dxc, tm), _halo_prev(dxc, tm), _halo_next(dxc, tm),
           _rows(zx, tm), _halo_prev(zx, tm), _halo_next(zx, tm), _full(cw)]
    return _row_call(name, body, n, ins, [_out_rows(t, w, _MM, tm)], accs=[(4, w), (1, w)])


def _sgu_mix(v2, ws_ref, bias, mixed_scr, tm):
    gw = v2.shape[1]
    gh = gw // _HEADS
    for nn in range(tm // _CHUNK):
        rs = slice(nn * _CHUNK, (nn + 1) * _CHUNK)
        for g in range(_HEADS):
            cs = slice(g * gh, (g + 1) * gh)
            mixed_scr[rs, cs] = _dot(ws_ref[g], v2[rs, cs].astype(_MM)) + bias[:, cs]
    return mixed_scr[...]


def _ln_fwd(v1, lg, lb):
    mu = jnp.mean(v1, axis=-1, keepdims=True)
    vc = v1 - mu
    rs = lax.rsqrt(jnp.mean(vc * vc, axis=-1, keepdims=True) + _EPS)
    vn = vc * rs
    return vn * lg + lb, vn, rs


def _bc_fwd(name, zmid, scw, lg, lb, ws_mm, bias, sw, gw, tm):
    t = zmid.shape[0]
    n = t // tm

    def body(step, ti, ins, ws, outs, accs, scr):
        zc, zp, zn, scw_r, lg_r, lb_r, ws_r, bias_r = ins
        first, last = ti == 0, ti == n - 1
        z = zc[...].astype(_F32)
        zb, zcc, zxx = z[:, 0:sw], z[:, sw:2 * sw], z[:, 2 * sw:3 * sw]
        zu, zv = z[:, 3 * sw:3 * sw + gw], z[:, 3 * sw + gw:3 * sw + 2 * gw]
        zpv, znv = zp[...].astype(_F32), zn[...].astype(_F32)
        qe = _with_halo(zcc * zxx, zpv[:, sw:2 * sw] * zpv[:, 2 * sw:3 * sw],
                        znv[:, sw:2 * sw] * znv[:, 2 * sw:3 * sw], first, last)
        scw_v = scw_r[...]
        cq = None
        for k in range(3):
            term = _shift_rows(qe, k - 1, tm) * scw_v[k:k + 1, :]
            cq = term if cq is None else cq + term
        outs[0][...] = (zb * cq).astype(_MM)
        v2, _, _ = _ln_fwd(_gelu(zv), lg_r[...], lb_r[...])
        mixed = _sgu_mix(v2, ws_r, bias_r[...], scr[0], tm)
        outs[1][...] = (_gelu(zu) * mixed).astype(_MM)

    ins = [_rows(zmid, tm), _halo_prev(zmid, tm), _halo_next(zmid, tm), _full(scw), _full(lg), _full(lb),
           _full(ws_mm), _full(bias)]
    return _row_call(name, body, n, ins, [_out_rows(t, sw, _MM, tm), _out_rows(t, gw, _MM, tm)],
                     scratch=[pltpu.VMEM((tm, gw), _F32)])


def _bc_bwd(name, zmid, dpb, dpc, scw, lg, lb, ws_mm, wst_mm, bias, sw, gw, tm):
    t = zmid.shape[0]
    n = t // tm
    gh = gw // _HEADS

    def body(step, ti, ins, ws, outs, accs, scr):
        zc, zp, zn, db_c, db_p, db_n, dc_r, scw_r, lg_r, lb_r, ws_r, wst_r, bias_r = ins
        mixed_scr, dv2_scr = scr
        first, last = ti == 0, ti == n - 1
        z = zc[...].astype(_F32)
        zb, zcc, zxx = z[:, 0:sw], z[:, sw:2 * sw], z[:, 2 * sw:3 * sw]
        zu, zv = z[:, 3 * sw:3 * sw + gw], z[:, 3 * sw + gw:3 * sw + 2 * gw]
        zpv, znv = zp[...].astype(_F32), zn[...].astype(_F32)
        qe = _with_halo(zcc * zxx, zpv[:, sw:2 * sw] * zpv[:, 2 * sw:3 * sw],
                        znv[:, sw:2 * sw] * znv[:, 2 * sw:3 * sw], first, last)
        dpb_v = db_c[...]
        dcq = dpb_v * zb
        dcqe = _with_halo(dcq, db_p[...] * zpv[:, 0:sw], db_n[...] * znv[:, 0:sw], first, last)
        scw_v = scw_r[...]
        cq, dq = None, None
        for k in range(3):
            qk = _shift_rows(qe, k - 1, tm)
            term = qk * scw_v[k:k + 1, :]
            cq = term if cq is None else cq + term
            dterm = _shift_rows(dcqe, 1 - k, tm) * scw_v[k:k + 1, :]
            dq = dterm if dq is None else dq + dterm
            accs[0][k:k + 1, :] += jnp.sum(dcq * qk, axis=0, keepdims=True)
        outs[0][:, 0:sw] = (dpb_v * cq).astype(_MM)
        outs[0][:, sw:2 * sw] = (dq * zxx).astype(_MM)
        outs[0][:, 2 * sw:3 * sw] = (dq * zcc).astype(_MM)
        lg_v = lg_r[...]
        v2, vn, rs = _ln_fwd(_gelu(zv), lg_v, lb_r[...])
        mixed = _sgu_mix(v2, ws_r, bias_r[...], mixed_scr, tm)
        dpc_v = dc_r[...]
        outs[0][:, 3 * sw:3 * sw + gw] = (dpc_v * mixed * _gelu_grad(zu)).astype(_MM)
        dmix = dpc_v * _gelu(zu)
        for nn in range(tm // _CHUNK):
            rsl = slice(nn * _CHUNK, (nn + 1) * _CHUNK)
            accs[4][...] += dmix[rsl, :]
            for g in range(_HEADS):
                cs = slice(g * gh, (g + 1) * gh)
                dm_b = dmix[rsl, cs].astype(_MM)
                accs[3][g] += _dot_nt(dm_b, v2[rsl, cs].astype(_MM))
                dv2_scr[rsl, cs] = _dot(wst_r[g], dm_b)
        dv2 = dv2_scr[...]
        accs[1][...] += jnp.sum(dv2 * vn, axis=0, keepdims=True)
        accs[2][...] += jnp.sum(dv2, axis=0, keepdims=True)
        dvn = dv2 * lg_v
        dv1 = rs * (dvn - jnp.mean(dvn, axis=-1, keepdims=True)
                    - vn * jnp.mean(dvn * vn, axis=-1, keepdims=True))
        outs[0][:, 3 * sw + gw:3 * sw + 2 * gw] = (dv1 * _gelu_grad(zv)).astype(_MM)

    ins = [_rows(zmid, tm), _halo_prev(zmid, tm), _halo_next(zmid, tm),
           _rows(dpb, tm), _halo_prev(dpb, tm), _halo_next(dpb, tm), _rows(dpc, tm),
           _full(scw), _full(lg), _full(lb), _full(ws_mm), _full(wst_mm), _full(bias)]
    accs = [(3, sw), (1, gw), (1, gw), (_HEADS, _CHUNK, _CHUNK), (_CHUNK, gw)]
    return _row_call(name, body, n, ins, [_out_rows(t, 3 * sw + 2 * gw, _MM, tm)], accs=accs,
                     scratch=[pltpu.VMEM((tm, gw), _F32), pltpu.VMEM((tm, gw), _F32)])


def _mix_proj(name, pa, pb, pc, zm, wlo, wsc, wsg, tm):
    t = pa.shape[0]
    d = wlo.shape[1]

    def body(step, ti, ins, ws, outs, accs, scr):
        ys = [_dot(ins[k][...], ws[k][...]) for k in range(3)]
        gm = _sigmoid(ins[3][...].astype(_F32))
        m = None
        for k in range(3):
            outs[k][...] = ys[k].astype(_MM)
            term = gm[:, k * d:(k + 1) * d] * ys[k]
            m = term if m is None else m + term
        outs[3][...] = m.astype(_MM)

    return _row_call(name, body, t // tm, [_rows(pa, tm), _rows(pb, tm), _rows(pc, tm), _rows(zm, tm)],
                     [_out_rows(t, d, _MM, tm)] * 4, hbm=[wlo, wsc, wsg])


def _mix_bwd_out(name, dxo, mo, ya, yb, yc, zm, post_g, wo, tm):
    t, d = dxo.shape

    def body(step, ti, ins, ws, outs, accs, scr):
        dxo_ref, mo_ref, ya_r, yb_r, yc_r, zm_r, g_ref = ins
        dmo, dg = _rms_bwd(dxo_ref[...], mo_ref[...], g_ref[...])
        accs[0][...] += dg
        dmob = dmo.astype(_MM)
        outs[0][...] = dmob
        dm = _dot_nt(dmob, ws[0][...])
        gm = _sigmoid(zm_r[...].astype(_F32))
        for k, y_r in enumerate((ya_r, yb_r, yc_r)):
            gk = gm[:, k * d:(k + 1) * d]
            outs[1 + k][...] = (dm * gk).astype(_MM)
            outs[4][:, k * d:(k + 1) * d] = (dm * y_r[...].astype(_F32) * gk * (1.0 - gk)).astype(_MM)

    ins = [_rows(dxo, tm), _rows(mo, tm), _rows(ya, tm), _rows(yb, tm), _rows(yc, tm), _rows(zm, tm),
           _full(post_g)]
    return _row_call(name, body, t // tm, ins,
                     [_out_rows(t, d, _MM, tm)] * 4 + [_out_rows(t, 3 * d, _MM, tm)], accs=[(1, d)], hbm=[wo])


def _mix_bwd_proj(name, dya, dyb, dyc, wlo, wsc, wsg, tm):
    t = dya.shape[0]

    def body(step, ti, ins, ws, outs, accs, scr):
        for k in range(3):
            outs[k][...] = _dot_nt(ins[k][...], ws[k][...])

    return _row_call(name, body, t // tm, [_rows(dya, tm), _rows(dyb, tm), _rows(dyc, tm)],
                     [_out_rows(t, w.shape[0], _F32, tm) for w in (wlo, wsc, wsg)], hbm=[wlo, wsc, wsg])


def _loss_grad(name, y, target, tm):
    t, d = y.shape

    def body(step, ti, ins, ws, outs, accs, scr):
        err = ins[0][...] - ins[1][...]
        outs[0][...] = err * (1.0 / d)
        accs[0][...] += (0.5 / d) * jnp.sum(err * err)

    dy, acc = _row_call(name, body, t // tm, [_rows(y, tm), _rows(target, tm)], [_out_rows(t, d, _F32, tm)],
                        accs=[(1, 128)])
    return acc[0, 0], dy


def _adamw(w, g, m, v):
    m = _ADAM_B1 * m + (1.0 - _ADAM_B1) * g
    v = _ADAM_B2 * v + (1.0 - _ADAM_B2) * (g * g)
    m_hat = m / (1.0 - _ADAM_B1 ** _ADAM_STEP)
    v_hat = v / (1.0 - _ADAM_B2 ** _ADAM_STEP)
    delta = -_ADAM_LR * (m_hat / (jnp.sqrt(v_hat) + _ADAM_EPS) + _ADAM_WD * w)
    return delta, m, v


def _flat_tm(rows):
    return 512 if rows % 512 == 0 else rows


def _sum_adamw(name, parts, w, m, v):
    rows, c = w.shape
    tm = _flat_tm(rows)
    np_ = len(parts)

    def body(step, ti, ins, ws, outs, accs, scr):
        g = ins[0][...].astype(_F32)
        for k in range(1, np_):
            g = g + ins[k][...].astype(_F32)
        delta, nm, nv = _adamw(ins[np_][...], g, ins[np_ + 1][...], ins[np_ + 2][...])
        outs[0][...] = g
        outs[1][...] = delta
        outs[2][...] = nm
        outs[3][...] = nv

    return _row_call(name, body, rows // tm, list(parts) + [_rows(w, tm), _rows(m, tm), _rows(v, tm)],
                     [_out_rows(rows, c, _F32, tm)] * 4)


def _sum_parts(name, parts, rows, c, out_dtype):
    tm = _flat_tm(rows)

    def body(step, ti, ins, ws, outs, accs, scr):
        g = ins[0][...].astype(_F32)
        for k in range(1, len(ins)):
            g = g + ins[k][...].astype(_F32)
        outs[0][...] = g.astype(out_dtype)

    return _row_call(name, body, rows // tm, list(parts), [_out_rows(rows, c, out_dtype, tm)])[0]


def _pack(arrs, dtype):
    flat = jnp.concatenate([a.reshape(-1).astype(dtype) for a in arrs])
    return _to_rows(flat, _flat_rows(flat.shape[0]))


def _pack_rows(arrs, dtype):
    parts = [a.reshape(-1, _LANES).astype(dtype) for a in arrs]
    n = sum(p.shape[0] for p in parts)
    rows = _flat_rows(n * _LANES)
    if rows > n:
        parts.append(jnp.zeros((rows - n, _LANES), dtype))
    return jnp.concatenate(parts, axis=0)


def _unpack_rows(flat, shapes):
    out, off = [], 0
    for s in shapes:
        n = math.prod(s) // _LANES
        out.append(flat[off:off + n].reshape(tuple(s)))
        off += n
    return out


def _unpack(flat, shapes):
    v = flat.reshape(-1)
    out, off = [], 0
    for s in shapes:
        n = math.prod(s)
        out.append(v[off:off + n].reshape(tuple(s)))
        off += n
    return out


def _place():
    return lax.axis_index("x"), lax.axis_index("y"), lax.axis_index("c")


class _Exchange:
    def __init__(self, ins, outs, sems, start, finish):
        self.ins, self.outs, self.sems, self.start, self.finish = ins, outs, sems, start, finish


def _run_exchange(name, ex):
    n_in, n_out = len(ex.ins), len(ex.outs)

    def body(*refs):
        ins, outs, sems = refs[:n_in], refs[n_in:n_in + n_out], refs[n_in + n_out:]
        ex.start(ins, outs, sems)
        ex.finish(ins, outs, sems)

    any_spec = pl.BlockSpec(memory_space=pl.ANY)
    return pl.pallas_call(body, name=name, out_shape=list(ex.outs), in_specs=[any_spec] * n_in,
                          out_specs=[any_spec] * n_out, scratch_shapes=list(ex.sems))(*ex.ins)


def _gather_exchange(block):
    r, c_ = block.shape

    def copies(ins, outs, sems):
        x_ref, out_ref = ins[0], outs[0]
        send_sems, recv_sems, local_sem = sems
        x, y, c = _place()
        me, sibling = (x, y, c), (x, y, 1 - c)
        chips = [(1 - x, y), (x, 1 - y), (1 - x, 1 - y)]

        def rows(px, py, pc):
            return out_ref.at[4 * px + 2 * py + pc]

        def copy(k, blk, to, src=None):
            return pltpu.make_async_remote_copy(
                src_ref=rows(*blk) if src is None else src, dst_ref=rows(*blk),
                send_sem=send_sems.at[k], recv_sem=recv_sems.at[k], device_id=to, device_id_type=_MESH)

        mine = pltpu.make_async_copy(x_ref, rows(*me), local_sem)
        first = [copy(0, me, sibling, src=x_ref)]
        first += [copy(1 + j, me, (*chip, c), src=x_ref) for j, chip in enumerate(chips)]
        passed = [copy(4 + j, (*chip, c), sibling) for j, chip in enumerate(chips)]
        landed = [copy(1 + j, (*chip, c), me) for j, chip in enumerate(chips)]
        landed_later = [copy(0, sibling, me)] + [copy(4 + j, (*chip, 1 - c), me) for j, chip in enumerate(chips)]
        return mine, first, passed, landed, landed_later

    def start(ins, outs, sems):
        mine, first, _, _, _ = copies(ins, outs, sems)
        mine.start()
        for cp in first:
            cp.start()

    def finish(ins, outs, sems):
        mine, first, passed, landed, landed_later = copies(ins, outs, sems)
        for j in range(3):
            landed[j].wait_recv()
            passed[j].start()
        for cp in landed_later:
            cp.wait_recv()
        for cp in first + passed:
            cp.wait_send()
        mine.wait()

    return _Exchange([block], [jax.ShapeDtypeStruct((_NDEV, r, c_), block.dtype)],
                     [pltpu.SemaphoreType.DMA((7,)), pltpu.SemaphoreType.DMA((7,)), pltpu.SemaphoreType.DMA(())],
                     start, finish)


def _all_gather(name, block):
    return _run_exchange(name, _gather_exchange(block))[0]


def _sibling_exchange_of(dm):
    _, _, r, c_ = dm.shape

    def copies(ins, outs, sems):
        x, y, c = _place()
        return [pltpu.make_async_remote_copy(
            src_ref=ins[0].at[k, 1 - c], dst_ref=outs[0].at[k], send_sem=sems[0].at[k],
            recv_sem=sems[1].at[k], device_id=(x, y, 1 - c), device_id_type=_MESH) for k in range(4)]

    def start(ins, outs, sems):
        for cp in copies(ins, outs, sems):
            cp.start()

    def finish(ins, outs, sems):
        for cp in copies(ins, outs, sems):
            cp.wait()

    return _Exchange([dm], [jax.ShapeDtypeStruct((4, r, c_), dm.dtype)],
                     [pltpu.SemaphoreType.DMA((4,)), pltpu.SemaphoreType.DMA((4,))], start, finish)


def _pair_sum(name, dm, got, core):
    _, _, r, c_ = dm.shape
    tm = _flat_tm(r)

    def kern(core_ref, a_ref, b_ref, o_ref):
        o_ref[...] = (a_ref[...] + b_ref[...]).astype(o_ref.dtype)

    grid_spec = pltpu.PrefetchScalarGridSpec(
        num_scalar_prefetch=1, grid=(4, r // tm),
        in_specs=[pl.BlockSpec((None, None, tm, c_), lambda k, i, cr: (k, cr[0], i, 0)),
                  pl.BlockSpec((None, tm, c_), lambda k, i, cr: (k, i, 0))],
        out_specs=pl.BlockSpec((None, tm, c_), lambda k, i, cr: (k, i, 0)))
    return pl.pallas_call(
        kern, name=name, grid_spec=grid_spec, out_shape=jax.ShapeDtypeStruct((4, r, c_), _MM),
        compiler_params=pltpu.CompilerParams(dimension_semantics=("arbitrary", "arbitrary"),
                                             vmem_limit_bytes=_VMEM_LIMIT),
    )(core, dm, got)


def _chip_sum(name, p, others, chip):
    _, r, c_ = p.shape
    tm = _flat_tm(r)

    def kern(chip_ref, p_ref, o0, o1, o2, g_out):
        g_out[...] = (p_ref[...].astype(_F32) + o0[...].astype(_F32) + o1[...].astype(_F32)
                      + o2[...].astype(_F32))

    grid_spec = pltpu.PrefetchScalarGridSpec(
        num_scalar_prefetch=1, grid=(r // tm,),
        in_specs=[pl.BlockSpec((None, tm, c_), lambda i, cr: (cr[0], i, 0))]
        + [pl.BlockSpec((None, tm, c_), lambda i, cr, k=k: (k, i, 0)) for k in range(3)],
        out_specs=pl.BlockSpec((tm, c_), lambda i, cr: (i, 0)))
    return pl.pallas_call(
        kern, name=name, grid_spec=grid_spec, out_shape=jax.ShapeDtypeStruct((r, c_), _F32),
        compiler_params=pltpu.CompilerParams(dimension_semantics=("arbitrary",), vmem_limit_bytes=_VMEM_LIMIT),
    )(chip, p, others, others, others)


def _row_tile(rows, lanes):
    for d in range(min(rows, max(8, (1 << 18) // lanes)) // 8 * 8, 7, -8):
        if rows % d == 0:
            return d
    return rows


def _adamw_update(name, w, g, m, v):
    shape = w.shape
    as2d = lambda a: a.reshape(-1, shape[-1])
    rows = math.prod(shape[:-1])
    tm = _row_tile(rows, shape[-1])

    def body(step, ti, ins, ws, outs, accs, scr):
        delta, nm, nv = _adamw(ins[0][...], ins[1][...], ins[2][...], ins[3][...])
        outs[0][...] = delta
        outs[1][...] = nm
        outs[2][...] = nv

    res = _row_call(name, body, rows // tm, [_rows(as2d(a), tm) for a in (w, g, m, v)],
                    [_out_rows(rows, shape[-1], _F32, tm)] * 3)
    return [r.reshape(shape) for r in res]


def _chip_exchange_of(p):
    _, r, c_ = p.shape

    def copies(ins, outs, sems):
        x, y, c = _place()
        chips = [(1 - x, y), (x, 1 - y), (1 - x, 1 - y)]
        return [pltpu.make_async_remote_copy(
            src_ref=ins[0].at[2 * px + py], dst_ref=outs[0].at[j], send_sem=sems[0].at[j],
            recv_sem=sems[1].at[j], device_id=(px, py, c), device_id_type=_MESH)
            for j, (px, py) in enumerate(chips)]

    def start(ins, outs, sems):
        for cp in copies(ins, outs, sems):
            cp.start()

    def finish(ins, outs, sems):
        for cp in copies(ins, outs, sems):
            cp.wait()

    return _Exchange([p], [jax.ShapeDtypeStruct((3, r, c_), p.dtype)],
                     [pltpu.SemaphoreType.DMA((3,)), pltpu.SemaphoreType.DMA((3,))], start, finish)


def _gather_full(gathered, names, shard_shapes, axes, unpack):
    per_dev = [unpack(gathered[d], shard_shapes) for d in range(_NDEV)]
    return {nme: jnp.concatenate([per_dev[d][i] for d in range(_NDEV)], axis=ax)
            for i, (nme, ax) in enumerate(zip(names, axes))}


def _block_rows(a, ax, d):
    s = a.shape[ax] // _NDEV
    return lax.slice_in_dim(a, d * s, (d + 1) * s, axis=ax).reshape(-1, _LANES)


def _to_rows(flat, rows):
    return jnp.pad(flat, (0, rows * _LANES - flat.shape[0])).reshape(rows, _LANES)


def _flat_rows(n):
    rows = -(-n // _LANES)
    return -(-rows // 512) * 512 if rows >= 512 else -(-rows // 16) * 16


def _tm(t, want):
    return min(t, want)


def _ffn_forward(tag, x, pre_g, wg, wu, wd, post_g):
    t = x.shape[0]
    h, a, b, s = _ffn_up(tag + "_up", x, pre_g, wg, wu, _tm(t, 256))
    f, x_out = _proj_norm_res(tag + "_down", s, x, post_g, wd, 0.5, _tm(t, 512))
    return x_out, dict(x=x, h=h, a=a, b=b, s=s, f=f)


def _ffn_backward(tag, dxo, sv, pre_g, wg, wu, wd, post_g, comm=None):
    t = dxo.shape[0]
    da, db, df, dpost, *carried = _ffn_bwd_post(tag + "_bwd_post", dxo, sv['f'], sv['a'], sv['b'], post_g, wd,
                                                0.5, _tm(t, 256), comm=comm)
    dx, dpre = _bwd_in_norm(tag + "_bwd_pre", [da, db], [wg, wu], sv['x'], dxo, pre_g, _tm(t, 512))
    tk = _tm(t, 1024)
    grads = dict(pre_g=dpre[0], post_g=dpost[0],
                 w_gate=_xty(tag + "_dwg", sv['h'], da, tk), w_up=_xty(tag + "_dwu", sv['h'], db, tk),
                 w_down=_xty(tag + "_dwd", sv['s'], df, tk))
    return dx, grads, carried


def _mixer_weights(w):
    lw = w['lru_conv_w'].shape[-1]
    sw = w['sc_conv_w'].shape[-1]
    gw = w['sgu_ln_g'].shape[-1]
    win = w['w_in']
    cuts = [0, lw, 2 * lw, 2 * lw + 3 * sw + 2 * gw, win.shape[1]]
    p = dict(lw=lw, sw=sw, gw=gw,
             win=[win[:, cuts[k]:cuts[k + 1]] for k in range(4)],
             cw=w['lru_conv_w'], cb=w['lru_conv_b'][None, :],
             wa=w['lru_wa'], wx=w['lru_wx'],
             ba=w['lru_ba'], bx=w['lru_bx'], lam=w['lru_lambda'],
             wlo=w['lru_w_out'], scw=w['sc_conv_w'], wsc=w['sc_w_out'],
             lg=w['sgu_ln_g'][None, :], lb=w['sgu_ln_b'][None, :],
             ws=w['sgu_w_s'].astype(_MM), wst=jnp.swapaxes(w['sgu_w_s'], 1, 2).astype(_MM),
             bias=jnp.repeat(w['sgu_b'].T, gw // _HEADS, axis=1),
             wsg=w['sgu_w_out'], wo=w['w_o'],
             pre_g=w['mix_pre_g'][None, :], post_g=w['mix_post_g'][None, :])
    return p


def _mixer_forward(tag, x, p, comm=None):
    t = x.shape[0]
    tl = _tm(t, 512)
    hm, zg, zx, zmid, zm, *carried = _mix_in(tag + "_in", x, p['pre_g'], p['win'], _tm(t, 512), comm=comm)
    lru = lambda d: (p['cw'], p['cb'], p['wa'][d], p['wx'][d], p['ba'][d:d + 1], p['bx'][d:d + 1],
                     p['lam'][d:d + 1])
    hf, = _lru_fwd(tag + "_lru_f", zx, *lru(0), tl, False)
    hb, pa = _lru_fwd(tag + "_lru_b", zx, *lru(1), tl, True, hf=hf, zg=zg)
    pb, pc = _bc_fwd(tag + "_bc", zmid, p['scw'], p['lg'], p['lb'], p['ws'], p['bias'], p['sw'], p['gw'], tl)
    ya, yb, yc, m = _mix_proj(tag + "_proj", pa, pb, pc, zm, p['wlo'], p['wsc'], p['wsg'], _tm(t, 512))
    mo, x_out = _proj_norm_res(tag + "_out", m, x, p['post_g'], p['wo'], 1.0, _tm(t, 512))
    sv = dict(x=x, hm=hm, zg=zg, zx=zx, zmid=zmid, zm=zm, hf=hf, hb=hb, pa=pa, pb=pb, pc=pc,
              ya=ya, yb=yb, yc=yc, m=m, mo=mo)
    return x_out, sv, carried


def _mixer_backward(tag, dxo, sv, p, comm=None):
    t = dxo.shape[0]
    tl = _tm(t, 256)
    tk = _tm(t, 1024)
    dmo, dya, dyb, dyc, dzm, dpost = _mix_bwd_out(tag + "_bwd_out", dxo, sv['mo'], sv['ya'], sv['yb'], sv['yc'],
                                                  sv['zm'], p['post_g'], p['wo'], _tm(t, 512))
    dpa, dpb, dpc = _mix_bwd_proj(tag + "_bwd_proj", dya, dyb, dyc, p['wlo'], p['wsc'], p['wsg'], _tm(t, 512))
    dzmid, dscw, dlg, dlb, dws, dbias = _bc_bwd(tag + "_bc_bwd", sv['zmid'], dpb, dpc, p['scw'], p['lg'], p['lb'],
                                                p['ws'], p['wst'], p['bias'], p['sw'], p['gw'], tl)
    lru = lambda d: (p['cw'], p['cb'], p['wa'][d], p['wx'][d], p['ba'][d:d + 1], p['bx'][d:d + 1],
                     p['lam'][d:d + 1])
    dxc0, dzg, d_h, dwa0, dwx0, dba0, dbx0, dlam0, *carried = _lru_bwd(
        tag + "_lru_bwd_f", sv['zx'], dpa, sv['hf'], *lru(0), tl, 0, zg=sv['zg'], h_other=sv['hb'], comm=comm)
    dxc, dwa1, dwx1, dba1, dbx1, dlam1 = _lru_bwd(tag + "_lru_bwd_b", sv['zx'], d_h, sv['hb'], *lru(1), tl, 1,
                                                  dxc_in=dxc0)
    dzx, dcw, dcb = _lru_conv_bwd(tag + "_conv_bwd", dxc, sv['zx'], p['cw'], _tm(t, 512))
    dzs = [dzg, dzx, dzmid, dzm]
    dx, dpre = _bwd_in_norm(tag + "_bwd_in", dzs, p['win'], sv['x'], dxo, p['pre_g'], _tm(t, 256))
    gh = p['gw'] // _HEADS
    grads = dict(
        mix_pre_g=dpre[0], mix_post_g=dpost[0],
        w_in=jnp.concatenate([_xty(tag + "_dwin%d" % k, sv['hm'], dz, tk) for k, dz in enumerate(dzs)], axis=1),
        lru_conv_w=dcw, lru_conv_b=dcb[0],
        lru_wa=jnp.stack([dwa0, dwa1]), lru_wx=jnp.stack([dwx0, dwx1]),
        lru_ba=jnp.concatenate([dba0, dba1]), lru_bx=jnp.concatenate([dbx0, dbx1]),
        lru_lambda=jnp.concatenate([dlam0, dlam1]),
        lru_w_out=_xty(tag + "_dwlo", sv['pa'], dya, tk),
        sc_conv_w=dscw, sc_w_out=_xty(tag + "_dwsc", sv['pb'], dyb, tk),
        sgu_ln_g=dlg[0], sgu_ln_b=dlb[0], sgu_w_s=dws,
        sgu_b=jnp.sum(dbias.reshape(_CHUNK, _HEADS, gh), axis=2).T,
        sgu_w_out=_xty(tag + "_dwsg", sv['pc'], dyc, tk),
        w_o=_xty(tag + "_dwo", sv['m'], dmo, tk))
    return dx, grads, carried


def _forward_backward(x, target, depth, gathered, weights_of, gather_of, blocks_of, core):
    t = x.shape[0]
    saved = []
    for l in range(depth):
        w = weights_of(l, gathered)
        g = lambda nme: w[nme][None, :]
        tag = "l%d_" % l
        x, s1 = _ffn_forward(tag + "ffn1", x, g('ffn1_pre_g'), w['ffn1_w_gate'], w['ffn1_w_up'], w['ffn1_w_down'],
                             g('ffn1_post_g'))
        p = _mixer_weights(w)
        x, sm, carried = _mixer_forward(tag + "mix", x, p, comm=gather_of(l + 1) if l + 1 < depth else None)
        gathered = carried[0] if carried else None
        x, s2 = _ffn_forward(tag + "ffn2", x, g('ffn2_pre_g'), w['ffn2_w_gate'], w['ffn2_w_up'], w['ffn2_w_down'],
                             g('ffn2_post_g'))
        saved.append((s1, sm, s2, p, w))
    loss, dx = _loss_grad("loss", x, target, _tm(t, 512))
    per_layer, reduced, above = [], [], None
    for l in reversed(range(depth)):
        s1, sm, s2, p, w = saved[l]
        g = lambda nme: w[nme][None, :]
        tag = "l%d_" % l
        grads = {}
        dx, g2, got = _ffn_backward(tag + "ffn2", dx, s2, g('ffn2_pre_g'), w['ffn2_w_gate'], w['ffn2_w_up'],
                                    w['ffn2_w_down'], g('ffn2_post_g'),
                                    comm=_sibling_exchange_of(above) if above is not None else None)
        grads.update({'ffn2_' + k: v for k, v in g2.items()})
        pair = _pair_sum(tag + "reduce_pair_sum", above, got[0], core) if above is not None else None
        dx, gm, others = _mixer_backward(tag + "mix", dx, sm, p,
                                         comm=_chip_exchange_of(pair) if pair is not None else None)
        grads.update(gm)
        if pair is not None:
            reduced.append((pair, others[0]))
        dx, g1, _ = _ffn_backward(tag + "ffn1", dx, s1, g('ffn1_pre_g'), w['ffn1_w_gate'], w['ffn1_w_up'],
                                  w['ffn1_w_down'], g('ffn1_post_g'))
        grads.update({'ffn1_' + k: v for k, v in g1.items()})
        per_layer.append(grads)
        above = blocks_of(grads)
    per_layer.reverse()
    reduced.reverse()
    return loss, dx, per_layer, reduced, above


def kernel(x, ffn1_pre_g, ffn1_w_gate, ffn1_w_up, ffn1_w_down, ffn1_post_g, mix_pre_g, w_in, lru_conv_w, lru_conv_b, lru_wa, lru_ba, lru_wx, lru_bx, lru_lambda, lru_w_out, sc_conv_w, sc_w_out, sgu_ln_g, sgu_ln_b, sgu_w_s, sgu_b, sgu_w_out, w_o, mix_post_g, ffn2_pre_g, ffn2_w_gate, ffn2_w_up, ffn2_w_down, ffn2_post_g, loss_target, m_ffn1_pre_g, m_ffn1_w_gate, m_ffn1_w_up, m_ffn1_w_down, m_ffn1_post_g, m_mix_pre_g, m_w_in, m_lru_conv_w, m_lru_conv_b, m_lru_wa, m_lru_ba, m_lru_wx, m_lru_bx, m_lru_lambda, m_lru_w_out, m_sc_conv_w, m_sc_w_out, m_sgu_ln_g, m_sgu_ln_b, m_sgu_w_s, m_sgu_b, m_sgu_w_out, m_w_o, m_mix_post_g, m_ffn2_pre_g, m_ffn2_w_gate, m_ffn2_w_up, m_ffn2_w_down, m_ffn2_post_g, v_ffn1_pre_g, v_ffn1_w_gate, v_ffn1_w_up, v_ffn1_w_down, v_ffn1_post_g, v_mix_pre_g, v_w_in, v_lru_conv_w, v_lru_conv_b, v_lru_wa, v_lru_ba, v_lru_wx, v_lru_bx, v_lru_lambda, v_lru_w_out, v_sc_conv_w, v_sc_w_out, v_sgu_ln_g, v_sgu_ln_b, v_sgu_w_s, v_sgu_b, v_sgu_w_out, v_w_o, v_mix_post_g, v_ffn2_pre_g, v_ffn2_w_gate, v_ffn2_w_up, v_ffn2_w_down, v_ffn2_post_g):
    args = locals()
    wts = {n: args[n] for n in _WEIGHTS}
    mom = {n: args['m_' + n] for n in _WEIGHTS}
    var = {n: args['v_' + n] for n in _WEIGHTS}
    cx, cy, cc = _place()
    dev = 4 * cx + 2 * cy + cc
    big, small = list(_BIG), list(_SMALL_SHARDED)

    depth = w_in.shape[0]
    core = jnp.reshape(cc, (1,)).astype(jnp.int32)
    chip = jnp.reshape(2 * cx + cy, (1,)).astype(jnp.int32)
    layer_shapes = [wts[n].shape[1:] for n in big]
    rows = _flat_rows(sum(math.prod(s) for s in layer_shapes))

    g_small = _all_gather("gather_vectors", _pack([wts[n] for n in small], _F32))
    vecs = dict(wts)
    vecs.update(_gather_full(g_small, small, [wts[n].shape for n in small], [_SMALL_SHARDED[n] for n in small],
                             _unpack))

    def layer_block(l):
        return _pack_rows([wts[n][l] for n in big], _MM)

    def weights_of(l, gathered):
        w = {n: vecs[n][l] for n in _WEIGHTS if n not in _BIG}
        w.update(_gather_full(gathered, big, layer_shapes, [_BIG[n] - 1 for n in big], _unpack_rows))
        return w

    def blocks_of(grads):
        pieces = []
        for d in range(_NDEV):
            blocks = [_block_rows(grads[n], _BIG[n] - 1, d) for n in big]
            fill = rows - sum(b.shape[0] for b in blocks)
            pieces += blocks + ([jnp.zeros((fill, _LANES), _F32)] if fill else [])
        return jnp.concatenate(pieces, axis=0).reshape(4, 2, rows, _LANES)

    loss, grad_x, grads, reduced, blocks0 = _forward_backward(
        x[0], loss_target[0], depth, _all_gather("l0_gather_matrices", layer_block(0)), weights_of,
        lambda l: _gather_exchange(layer_block(l)), blocks_of, core)
    loss = lax.psum(loss, ("x", "y", "c"))

    got0, = _run_exchange("l0_reduce_sibling", _sibling_exchange_of(blocks0))
    pair0 = _pair_sum("l0_reduce_pair_sum", blocks0, got0, core)
    others0, = _run_exchange("l0_reduce_chips", _chip_exchange_of(pair0))
    per_layer = [_unpack_rows(_chip_sum("l%d_reduce_final_sum" % l, pair, others, chip), layer_shapes)
                 for l, (pair, others) in enumerate([(pair0, others0)] + reduced)]
    out = {}
    for i, n in enumerate(big):
        g = jnp.stack([per_layer[l][i] for l in range(depth)])
        out['grad_' + n] = g
        out['delta_' + n], out['new_m_' + n], out['new_v_' + n] = _adamw_update("update_" + n, wts[n], g, mom[n],
                                                                                 var[n])

    vec = _REPLICATED + small
    gvec = {n: jnp.stack([g[n] for g in grads]) for n in vec}
    part = _pack([gvec[n] for n in vec], _F32)
    allp = _all_gather("gather_vector_grads", part)
    rv = part.shape[0]
    tmv = _flat_tm(rv)
    gsum = _sum_parts("reduce_vector_grads", [_rows3(allp, k, tmv) for k in range(_NDEV)], rv, _LANES, _F32)
    gfull = dict(zip(vec, _unpack(gsum, [gvec[n].shape for n in vec])))
    gloc = []
    for n in vec:
        if n in _SMALL_SHARDED:
            ax = _SMALL_SHARDED[n]
            sz = wts[n].shape[ax]
            gloc.append(lax.dynamic_slice_in_dim(gfull[n], dev * sz, sz, axis=ax))
        else:
            gloc.append(gfull[n])
    gl = _pack(gloc, _F32)
    g_s, d_s, m_s, v_s = _sum_adamw("update_vectors", [_rows(gl, _flat_tm(gl.shape[0]))],
                                    _pack([wts[n] for n in vec], _F32), _pack([mom[n] for n in vec], _F32),
                                    _pack([var[n] for n in vec], _F32))
    shapes_s = [wts[n].shape for n in vec]
    for key, flat in (('grad_', g_s), ('delta_', d_s), ('new_m_', m_s), ('new_v_', v_s)):
        for n, a in zip(vec, _unpack(flat, shapes_s)):
            out[key + n] = a

    res = [loss, grad_x[None]]
    for key in ('grad_', 'delta_', 'new_m_', 'new_v_'):
        res += [out[key + n] for n in _WEIGHTS]
    return tuple(res)
```

```python
import functools
import math

import jax
import jax.numpy as jnp
from jax import lax
from jax.experimental import pallas as pl
from jax.experimental.pallas import tpu as pltpu

_F32 = jnp.float32
_MM = jnp.bfloat16
_EPS = 1e-6
_HEADS = 4
_CHUNK = 128
_LRU_C = 8.0
_HALO = 16
_LANES = 1024
_NDEV = 8
_VMEM_LIMIT = 56 * 1024 * 1024
_GELU_K = math.sqrt(2.0 / math.pi)
_GELU_C = 0.044715
_MESH = pl.DeviceIdType.MESH

_ADAM_LR, _ADAM_B1, _ADAM_B2, _ADAM_EPS, _ADAM_WD, _ADAM_STEP = 1e-3, 0.9, 0.999, 1e-8, 0.01, 10

_WEIGHTS = ['ffn1_pre_g', 'ffn1_w_gate', 'ffn1_w_up', 'ffn1_w_down', 'ffn1_post_g', 'mix_pre_g', 'w_in',
            'lru_conv_w', 'lru_conv_b', 'lru_wa', 'lru_ba', 'lru_wx', 'lru_bx', 'lru_lambda', 'lru_w_out',
            'sc_conv_w', 'sc_w_out', 'sgu_ln_g', 'sgu_ln_b', 'sgu_w_s', 'sgu_b', 'sgu_w_out', 'w_o',
            'mix_post_g', 'ffn2_pre_g', 'ffn2_w_gate', 'ffn2_w_up', 'ffn2_w_down', 'ffn2_post_g']
_BIG = {'ffn1_w_gate': 2, 'ffn1_w_up': 2, 'ffn1_w_down': 1, 'w_in': 2, 'lru_wa': 3, 'lru_wx': 3,
        'lru_w_out': 1, 'sc_w_out': 2, 'sgu_w_out': 2, 'w_o': 1,
        'ffn2_w_gate': 2, 'ffn2_w_up': 2, 'ffn2_w_down': 1}
_SMALL_SHARDED = {'lru_conv_w': 2, 'lru_ba': 2, 'lru_bx': 2, 'lru_lambda': 2, 'sc_conv_w': 2}
_REPLICATED = ['ffn1_pre_g', 'ffn1_post_g', 'mix_pre_g', 'lru_conv_b', 'sgu_ln_g', 'sgu_ln_b', 'sgu_w_s',
               'sgu_b', 'mix_post_g', 'ffn2_pre_g', 'ffn2_post_g']


def _dot(a, b):
    return jnp.dot(a, b, preferred_element_type=_F32)


def _dot_nt(a, b):
    return lax.dot_general(a, b, (((1,), (1,)), ((), ())), preferred_element_type=_F32)


def _dot_tn(a, b):
    return lax.dot_general(a, b, (((0,), (0,)), ((), ())), preferred_element_type=_F32)


def _sigmoid(x):
    return 0.5 * jnp.tanh(0.5 * x) + 0.5


def _gelu(x):
    t = jnp.tanh(x * (_GELU_K + (_GELU_K * _GELU_C) * (x * x)))
    return (0.5 * x) * (1.0 + t)


def _gelu_and_grad(x):
    x2 = x * x
    t = jnp.tanh(x * (_GELU_K + (_GELU_K * _GELU_C) * x2))
    hx = 0.5 * x
    return hx * (1.0 + t), 0.5 * (1.0 + t) + hx * (1.0 - t * t) * (_GELU_K + (3.0 * _GELU_K * _GELU_C) * x2)


def _gelu_grad(x):
    return _gelu_and_grad(x)[1]


def _rms_fwd(x, g):
    r = lax.rsqrt(jnp.mean(x * x, axis=-1, keepdims=True) + _EPS)
    return x * r * g


def _rms_bwd(dy, x, g):
    r = lax.rsqrt(jnp.mean(x * x, axis=-1, keepdims=True) + _EPS)
    xh = x * r
    dxh = dy * g
    dx = r * (dxh - xh * jnp.mean(dxh * xh, axis=-1, keepdims=True))
    return dx, jnp.sum(dy * xh, axis=0, keepdims=True)


def _neg_softplus_neg(lam):
    e = jnp.exp(-jnp.abs(lam))
    l1p = jnp.where(e < 1e-2, e * (1.0 - e * (0.5 - e * (1.0 / 3.0 - 0.25 * e))), jnp.log(1.0 + e))
    return -_LRU_C * (jnp.maximum(-lam, 0.0) + l1p)


def _shift_rows(xe, d, tm):
    n = xe.shape[0]
    if d == 0:
        return xe[_HALO:_HALO + tm]
    return pltpu.roll(xe, (-d) % n, axis=0)[_HALO:_HALO + tm]


def _with_halo(cur, prev, nxt, first, last):
    p = jnp.where(first, 0.0, prev.astype(_F32))
    n = jnp.where(last, 0.0, nxt.astype(_F32))
    return jnp.concatenate([p, cur.astype(_F32), n], axis=0)


def _rows(arr, tm):
    c = arr.shape[1]
    return (arr, (tm, c), lambda ti: (ti, 0))


def _rows3(arr, k, tm):
    c = arr.shape[2]
    return (arr, (None, tm, c), lambda ti, k=k: (k, ti, 0))


def _halo_prev(arr, tm):
    c = arr.shape[1]
    return (arr, (_HALO, c), lambda ti: (jnp.maximum(ti * (tm // _HALO) - 1, 0), 0))


def _halo_next(arr, tm):
    c = arr.shape[1]
    nblk = arr.shape[0] // _HALO
    return (arr, (_HALO, c), lambda ti: (jnp.minimum((ti + 1) * (tm // _HALO), nblk - 1), 0))


def _full(arr):
    nd = arr.ndim
    return (arr, arr.shape, lambda ti, nd=nd: (0,) * nd)


def _out_rows(t, c, dtype, tm):
    return ((t, c), dtype, (tm, c), lambda ti: (ti, 0))


def _row_call(name, body, n_tiles, ins, outs, accs=(), hbm=(), scratch=(), reverse=False, comm=None):
    n_in, n_hbm, n_out, n_acc = len(ins), len(hbm), len(outs), len(accs)
    c_ins = list(comm.ins) if comm else []
    c_outs = list(comm.outs) if comm else []
    c_sems = list(comm.sems) if comm else []

    def tile_of(step):
        return (n_tiles - 1 - step) if reverse else step

    def spec(block, index_fn):
        return pl.BlockSpec(block, lambda s, f=index_fn: f(tile_of(s)))

    def kern(*refs):
        in_refs = refs[:n_in]
        hbm_refs = refs[n_in:n_in + n_hbm]
        o0 = n_in + n_hbm + len(c_ins)
        cin_refs = refs[n_in + n_hbm:o0]
        out_refs = refs[o0:o0 + n_out]
        acc_refs = refs[o0 + n_out:o0 + n_out + n_acc]
        s0 = o0 + n_out + n_acc + len(c_outs)
        cout_refs = refs[o0 + n_out + n_acc:s0]
        w_refs = refs[s0:s0 + n_hbm]
        csem_refs = refs[s0 + n_hbm:s0 + n_hbm + len(c_sems)]
        scr = refs[s0 + n_hbm + len(c_sems):]
        step = pl.program_id(0)

        @pl.when(step == 0)
        def _():
            if comm:
                comm.start(cin_refs, cout_refs, csem_refs)
            for src, dst in zip(hbm_refs, w_refs):
                pltpu.sync_copy(src, dst)
            for a in acc_refs:
                a[...] = jnp.zeros(a.shape, a.dtype)

        body(step, tile_of(step), in_refs, w_refs, out_refs, acc_refs, scr)

        if comm:
            @pl.when(step == n_tiles - 1)
            def _():
                comm.finish(cin_refs, cout_refs, csem_refs)

    any_spec = pl.BlockSpec(memory_space=pl.ANY)
    in_specs = [spec(b, f) for (_, b, f) in ins] + [any_spec] * (n_hbm + len(c_ins))
    out_specs = [spec(b, f) for (_, _, b, f) in outs]
    out_specs += [pl.BlockSpec(s, lambda st, nd=len(s): (0,) * nd) for s in accs] + [any_spec] * len(c_outs)
    out_shape = [jax.ShapeDtypeStruct(s, d) for (s, d, _, _) in outs]
    out_shape += [jax.ShapeDtypeStruct(s, _F32) for s in accs] + c_outs
    scratch_shapes = [pltpu.VMEM(w.shape, w.dtype) for w in hbm] + c_sems + list(scratch)
    res = pl.pallas_call(
        kern, name=name, grid=(n_tiles,), in_specs=in_specs, out_specs=out_specs, out_shape=out_shape,
        scratch_shapes=scratch_shapes,
        compiler_params=pltpu.CompilerParams(dimension_semantics=("arbitrary",), vmem_limit_bytes=_VMEM_LIMIT),
    )(*[a for (a, _, _) in ins], *hbm, *c_ins)
    return list(res)


def _xty(name, x, y, tk):
    t, k1 = x.shape
    k2 = y.shape[1]

    def kern(x_ref, y_ref, o_ref):
        @pl.when(pl.program_id(0) == 0)
        def _():
            o_ref[...] = jnp.zeros(o_ref.shape, o_ref.dtype)

        o_ref[...] += _dot_tn(x_ref[...], y_ref[...])

    return pl.pallas_call(
        kern, name=name, grid=(t // tk,),
        in_specs=[pl.BlockSpec((tk, k1), lambda k: (k, 0)), pl.BlockSpec((tk, k2), lambda k: (k, 0))],
        out_specs=pl.BlockSpec((k1, k2), lambda k: (0, 0)),
        out_shape=jax.ShapeDtypeStruct((k1, k2), _F32),
        compiler_params=pltpu.CompilerParams(dimension_semantics=("arbitrary",), vmem_limit_bytes=_VMEM_LIMIT),
    )(x, y)


def _ffn_up(name, x, pre_g, wg, wu, tm):
    t, d = x.shape
    f = wg.shape[1]

    def body(step, ti, ins, ws, outs, accs, scr):
        x_ref, g_ref = ins
        h = _rms_fwd(x_ref[...], g_ref[...]).astype(_MM)
        a = _dot(h, ws[0][...])
        b = _dot(h, ws[1][...])
        outs[0][...] = h
        outs[1][...] = a.astype(_MM)
        outs[2][...] = b.astype(_MM)
        outs[3][...] = (a * _sigmoid(a) * b).astype(_MM)

    return _row_call(name, body, t // tm, [_rows(x, tm), _full(pre_g)],
                     [_out_rows(t, d, _MM, tm), _out_rows(t, f, _MM, tm), _out_rows(t, f, _MM, tm),
                      _out_rows(t, f, _MM, tm)], hbm=[wg, wu])


def _proj_norm_res(name, lhs, x, post_g, w, scale, tm):
    t, d = x.shape

    def body(step, ti, ins, ws, outs, accs, scr):
        l_ref, x_ref, g_ref = ins
        f = _dot(l_ref[...], ws[0][...])
        outs[0][...] = f
        outs[1][...] = x_ref[...] + scale * _rms_fwd(f, g_ref[...])

    return _row_call(name, body, t // tm, [_rows(lhs, tm), _rows(x, tm), _full(post_g)],
                     [_out_rows(t, d, _F32, tm), _out_rows(t, d, _F32, tm)], hbm=[w])


def _ffn_bwd_post(name, dxo, f, a, b, post_g, wd, scale, tm, comm=None):
    t, d = dxo.shape
    ff = a.shape[1]

    def body(step, ti, ins, ws, outs, accs, scr):
        dxo_ref, f_ref, a_ref, b_ref, g_ref = ins
        df, dg = _rms_bwd(scale * dxo_ref[...], f_ref[...], g_ref[...])
        accs[0][...] += dg
        dfb = df.astype(_MM)
        ds = _dot_nt(dfb, ws[0][...])
        a32 = a_ref[...].astype(_F32)
        b32 = b_ref[...].astype(_F32)
        sg = _sigmoid(a32)
        outs[0][...] = (ds * b32 * (sg * (1.0 + a32 * (1.0 - sg)))).astype(_MM)
        outs[1][...] = (ds * (a32 * sg)).astype(_MM)
        outs[2][...] = dfb

    return _row_call(name, body, t // tm,
                     [_rows(dxo, tm), _rows(f, tm), _rows(a, tm), _rows(b, tm), _full(post_g)],
                     [_out_rows(t, ff, _MM, tm), _out_rows(t, ff, _MM, tm), _out_rows(t, d, _MM, tm)],
                     accs=[(1, d)], hbm=[wd], comm=comm)


def _bwd_in_norm(name, dzs, ws_list, x, dxo, pre_g, tm):
    t, d = x.shape
    nz = len(dzs)

    def body(step, ti, ins, ws, outs, accs, scr):
        dh = _dot_nt(ins[0][...], ws[0][...])
        for k in range(1, nz):
            dh = dh + _dot_nt(ins[k][...], ws[k][...])
        x_ref, dxo_ref, g_ref = ins[nz:]
        dx, dg = _rms_bwd(dh, x_ref[...], g_ref[...])
        accs[0][...] += dg
        outs[0][...] = dxo_ref[...] + dx

    return _row_call(name, body, t // tm,
                     [_rows(z, tm) for z in dzs] + [_rows(x, tm), _rows(dxo, tm), _full(pre_g)],
                     [_out_rows(t, d, _F32, tm)], accs=[(1, d)], hbm=list(ws_list))


def _mix_in(name, x, pre_g, w_parts, tm, comm=None):
    t, d = x.shape

    def body(step, ti, ins, ws, outs, accs, scr):
        x_ref, g_ref = ins
        h = _rms_fwd(x_ref[...], g_ref[...]).astype(_MM)
        outs[0][...] = h
        for k in range(len(ws)):
            outs[1 + k][...] = _dot(h, ws[k][...]).astype(_MM)

    return _row_call(name, body, t // tm, [_rows(x, tm), _full(pre_g)],
                     [_out_rows(t, d, _MM, tm)] + [_out_rows(t, w.shape[1], _MM, tm) for w in w_parts],
                     hbm=list(w_parts), comm=comm)


def _lru_conv(xe, cw, cb, tm):
    xc = cb
    for k in range(4):
        xc = xc + _shift_rows(xe, k - 2, tm) * cw[k:k + 1, :]
    return xc


def _lru_gates(xc, wa_ref, wx_ref, ba, bx, c):
    dh = xc.shape[1] // _HEADS
    gas, gxs = [], []
    for hh in range(_HEADS):
        xs = xc[:, hh * dh:(hh + 1) * dh].astype(_MM)
        gas.append(_dot(xs, wa_ref[hh]))
        gxs.append(_dot(xs, wx_ref[hh]))
    r = _sigmoid(jnp.concatenate(gas, axis=1) + ba)
    i = _sigmoid(jnp.concatenate(gxs, axis=1) + bx)
    la = c * r
    a = jnp.exp(la)
    em = -jnp.tanh(la) * (a * a + 1.0)
    return r, i, a, em


def _scan_scratch(tm, w):
    return [pltpu.VMEM((tm, w), _F32), pltpu.VMEM((tm, w), _F32)]


def _tile_scan(a, u, scan_scr, h_dst, carry, tm, descending):
    a_scr, u_scr = scan_scr
    a_scr[...] = a
    u_scr[...] = u
    ng = tm // 8
    w = a.shape[1]
    row = lax.broadcasted_iota(jnp.int32, (8, w), 0)

    def grp(j, carry):
        g = (ng - 1 - j) if descending else j
        r0 = pl.multiple_of(g * 8, 8)
        a8 = a_scr[pl.ds(r0, 8), :]
        u8 = u_scr[pl.ds(r0, 8), :]
        for dd in (1, 2, 4):
            if descending:
                ok = row < 8 - dd
                sh = 8 - dd
            else:
                ok = row >= dd
                sh = dd
            a_s = jnp.where(ok, pltpu.roll(a8, sh, axis=0), 1.0)
            u_s = jnp.where(ok, pltpu.roll(u8, sh, axis=0), 0.0)
            u8 = a8 * u_s + u8
            a8 = a8 * a_s
        h8 = u8 + a8 * carry
        h_dst[pl.ds(r0, 8), :] = h8
        return h8[0:1, :] if descending else h8[7:8, :]

    return lax.fori_loop(0, ng, grp, carry, unroll=2)


def _lru_fwd(name, zx, cw, cb, wa, wx, ba, bx, lam, tm, descending, hf=None, zg=None):
    t, w = zx.shape
    n = t // tm

    def body(step, ti, ins, ws, outs, accs, scr):
        zc, zp, zn, cw_r, cb_r, wa_r, wx_r, ba_r, bx_r, lam_r = ins[:10]
        carry_scr = scr[2]
        xe = _with_halo(zc[...], zp[...], zn[...], ti == 0, ti == n - 1)
        xc = _lru_conv(xe, cw_r[...], cb_r[...], tm)
        c = _neg_softplus_neg(lam_r[...])
        r, i, a, em = _lru_gates(xc, wa_r, wx_r, ba_r[...], bx_r[...], c)

        @pl.when(step == 0)
        def _():
            carry_scr[...] = jnp.zeros(carry_scr.shape, _F32)

        carry_scr[...] = _tile_scan(a, i * xc * jnp.sqrt(em), scr[:2], outs[0], carry_scr[...], tm, descending)
        if descending:
            hf_r, zg_r = ins[10:]
            outs[1][...] = ((hf_r[...] + outs[0][...]) * _gelu(zg_r[...].astype(_F32))).astype(_MM)

    ins = [_rows(zx, tm), _halo_prev(zx, tm), _halo_next(zx, tm), _full(cw), _full(cb), _full(wa), _full(wx),
           _full(ba), _full(bx), _full(lam)]
    outs = [_out_rows(t, w, _F32, tm)]
    if descending:
        ins += [_rows(hf, tm), _rows(zg, tm)]
        outs += [_out_rows(t, w, _MM, tm)]
    scratch = _scan_scratch(tm, w) + [pltpu.VMEM((1, w), _F32)]
    return _row_call(name, body, n, ins, outs, scratch=scratch, reverse=descending)


def _lru_bwd(name, zx, d_in, h_own, cw, cb, wa, wx, ba, bx, lam, tm, direction, zg=None, h_other=None,
             dxc_in=None, comm=None):
    t, w = zx.shape
    n = t // tm
    dh_ = w // _HEADS
    adj_desc = direction == 0

    def body(step, ti, ins, ws, outs, accs, scr):
        zc, zp, zn, din_r, ho_r, hh_r, cw_r, cb_r, wa_r, wx_r, ba_r, bx_r, lam_r = ins[:13]
        p_scr, carry_scr = scr[2:]
        first, last = ti == 0, ti == n - 1
        xe = _with_halo(zc[...], zp[...], zn[...], first, last)
        xc = _lru_conv(xe, cw_r[...], cb_r[...], tm)
        lam_v = lam_r[...]
        c = _neg_softplus_neg(lam_v)
        r, i, a, em = _lru_gates(xc, wa_r, wx_r, ba_r[...], bx_r[...], c)
        m = jnp.sqrt(em)
        if direction == 0:
            gel, gel_grad = _gelu_and_grad(ins[13][...].astype(_F32))
            dpa_v = din_r[...]
            d_h = dpa_v * gel
        else:
            d_h = din_r[...]
        @pl.when(step == 0)
        def _():
            carry_scr[...] = jnp.zeros(carry_scr.shape, _F32)

        carry_in = carry_scr[...]
        carry_scr[...] = _tile_scan(a, a * d_h, scr[:2], p_scr, carry_in, tm, adj_desc)
        p = p_scr[...]
        row = lax.broadcasted_iota(jnp.int32, (tm, w), 0)
        h_t = ho_r[...]
        if adj_desc:
            p_nb = jnp.where(row == tm - 1, carry_in, pltpu.roll(p, tm - 1, axis=0))
            edge = jnp.where(first, 0.0, hh_r[_HALO - 1:_HALO, :])
            h_nb = jnp.where(row == 0, edge, pltpu.roll(h_t, 1, axis=0))
        else:
            p_nb = jnp.where(row == 0, carry_in, pltpu.roll(p, 1, axis=0))
            edge = jnp.where(last, 0.0, hh_r[0:1, :])
            h_nb = jnp.where(row == tm - 1, edge, pltpu.roll(h_t, tm - 1, axis=0))
        g = d_h + p_nb
        gi = g * i
        d_i = g * xc * m
        dxc = gi * m
        d_m = gi * xc
        d_l = g * h_nb * a - d_m * (1.0 - em) / m
        accs[4][...] += jnp.sum(d_l * r, axis=0, keepdims=True)
        dga = d_l * c * r * (1.0 - r)
        dgx = d_i * i * (1.0 - i)
        accs[2][...] += jnp.sum(dga, axis=0, keepdims=True)
        accs[3][...] += jnp.sum(dgx, axis=0, keepdims=True)
        parts = []
        for hh in range(_HEADS):
            sl = slice(hh * dh_, (hh + 1) * dh_)
            xs = xc[:, sl].astype(_MM)
            da_h = dga[:, sl].astype(_MM)
            dx_h = dgx[:, sl].astype(_MM)
            accs[0][hh] += _dot_tn(xs, da_h)
            accs[1][hh] += _dot_tn(xs, dx_h)
            parts.append(_dot_nt(da_h, wa_r[hh]) + _dot_nt(dx_h, wx_r[hh]))
        dxc = dxc + jnp.concatenate(parts, axis=1)
        if direction == 0:
            outs[0][...] = dxc
            outs[1][...] = (dpa_v * (h_t + ins[14][...]) * gel_grad).astype(_MM)
            outs[2][...] = d_h
        else:
            outs[0][...] = dxc + ins[13][...]

        @pl.when(step == n - 1)
        def _():
            accs[4][...] = accs[4][...] * (_LRU_C * _sigmoid(-lam_v))

    halo_h = _halo_prev(h_own, tm) if adj_desc else _halo_next(h_own, tm)
    ins = [_rows(zx, tm), _halo_prev(zx, tm), _halo_next(zx, tm), _rows(d_in, tm), _rows(h_own, tm), halo_h,
           _full(cw), _full(cb), _full(wa), _full(wx), _full(ba), _full(bx), _full(lam)]
    outs = [_out_rows(t, w, _F32, tm)]
    if direction == 0:
        ins += [_rows(zg, tm), _rows(h_other, tm)]
        outs += [_out_rows(t, w, _MM, tm), _out_rows(t, w, _F32, tm)]
    else:
        ins += [_rows(dxc_in, tm)]
    accs = [(_HEADS, dh_, dh_), (_HEADS, dh_, dh_), (1, w), (1, w), (1, w)]
    scratch = _scan_scratch(tm, w) + [pltpu.VMEM((tm, w), _F32), pltpu.VMEM((1, w), _F32)]
    return _row_call(name, body, n, ins, outs, accs=accs, scratch=scratch, reverse=adj_desc, comm=comm)


def _lru_conv_bwd(name, dxc, zx, cw, tm):
    t, w = zx.shape
    n = t // tm

    def body(step, ti, ins, ws, outs, accs, scr):
        dc, dp, dn, zc, zp, zn, cw_r = ins
        first, last = ti == 0, ti == n - 1
        de = _with_halo(dc[...], dp[...], dn[...], first, last)
        ze = _with_halo(zc[...], zp[...], zn[...], first, last)
        cw_v = cw_r[...]
        d_cur = dc[...]
        dz = None
        for k in range(4):
            term = _shift_rows(de, 2 - k, tm) * cw_v[k:k + 1, :]
            dz = term if dz is None else dz + term
            accs[0][k:k + 1, :] += jnp.sum(d_cur * _shift_rows(ze, k - 2, tm), axis=0, keepdims=True)
        accs[1][...] += jnp.sum(d_cur, axis=0, keepdims=True)
        outs[0][...] = dz.astype(_MM)

    ins = [_rows(dxc, tm), _halo_prev(dxc, tm), _halo_next(dxc, tm),
           _rows(zx, tm), _halo_prev(zx, tm), _halo_next(zx, tm), _full(cw)]
    return _row_call(name, body, n, ins, [_out_rows(t, w, _MM, tm)], accs=[(4, w), (1, w)])


def _sgu_mix(v2, ws_ref, bias, mixed_scr, tm):
    gw = v2.shape[1]
    gh = gw // _HEADS
    for nn in range(tm // _CHUNK):
        rs = slice(nn * _CHUNK, (nn + 1) * _CHUNK)
        for g in range(_HEADS):
            cs = slice(g * gh, (g + 1) * gh)
            mixed_scr[rs, cs] = _dot(ws_ref[g], v2[rs, cs].astype(_MM)) + bias[:, cs]
    return mixed_scr[...]


def _ln_fwd(v1, lg, lb):
    mu = jnp.mean(v1, axis=-1, keepdims=True)
    vc = v1 - mu
    rs = lax.rsqrt(jnp.mean(vc * vc, axis=-1, keepdims=True) + _EPS)
    vn = vc * rs
    return vn * lg + lb, vn, rs


def _bc_fwd(name, zmid, scw, lg, lb, ws_mm, bias, sw, gw, tm):
    t = zmid.shape[0]
    n = t // tm

    def body(step, ti, ins, ws, outs, accs, scr):
        zc, zp, zn, scw_r, lg_r, lb_r, ws_r, bias_r = ins
        first, last = ti == 0, ti == n - 1
        z = zc[...].astype(_F32)
        zb, zcc, zxx = z[:, 0:sw], z[:, sw:2 * sw], z[:, 2 * sw:3 * sw]
        zu, zv = z[:, 3 * sw:3 * sw + gw], z[:, 3 * sw + gw:3 * sw + 2 * gw]
        zpv, znv = zp[...].astype(_F32), zn[...].astype(_F32)
        qe = _with_halo(zcc * zxx, zpv[:, sw:2 * sw] * zpv[:, 2 * sw:3 * sw],
                        znv[:, sw:2 * sw] * znv[:, 2 * sw:3 * sw], first, last)
        scw_v = scw_r[...]
        cq = None
        for k in range(3):
            term = _shift_rows(qe, k - 1, tm) * scw_v[k:k + 1, :]
            cq = term if cq is None else cq + term
        outs[0][...] = (zb * cq).astype(_MM)
        v2, _, _ = _ln_fwd(_gelu(zv), lg_r[...], lb_r[...])
        mixed = _sgu_mix(v2, ws_r, bias_r[...], scr[0], tm)
        outs[1][...] = (_gelu(zu) * mixed).astype(_MM)

    ins = [_rows(zmid, tm), _halo_prev(zmid, tm), _halo_next(zmid, tm), _full(scw), _full(lg), _full(lb),
           _full(ws_mm), _full(bias)]
    return _row_call(name, body, n, ins, [_out_rows(t, sw, _MM, tm), _out_rows(t, gw, _MM, tm)],
                     scratch=[pltpu.VMEM((tm, gw), _F32)])


def _bc_bwd(name, zmid, dpb, dpc, scw, lg, lb, ws_mm, wst_mm, bias, sw, gw, tm):
    t = zmid.shape[0]
    n = t // tm
    gh = gw // _HEADS

    def body(step, ti, ins, ws, outs, accs, scr):
        zc, zp, zn, db_c, db_p, db_n, dc_r, scw_r, lg_r, lb_r, ws_r, wst_r, bias_r = ins
        mixed_scr, dv2_scr = scr
        first, last = ti == 0, ti == n - 1
        z = zc[...].astype(_F32)
        zb, zcc, zxx = z[:, 0:sw], z[:, sw:2 * sw], z[:, 2 * sw:3 * sw]
        zu, zv = z[:, 3 * sw:3 * sw + gw], z[:, 3 * sw + gw:3 * sw + 2 * gw]
        zpv, znv = zp[...].astype(_F32), zn[...].astype(_F32)
        qe = _with_halo(zcc * zxx, zpv[:, sw:2 * sw] * zpv[:, 2 * sw:3 * sw],
                        znv[:, sw:2 * sw] * znv[:, 2 * sw:3 * sw], first, last)
        dpb_v = db_c[...]
        dcq = dpb_v * zb
        dcqe = _with_halo(dcq, db_p[...] * zpv[:, 0:sw], db_n[...] * znv[:, 0:sw], first, last)
        scw_v = scw_r[...]
        cq, dq = None, None
        for k in range(3):
            qk = _shift_rows(qe, k - 1, tm)
            term = qk * scw_v[k:k + 1, :]
            cq = term if cq is None else cq + term
            dterm = _shift_rows(dcqe, 1 - k, tm) * scw_v[k:k + 1, :]
            dq = dterm if dq is None else dq + dterm
            accs[0][k:k + 1, :] += jnp.sum(dcq * qk, axis=0, keepdims=True)
        outs[0][:, 0:sw] = (dpb_v * cq).astype(_MM)
        outs[0][:, sw:2 * sw] = (dq * zxx).astype(_MM)
        outs[0][:, 2 * sw:3 * sw] = (dq * zcc).astype(_MM)
        lg_v = lg_r[...]
        v2, vn, rs = _ln_fwd(_gelu(zv), lg_v, lb_r[...])
        mixed = _sgu_mix(v2, ws_r, bias_r[...], mixed_scr, tm)
        dpc_v = dc_r[...]
        outs[0][:, 3 * sw:3 * sw + gw] = (dpc_v * mixed * _gelu_grad(zu)).astype(_MM)
        dmix = dpc_v * _gelu(zu)
        for nn in range(tm // _CHUNK):
            rsl = slice(nn * _CHUNK, (nn + 1) * _CHUNK)
            accs[4][...] += dmix[rsl, :]
            for g in range(_HEADS):
                cs = slice(g * gh, (g + 1) * gh)
                dm_b = dmix[rsl, cs].astype(_MM)
                accs[3][g] += _dot_nt(dm_b, v2[rsl, cs].astype(_MM))
                dv2_scr[rsl, cs] = _dot(wst_r[g], dm_b)
        dv2 = dv2_scr[...]
        accs[1][...] += jnp.sum(dv2 * vn, axis=0, keepdims=True)
        accs[2][...] += jnp.sum(dv2, axis=0, keepdims=True)
        dvn = dv2 * lg_v
        dv1 = rs * (dvn - jnp.mean(dvn, axis=-1, keepdims=True)
                    - vn * jnp.mean(dvn * vn, axis=-1, keepdims=True))
        outs[0][:, 3 * sw + gw:3 * sw + 2 * gw] = (dv1 * _gelu_grad(zv)).astype(_MM)

    ins = [_rows(zmid, tm), _halo_prev(zmid, tm), _halo_next(zmid, tm),
           _rows(dpb, tm), _halo_prev(dpb, tm), _halo_next(dpb, tm), _rows(dpc, tm),
           _full(scw), _full(lg), _full(lb), _full(ws_mm), _full(wst_mm), _full(bias)]
    accs = [(3, sw), (1, gw), (1, gw), (_HEADS, _CHUNK, _CHUNK), (_CHUNK, gw)]
    return _row_call(name, body, n, ins, [_out_rows(t, 3 * sw + 2 * gw, _MM, tm)], accs=accs,
                     scratch=[pltpu.VMEM((tm, gw), _F32), pltpu.VMEM((tm, gw), _F32)])


def _mix_proj(name, pa, pb, pc, zm, wlo, wsc, wsg, tm):
    t = pa.shape[0]
    d = wlo.shape[1]

    def body(step, ti, ins, ws, outs, accs, scr):
        ys = [_dot(ins[k][...], ws[k][...]) for k in range(3)]
        gm = _sigmoid(ins[3][...].astype(_F32))
        m = None
        for k in range(3):
            outs[k][...] = ys[k].astype(_MM)
            term = gm[:, k * d:(k + 1) * d] * ys[k]
            m = term if m is None else m + term
        outs[3][...] = m.astype(_MM)

    return _row_call(name, body, t // tm, [_rows(pa, tm), _rows(pb, tm), _rows(pc, tm), _rows(zm, tm)],
                     [_out_rows(t, d, _MM, tm)] * 4, hbm=[wlo, wsc, wsg])


def _mix_bwd_out(name, dxo, mo, ya, yb, yc, zm, post_g, wo, tm):
    t, d = dxo.shape

    def body(step, ti, ins, ws, outs, accs, scr):
        dxo_ref, mo_ref, ya_r, yb_r, yc_r, zm_r, g_ref = ins
        dmo, dg = _rms_bwd(dxo_ref[...], mo_ref[...], g_ref[...])
        accs[0][...] += dg
        dmob = dmo.astype(_MM)
        outs[0][...] = dmob
        dm = _dot_nt(dmob, ws[0][...])
        gm = _sigmoid(zm_r[...].astype(_F32))
        for k, y_r in enumerate((ya_r, yb_r, yc_r)):
            gk = gm[:, k * d:(k + 1) * d]
            outs[1 + k][...] = (dm * gk).astype(_MM)
            outs[4][:, k * d:(k + 1) * d] = (dm * y_r[...].astype(_F32) * gk * (1.0 - gk)).astype(_MM)

    ins = [_rows(dxo, tm), _rows(mo, tm), _rows(ya, tm), _rows(yb, tm), _rows(yc, tm), _rows(zm, tm),
           _full(post_g)]
    return _row_call(name, body, t // tm, ins,
                     [_out_rows(t, d, _MM, tm)] * 4 + [_out_rows(t, 3 * d, _MM, tm)], accs=[(1, d)], hbm=[wo])


def _mix_bwd_proj(name, dya, dyb, dyc, wlo, wsc, wsg, tm):
    t = dya.shape[0]

    def body(step, ti, ins, ws, outs, accs, scr):
        for k in range(3):
            outs[k][...] = _dot_nt(ins[k][...], ws[k][...])

    return _row_call(name, body, t // tm, [_rows(dya, tm), _rows(dyb, tm), _rows(dyc, tm)],
                     [_out_rows(t, w.shape[0], _F32, tm) for w in (wlo, wsc, wsg)], hbm=[wlo, wsc, wsg])


def _loss_grad(name, y, target, tm):
    t, d = y.shape

    def body(step, ti, ins, ws, outs, accs, scr):
        err = ins[0][...] - ins[1][...]
        outs[0][...] = err * (1.0 / d)
        accs[0][...] += (0.5 / d) * jnp.sum(err * err)

    dy, acc = _row_call(name, body, t // tm, [_rows(y, tm), _rows(target, tm)], [_out_rows(t, d, _F32, tm)],
                        accs=[(1, 128)])
    return acc[0, 0], dy


def _adamw(w, g, m, v):
    m = _ADAM_B1 * m + (1.0 - _ADAM_B1) * g
    v = _ADAM_B2 * v + (1.0 - _ADAM_B2) * (g * g)
    m_hat = m / (1.0 - _ADAM_B1 ** _ADAM_STEP)
    v_hat = v / (1.0 - _ADAM_B2 ** _ADAM_STEP)
    delta = -_ADAM_LR * (m_hat / (jnp.sqrt(v_hat) + _ADAM_EPS) + _ADAM_WD * w)
    return delta, m, v


def _flat_tm(rows):
    return 512 if rows % 512 == 0 else rows


def _sum_adamw(name, parts, w, m, v):
    rows, c = w.shape
    tm = _flat_tm(rows)
    np_ = len(parts)

    def body(step, ti, ins, ws, outs, accs, scr):
        g = ins[0][...].astype(_F32)
        for k in range(1, np_):
            g = g + ins[k][...].astype(_F32)
        delta, nm, nv = _adamw(ins[np_][...], g, ins[np_ + 1][...], ins[np_ + 2][...])
        outs[0][...] = g
        outs[1][...] = delta
        outs[2][...] = nm
        outs[3][...] = nv

    return _row_call(name, body, rows // tm, list(parts) + [_rows(w, tm), _rows(m, tm), _rows(v, tm)],
                     [_out_rows(rows, c, _F32, tm)] * 4)


def _sum_parts(name, parts, rows, c, out_dtype):
    tm = _flat_tm(rows)

    def body(step, ti, ins, ws, outs, accs, scr):
        g = ins[0][...].astype(_F32)
        for k in range(1, len(ins)):
            g = g + ins[k][...].astype(_F32)
        outs[0][...] = g.astype(out_dtype)

    return _row_call(name, body, rows // tm, list(parts), [_out_rows(rows, c, out_dtype, tm)])[0]


def _pack(arrs, dtype):
    flat = jnp.concatenate([a.reshape(-1).astype(dtype) for a in arrs])
    return _to_rows(flat, _flat_rows(flat.shape[0]))


def _pack_rows(arrs, dtype):
    parts = [a.reshape(-1, _LANES).astype(dtype) for a in arrs]
    n = sum(p.shape[0] for p in parts)
    rows = _flat_rows(n * _LANES)
    if rows > n:
        parts.append(jnp.zeros((rows - n, _LANES), dtype))
    return jnp.concatenate(parts, axis=0)


def _unpack_rows(flat, shapes):
    out, off = [], 0
    for s in shapes:
        n = math.prod(s) // _LANES
        out.append(flat[off:off + n].reshape(tuple(s)))
        off += n
    return out


def _unpack(flat, shapes):
    v = flat.reshape(-1)
    out, off = [], 0
    for s in shapes:
        n = math.prod(s)
        out.append(v[off:off + n].reshape(tuple(s)))
        off += n
    return out


def _place():
    return lax.axis_index("x"), lax.axis_index("y"), lax.axis_index("c")


class _Exchange:
    def __init__(self, ins, outs, sems, start, finish):
        self.ins, self.outs, self.sems, self.start, self.finish = ins, outs, sems, start, finish


def _run_exchange(name, ex):
    n_in, n_out = len(ex.ins), len(ex.outs)

    def body(*refs):
        ins, outs, sems = refs[:n_in], refs[n_in:n_in + n_out], refs[n_in + n_out:]
        ex.start(ins, outs, sems)
        ex.finish(ins, outs, sems)

    any_spec = pl.BlockSpec(memory_space=pl.ANY)
    return pl.pallas_call(body, name=name, out_shape=list(ex.outs), in_specs=[any_spec] * n_in,
                          out_specs=[any_spec] * n_out, scratch_shapes=list(ex.sems))(*ex.ins)


def _gather_exchange(block):
    r, c_ = block.shape

    def copies(ins, outs, sems):
        x_ref, out_ref = ins[0], outs[0]
        send_sems, recv_sems, local_sem = sems
        x, y, c = _place()
        me, sibling = (x, y, c), (x, y, 1 - c)
        chips = [(1 - x, y), (x, 1 - y), (1 - x, 1 - y)]

        def rows(px, py, pc):
            return out_ref.at[4 * px + 2 * py + pc]

        def copy(k, blk, to, src=None):
            return pltpu.make_async_remote_copy(
                src_ref=rows(*blk) if src is None else src, dst_ref=rows(*blk),
                send_sem=send_sems.at[k], recv_sem=recv_sems.at[k], device_id=to, device_id_type=_MESH)

        mine = pltpu.make_async_copy(x_ref, rows(*me), local_sem)
        first = [copy(0, me, sibling, src=x_ref)]
        first += [copy(1 + j, me, (*chip, c), src=x_ref) for j, chip in enumerate(chips)]
        passed = [copy(4 + j, (*chip, c), sibling) for j, chip in enumerate(chips)]
        landed = [copy(1 + j, (*chip, c), me) for j, chip in enumerate(chips)]
        landed_later = [copy(0, sibling, me)] + [copy(4 + j, (*chip, 1 - c), me) for j, chip in enumerate(chips)]
        return mine, first, passed, landed, landed_later

    def start(ins, outs, sems):
        mine, first, _, _, _ = copies(ins, outs, sems)
        mine.start()
        for cp in first:
            cp.start()

    def finish(ins, outs, sems):
        mine, first, passed, landed, landed_later = copies(ins, outs, sems)
        for j in range(3):
            landed[j].wait_recv()
            passed[j].start()
        for cp in landed_later:
            cp.wait_recv()
        for cp in first + passed:
            cp.wait_send()
        mine.wait()

    return _Exchange([block], [jax.ShapeDtypeStruct((_NDEV, r, c_), block.dtype)],
                     [pltpu.SemaphoreType.DMA((7,)), pltpu.SemaphoreType.DMA((7,)), pltpu.SemaphoreType.DMA(())],
                     start, finish)


def _all_gather(name, block):
    return _run_exchange(name, _gather_exchange(block))[0]


def _sibling_exchange_of(dm):
    _, _, r, c_ = dm.shape

    def copies(ins, outs, sems):
        x, y, c = _place()
        return [pltpu.make_async_remote_copy(
            src_ref=ins[0].at[k, 1 - c], dst_ref=outs[0].at[k], send_sem=sems[0].at[k],
            recv_sem=sems[1].at[k], device_id=(x, y, 1 - c), device_id_type=_MESH) for k in range(4)]

    def start(ins, outs, sems):
        for cp in copies(ins, outs, sems):
            cp.start()

    def finish(ins, outs, sems):
        for cp in copies(ins, outs, sems):
            cp.wait()

    return _Exchange([dm], [jax.ShapeDtypeStruct((4, r, c_), dm.dtype)],
                     [pltpu.SemaphoreType.DMA((4,)), pltpu.SemaphoreType.DMA((4,))], start, finish)


def _pair_sum(name, dm, got, core):
    _, _, r, c_ = dm.shape
    tm = _flat_tm(r)

    def kern(core_ref, a_ref, b_ref, o_ref):
        o_ref[...] = (a_ref[...] + b_ref[...]).astype(o_ref.dtype)

    grid_spec = pltpu.PrefetchScalarGridSpec(
        num_scalar_prefetch=1, grid=(4, r // tm),
        in_specs=[pl.BlockSpec((None, None, tm, c_), lambda k, i, cr: (k, cr[0], i, 0)),
                  pl.BlockSpec((None, tm, c_), lambda k, i, cr: (k, i, 0))],
        out_specs=pl.BlockSpec((None, tm, c_), lambda k, i, cr: (k, i, 0)))
    return pl.pallas_call(
        kern, name=name, grid_spec=grid_spec, out_shape=jax.ShapeDtypeStruct((4, r, c_), _MM),
        compiler_params=pltpu.CompilerParams(dimension_semantics=("arbitrary", "arbitrary"),
                                             vmem_limit_bytes=_VMEM_LIMIT),
    )(core, dm, got)


def _chip_sum(name, p, others, chip):
    _, r, c_ = p.shape
    tm = _flat_tm(r)

    def kern(chip_ref, p_ref, o0, o1, o2, g_out):
        g_out[...] = (p_ref[...].astype(_F32) + o0[...].astype(_F32) + o1[...].astype(_F32)
                      + o2[...].astype(_F32))

    grid_spec = pltpu.PrefetchScalarGridSpec(
        num_scalar_prefetch=1, grid=(r // tm,),
        in_specs=[pl.BlockSpec((None, tm, c_), lambda i, cr: (cr[0], i, 0))]
        + [pl.BlockSpec((None, tm, c_), lambda i, cr, k=k: (k, i, 0)) for k in range(3)],
        out_specs=pl.BlockSpec((tm, c_), lambda i, cr: (i, 0)))
    return pl.pallas_call(
        kern, name=name, grid_spec=grid_spec, out_shape=jax.ShapeDtypeStruct((r, c_), _F32),
        compiler_params=pltpu.CompilerParams(dimension_semantics=("arbitrary",), vmem_limit_bytes=_VMEM_LIMIT),
    )(chip, p, others, others, others)


def _row_tile(rows, lanes):
    for d in range(min(rows, max(8, (1 << 18) // lanes)) // 8 * 8, 7, -8):
        if rows % d == 0:
            return d
    return rows


def _adamw_update(name, w, g, m, v):
    shape = w.shape
    as2d = lambda a: a.reshape(-1, shape[-1])
    rows = math.prod(shape[:-1])
    tm = _row_tile(rows, shape[-1])

    def body(step, ti, ins, ws, outs, accs, scr):
        delta, nm, nv = _adamw(ins[0][...], ins[1][...], ins[2][...], ins[3][...])
        outs[0][...] = delta
        outs[1][...] = nm
        outs[2][...] = nv

    res = _row_call(name, body, rows // tm, [_rows(as2d(a), tm) for a in (w, g, m, v)],
                    [_out_rows(rows, shape[-1], _F32, tm)] * 3)
    return [r.reshape(shape) for r in res]


def _chip_exchange_of(p):
    _, r, c_ = p.shape

    def copies(ins, outs, sems):
        x, y, c = _place()
        chips = [(1 - x, y), (x, 1 - y), (1 - x, 1 - y)]
        return [pltpu.make_async_remote_copy(
            src_ref=ins[0].at[2 * px + py], dst_ref=outs[0].at[j], send_sem=sems[0].at[j],
            recv_sem=sems[1].at[j], device_id=(px, py, c), device_id_type=_MESH)
            for j, (px, py) in enumerate(chips)]

    def start(ins, outs, sems):
        for cp in copies(ins, outs, sems):
            cp.start()

    def finish(ins, outs, sems):
        for cp in copies(ins, outs, sems):
            cp.wait()

    return _Exchange([p], [jax.ShapeDtypeStruct((3, r, c_), p.dtype)],
                     [pltpu.SemaphoreType.DMA((3,)), pltpu.SemaphoreType.DMA((3,))], start, finish)


def _gather_full(gathered, names, shard_shapes, axes, unpack):
    per_dev = [unpack(gathered[d], shard_shapes) for d in range(_NDEV)]
    return {nme: jnp.concatenate([per_dev[d][i] for d in range(_NDEV)], axis=ax)
            for i, (nme, ax) in enumerate(zip(names, axes))}


def _block_rows(a, ax, d):
    s = a.shape[ax] // _NDEV
    return lax.slice_in_dim(a, d * s, (d + 1) * s, axis=ax).reshape(-1, _LANES)


def _to_rows(flat, rows):
    return jnp.pad(flat, (0, rows * _LANES - flat.shape[0])).reshape(rows, _LANES)


def _flat_rows(n):
    rows = -(-n // _LANES)
    return -(-rows // 512) * 512 if rows >= 512 else -(-rows // 16) * 16


def _tm(t, want):
    return min(t, want)


def _ffn_forward(tag, x, pre_g, wg, wu, wd, post_g):
    t = x.shape[0]
    h, a, b, s = _ffn_up(tag + "_up", x, pre_g, wg, wu, _tm(t, 512))
    f, x_out = _proj_norm_res(tag + "_down", s, x, post_g, wd, 0.5, _tm(t, 512))
    return x_out, dict(x=x, h=h, a=a, b=b, s=s, f=f)


def _ffn_backward(tag, dxo, sv, pre_g, wg, wu, wd, post_g, comm=None):
    t = dxo.shape[0]
    da, db, df, dpost, *carried = _ffn_bwd_post(tag + "_bwd_post", dxo, sv['f'], sv['a'], sv['b'], post_g, wd,
                                                0.5, _tm(t, 512), comm=comm)
    dx, dpre = _bwd_in_norm(tag + "_bwd_pre", [da, db], [wg, wu], sv['x'], dxo, pre_g, _tm(t, 512))
    tk = _tm(t, 1024)
    grads = dict(pre_g=dpre[0], post_g=dpost[0],
                 w_gate=_xty(tag + "_dwg", sv['h'], da, tk), w_up=_xty(tag + "_dwu", sv['h'], db, tk),
                 w_down=_xty(tag + "_dwd", sv['s'], df, tk))
    return dx, grads, carried


def _mixer_weights(w):
    lw = w['lru_conv_w'].shape[-1]
    sw = w['sc_conv_w'].shape[-1]
    gw = w['sgu_ln_g'].shape[-1]
    win = w['w_in']
    cuts = [0, lw, 2 * lw, 2 * lw + 3 * sw + 2 * gw, win.shape[1]]
    p = dict(lw=lw, sw=sw, gw=gw,
             win=[win[:, cuts[k]:cuts[k + 1]] for k in range(4)],
             cw=w['lru_conv_w'], cb=w['lru_conv_b'][None, :],
             wa=w['lru_wa'], wx=w['lru_wx'],
             ba=w['lru_ba'], bx=w['lru_bx'], lam=w['lru_lambda'],
             wlo=w['lru_w_out'], scw=w['sc_conv_w'], wsc=w['sc_w_out'],
             lg=w['sgu_ln_g'][None, :], lb=w['sgu_ln_b'][None, :],
             ws=w['sgu_w_s'].astype(_MM), wst=jnp.swapaxes(w['sgu_w_s'], 1, 2).astype(_MM),
             bias=jnp.repeat(w['sgu_b'].T, gw // _HEADS, axis=1),
             wsg=w['sgu_w_out'], wo=w['w_o'],
             pre_g=w['mix_pre_g'][None, :], post_g=w['mix_post_g'][None, :])
    return p


def _mixer_forward(tag, x, p, comm=None):
    t = x.shape[0]
    tl = _tm(t, 512)
    hm, zg, zx, zmid, zm, *carried = _mix_in(tag + "_in", x, p['pre_g'], p['win'], _tm(t, 512), comm=comm)
    lru = lambda d: (p['cw'], p['cb'], p['wa'][d], p['wx'][d], p['ba'][d:d + 1], p['bx'][d:d + 1],
                     p['lam'][d:d + 1])
    hf, = _lru_fwd(tag + "_lru_f", zx, *lru(0), tl, False)
    hb, pa = _lru_fwd(tag + "_lru_b", zx, *lru(1), tl, True, hf=hf, zg=zg)
    pb, pc = _bc_fwd(tag + "_bc", zmid, p['scw'], p['lg'], p['lb'], p['ws'], p['bias'], p['sw'], p['gw'], tl)
    ya, yb, yc, m = _mix_proj(tag + "_proj", pa, pb, pc, zm, p['wlo'], p['wsc'], p['wsg'], _tm(t, 512))
    mo, x_out = _proj_norm_res(tag + "_out", m, x, p['post_g'], p['wo'], 1.0, _tm(t, 512))
    sv = dict(x=x, hm=hm, zg=zg, zx=zx, zmid=zmid, zm=zm, hf=hf, hb=hb, pa=pa, pb=pb, pc=pc,
              ya=ya, yb=yb, yc=yc, m=m, mo=mo)
    return x_out, sv, carried


def _mixer_backward(tag, dxo, sv, p, comm=None):
    t = dxo.shape[0]
    tl = _tm(t, 256)
    tk = _tm(t, 1024)
    dmo, dya, dyb, dyc, dzm, dpost = _mix_bwd_out(tag + "_bwd_out", dxo, sv['mo'], sv['ya'], sv['yb'], sv['yc'],
                                                  sv['zm'], p['post_g'], p['wo'], _tm(t, 512))
    dpa, dpb, dpc = _mix_bwd_proj(tag + "_bwd_proj", dya, dyb, dyc, p['wlo'], p['wsc'], p['wsg'], _tm(t, 512))
    dzmid, dscw, dlg, dlb, dws, dbias = _bc_bwd(tag + "_bc_bwd", sv['zmid'], dpb, dpc, p['scw'], p['lg'], p['lb'],
                                                p['ws'], p['wst'], p['bias'], p['sw'], p['gw'], tl)
    lru = lambda d: (p['cw'], p['cb'], p['wa'][d], p['wx'][d], p['ba'][d:d + 1], p['bx'][d:d + 1],
                     p['lam'][d:d + 1])
    dxc0, dzg, d_h, dwa0, dwx0, dba0, dbx0, dlam0, *carried = _lru_bwd(
        tag + "_lru_bwd_f", sv['zx'], dpa, sv['hf'], *lru(0), tl, 0, zg=sv['zg'], h_other=sv['hb'], comm=comm)
    dxc, dwa1, dwx1, dba1, dbx1, dlam1 = _lru_bwd(tag + "_lru_bwd_b", sv['zx'], d_h, sv['hb'], *lru(1), tl, 1,
                                                  dxc_in=dxc0)
    dzx, dcw, dcb = _lru_conv_bwd(tag + "_conv_bwd", dxc, sv['zx'], p['cw'], _tm(t, 512))
    dzs = [dzg, dzx, dzmid, dzm]
    dx, dpre = _bwd_in_norm(tag + "_bwd_in", dzs, p['win'], sv['x'], dxo, p['pre_g'], _tm(t, 512))
    gh = p['gw'] // _HEADS
    grads = dict(
        mix_pre_g=dpre[0], mix_post_g=dpost[0],
        w_in=jnp.concatenate([_xty(tag + "_dwin%d" % k, sv['hm'], dz, tk) for k, dz in enumerate(dzs)], axis=1),
        lru_conv_w=dcw, lru_conv_b=dcb[0],
        lru_wa=jnp.stack([dwa0, dwa1]), lru_wx=jnp.stack([dwx0, dwx1]),
        lru_ba=jnp.concatenate([dba0, dba1]), lru_bx=jnp.concatenate([dbx0, dbx1]),
        lru_lambda=jnp.concatenate([dlam0, dlam1]),
        lru_w_out=_xty(tag + "_dwlo", sv['pa'], dya, tk),
        sc_conv_w=dscw, sc_w_out=_xty(tag + "_dwsc", sv['pb'], dyb, tk),
        sgu_ln_g=dlg[0], sgu_ln_b=dlb[0], sgu_w_s=dws,
        sgu_b=jnp.sum(dbias.reshape(_CHUNK, _HEADS, gh), axis=2).T,
        sgu_w_out=_xty(tag + "_dwsg", sv['pc'], dyc, tk),
        w_o=_xty(tag + "_dwo", sv['m'], dmo, tk))
    return dx, grads, carried


def _forward_backward(x, target, depth, gathered, weights_of, gather_of, blocks_of, core):
    t = x.shape[0]
    saved = []
    for l in range(depth):
        w = weights_of(l, gathered)
        g = lambda nme: w[nme][None, :]
        tag = "l%d_" % l
        x, s1 = _ffn_forward(tag + "ffn1", x, g('ffn1_pre_g'), w['ffn1_w_gate'], w['ffn1_w_up'], w['ffn1_w_down'],
                             g('ffn1_post_g'))
        p = _mixer_weights(w)
        x, sm, carried = _mixer_forward(tag + "mix", x, p, comm=gather_of(l + 1) if l + 1 < depth else None)
        gathered = carried[0] if carried else None
        x, s2 = _ffn_forward(tag + "ffn2", x, g('ffn2_pre_g'), w['ffn2_w_gate'], w['ffn2_w_up'], w['ffn2_w_down'],
                             g('ffn2_post_g'))
        saved.append((s1, sm, s2, p, w))
    loss, dx = _loss_grad("loss", x, target, _tm(t, 512))
    per_layer, reduced, above = [], [], None
    for l in reversed(range(depth)):
        s1, sm, s2, p, w = saved[l]
        g = lambda nme: w[nme][None, :]
        tag = "l%d_" % l
        grads = {}
        dx, g2, got = _ffn_backward(tag + "ffn2", dx, s2, g('ffn2_pre_g'), w['ffn2_w_gate'], w['ffn2_w_up'],
                                    w['ffn2_w_down'], g('ffn2_post_g'),
                                    comm=_sibling_exchange_of(above) if above is not None else None)
        grads.update({'ffn2_' + k: v for k, v in g2.items()})
        pair = _pair_sum(tag + "reduce_pair_sum", above, got[0], core) if above is not None else None
        dx, gm, others = _mixer_backward(tag + "mix", dx, sm, p,
                                         comm=_chip_exchange_of(pair) if pair is not None else None)
        grads.update(gm)
        if pair is not None:
            reduced.append((pair, others[0]))
        dx, g1, _ = _ffn_backward(tag + "ffn1", dx, s1, g('ffn1_pre_g'), w['ffn1_w_gate'], w['ffn1_w_up'],
                                  w['ffn1_w_down'], g('ffn1_post_g'))
        grads.update({'ffn1_' + k: v for k, v in g1.items()})
        per_layer.append(grads)
        above = blocks_of(grads)
    per_layer.reverse()
    reduced.reverse()
    return loss, dx, per_layer, reduced, above


def kernel(x, ffn1_pre_g, ffn1_w_gate, ffn1_w_up, ffn1_w_down, ffn1_post_g, mix_pre_g, w_in, lru_conv_w, lru_conv_b, lru_wa, lru_ba, lru_wx, lru_bx, lru_lambda, lru_w_out, sc_conv_w, sc_w_out, sgu_ln_g, sgu_ln_b, sgu_w_s, sgu_b, sgu_w_out, w_o, mix_post_g, ffn2_pre_g, ffn2_w_gate, ffn2_w_up, ffn2_w_down, ffn2_post_g, loss_target, m_ffn1_pre_g, m_ffn1_w_gate, m_ffn1_w_up, m_ffn1_w_down, m_ffn1_post_g, m_mix_pre_g, m_w_in, m_lru_conv_w, m_lru_conv_b, m_lru_wa, m_lru_ba, m_lru_wx, m_lru_bx, m_lru_lambda, m_lru_w_out, m_sc_conv_w, m_sc_w_out, m_sgu_ln_g, m_sgu_ln_b, m_sgu_w_s, m_sgu_b, m_sgu_w_out, m_w_o, m_mix_post_g, m_ffn2_pre_g, m_ffn2_w_gate, m_ffn2_w_up, m_ffn2_w_down, m_ffn2_post_g, v_ffn1_pre_g, v_ffn1_w_gate, v_ffn1_w_up, v_ffn1_w_down, v_ffn1_post_g, v_mix_pre_g, v_w_in, v_lru_conv_w, v_lru_conv_b, v_lru_wa, v_lru_ba, v_lru_wx, v_lru_bx, v_lru_lambda, v_lru_w_out, v_sc_conv_w, v_sc_w_out, v_sgu_ln_g, v_sgu_ln_b, v_sgu_w_s, v_sgu_b, v_sgu_w_out, v_w_o, v_mix_post_g, v_ffn2_pre_g, v_ffn2_w_gate, v_ffn2_w_up, v_ffn2_w_down, v_ffn2_post_g):
    args = locals()
    wts = {n: args[n] for n in _WEIGHTS}
    mom = {n: args['m_' + n] for n in _WEIGHTS}
    var = {n: args['v_' + n] for n in _WEIGHTS}
    cx, cy, cc = _place()
    dev = 4 * cx + 2 * cy + cc
    big, small = list(_BIG), list(_SMALL_SHARDED)

    depth = w_in.shape[0]
    core = jnp.reshape(cc, (1,)).astype(jnp.int32)
    chip = jnp.reshape(2 * cx + cy, (1,)).astype(jnp.int32)
    layer_shapes = [wts[n].shape[1:] for n in big]
    rows = _flat_rows(sum(math.prod(s) for s in layer_shapes))

    g_small = _all_gather("gather_vectors", _pack([wts[n] for n in small], _F32))
    vecs = dict(wts)
    vecs.update(_gather_full(g_small, small, [wts[n].shape for n in small], [_SMALL_SHARDED[n] for n in small],
                             _unpack))

    def layer_block(l):
        return _pack_rows([wts[n][l] for n in big], _MM)

    def weights_of(l, gathered):
        w = {n: vecs[n][l] for n in _WEIGHTS if n not in _BIG}
        w.update(_gather_full(gathered, big, layer_shapes, [_BIG[n] - 1 for n in big], _unpack_rows))
        return w

    def blocks_of(grads):
        pieces = []
        for d in range(_NDEV):
            blocks = [_block_rows(grads[n], _BIG[n] - 1, d) for n in big]
            fill = rows - sum(b.shape[0] for b in blocks)
            pieces += blocks + ([jnp.zeros((fill, _LANES), _F32)] if fill else [])
        return jnp.concatenate(pieces, axis=0).reshape(4, 2, rows, _LANES)

    loss, grad_x, grads, reduced, blocks0 = _forward_backward(
        x[0], loss_target[0], depth, _all_gather("l0_gather_matrices", layer_block(0)), weights_of,
        lambda l: _gather_exchange(layer_block(l)), blocks_of, core)
    loss = lax.psum(loss, ("x", "y", "c"))

    got0, = _run_exchange("l0_reduce_sibling", _sibling_exchange_of(blocks0))
    pair0 = _pair_sum("l0_reduce_pair_sum", blocks0, got0, core)
    others0, = _run_exchange("l0_reduce_chips", _chip_exchange_of(pair0))
    per_layer = [_unpack_rows(_chip_sum("l%d_reduce_final_sum" % l, pair, others, chip), layer_shapes)
                 for l, (pair, others) in enumerate([(pair0, others0)] + reduced)]
    out = {}
    for i, n in enumerate(big):
        g = jnp.stack([per_layer[l][i] for l in range(depth)])
        out['grad_' + n] = g
        out['delta_' + n], out['new_m_' + n], out['new_v_' + n] = _adamw_update("update_" + n, wts[n], g, mom[n],
                                                                                 var[n])

    vec = _REPLICATED + small
    gvec = {n: jnp.stack([g[n] for g in grads]) for n in vec}
    part = _pack([gvec[n] for n in vec], _F32)
    allp = _all_gather("gather_vector_grads", part)
    rv = part.shape[0]
    tmv = _flat_tm(rv)
    gsum = _sum_parts("reduce_vector_grads", [_rows3(allp, k, tmv) for k in range(_NDEV)], rv, _LANES, _F32)
    gfull = dict(zip(vec, _unpack(gsum, [gvec[n].shape for n in vec])))
    gloc = []
    for n in vec:
        if n in _SMALL_SHARDED:
            ax = _SMALL_SHARDED[n]
            sz = wts[n].shape[ax]
            gloc.append(lax.dynamic_slice_in_dim(gfull[n], dev * sz, sz, axis=ax))
        else:
            gloc.append(gfull[n])
    gl = _pack(gloc, _F32)
    g_s, d_s, m_s, v_s = _sum_adamw("update_vectors", [_rows(gl, _flat_tm(gl.shape[0]))],
                                    _pack([wts[n] for n in vec], _F32), _pack([mom[n] for n in vec], _F32),
                                    _pack([var[n] for n in vec], _F32))
    shapes_s = [wts[n].shape for n in vec]
    for key, flat in (('grad_', g_s), ('delta_', d_s), ('new_m_', m_s), ('new_v_', v_s)):
        for n, a in zip(vec, _unpack(flat, shapes_s)):
            out[key + n] = a

    res = [loss, grad_x[None]]
    for key in ('grad_', 'delta_', 'new_m_', 'new_v_'):
        res += [out[key + n] for n in _WEIGHTS]
    return tuple(res)
```
